```python
import math
import jax, jax.numpy as jnp
from jax import lax
import numpy as np

D_MODEL = 2048
BATCH = 4
SEQ = 2048
DEPTH = 2
DEC_BATCH = 128
DEC_SEQ = 1
PAST_LEN = 16384
PAGE_SIZE = 128

N_MIXERS = 2
N_NSA = (DEPTH + 1) // 2
N_MLA = DEPTH // 2
NSA_HEADS = 16
NSA_HD = 128
CMP_BLOCK = 32
CMP_STRIDE = 16
SLC_BLOCK = 64
N_SEL = 16
WINDOW = 512
NSA_Q_COLS = NSA_HEADS * NSA_HD
NSA_KV_SLOTS = 6
NSA_IN_COLS = NSA_Q_COLS + NSA_KV_SLOTS * NSA_HD + 3 * NSA_HEADS
NSA_SCALE = NSA_HD ** -0.5
MLA_HEADS = 16
Q_LORA = 768
KV_LORA = 512
QK_NOPE = 128
QK_ROPE = 64
V_HD = 128
MLA_IN_COLS = Q_LORA + KV_LORA + QK_ROPE
MLA_ROW = KV_LORA + QK_ROPE
MLA_SCALE = (QK_NOPE + QK_ROPE) ** -0.5
ROPE_THETA = 10000.0
T5_BUCKETS = 32
T5_MAX_DIST = 128
D_FF = -(-8 * D_MODEL // (3 * 256)) * 256
PE_DIM = 256
LN_EPS = 1e-5
RMS_EPS = 1e-6
DN_ALPHA = (2 * DEPTH) ** 0.25
DN_BETA = (8 * DEPTH) ** -0.25
QB = 128
NEG_INF = -1e30
FORCE_SCORE = 1e4
F32 = jnp.float32

kernel_name = "nsa_mla_hybrid_step"


def layer_norm(x, g, b):
    xf = x.astype(F32)
    mu = jnp.mean(xf, -1, keepdims=True)
    var = jnp.mean(jnp.square(xf - mu), -1, keepdims=True)
    return ((xf - mu) * lax.rsqrt(var + LN_EPS) * g + b).astype(x.dtype)


def rms_norm(x, g):
    xf = x.astype(F32)
    return (xf * lax.rsqrt(jnp.mean(xf * xf, -1, keepdims=True) + RMS_EPS) * g).astype(x.dtype)


def masked_softmax(logits, valid):
    p = jax.nn.softmax(jnp.where(valid, logits, NEG_INF), axis=-1)
    return jnp.where(valid, p, 0.0)


def t5_bucket(dist):
    n = jnp.maximum(dist, 0)
    max_exact = T5_BUCKETS // 2
    nf = jnp.maximum(n, 1).astype(F32)
    large = max_exact + (jnp.log(nf / max_exact) / math.log(T5_MAX_DIST / max_exact)
                         * (T5_BUCKETS - max_exact)).astype(jnp.int32)
    large = jnp.minimum(large, T5_BUCKETS - 1)
    return jnp.where(n < max_exact, n, large)


def t5_bias(table, dist):
    return jnp.moveaxis(table[t5_bucket(dist)], -1, -2).astype(F32)


def biased_attend(q, k, v, qpos, kpos, valid, t5_table):
    logits = jnp.einsum('...qhd,...kd->...qhk', q, k).astype(F32) * NSA_SCALE
    logits = logits + t5_bias(t5_table, qpos[..., :, None] - kpos[..., None, :])
    p = masked_softmax(logits, valid[..., :, None, :])
    return jnp.einsum('...qhk,...kd->...qhd', p.astype(v.dtype), v), p


def nsa_project(x, w_in):
    lead = x.shape[:-1]
    h = x @ w_in
    q = h[..., :NSA_Q_COLS].reshape(*lead, NSA_HEADS, NSA_HD)
    kv = h[..., NSA_Q_COLS:NSA_Q_COLS + NSA_KV_SLOTS * NSA_HD].reshape(*lead, NSA_KV_SLOTS, NSA_HD)
    gates = jax.nn.sigmoid(h[..., NSA_Q_COLS + NSA_KV_SLOTS * NSA_HD:].astype(F32))
    gates = gates.reshape(*lead, 3, NSA_HEADS).astype(x.dtype)
    return q, kv, gates


def nsa_compress(rows, a, w_phi):
    lk = rows.shape[0]
    lp = -(-lk // SLC_BLOCK) * SLC_BLOCK
    rows = jnp.pad(rows, ((0, lp - lk), (0, 0)))
    chunks = rows.reshape(lp // CMP_STRIDE, CMP_STRIDE, NSA_HD)
    first = jnp.einsum('cjd,jd->cd', chunks, a[:CMP_STRIDE])
    second = jnp.einsum('cjd,jd->cd', chunks, a[CMP_STRIDE:])
    return (first[:-1] + second[1:]) @ w_phi


def nsa_cmp_slc(q, qpos, k_cmp_rows, v_cmp_rows, get_slc, n_slc_blocks, cmp_a, cmp_phi, t5_table):
    k_cmp = nsa_compress(k_cmp_rows, cmp_a[0], cmp_phi[0])
    v_cmp = nsa_compress(v_cmp_rows, cmp_a[1], cmp_phi[1])
    n_cmp = k_cmp.shape[0]
    cmp_end = jnp.arange(n_cmp) * CMP_STRIDE + (CMP_BLOCK - 1)
    n_sel = min(N_SEL, n_slc_blocks)
    blk_ids = jnp.arange(n_slc_blocks)
    tq = q.shape[0]
    qb = QB if tq % QB == 0 else tq

    def block_fn(args):
        qk, pk = args
        o_cmp, p_cmp = biased_attend(qk, k_cmp, v_cmp, pk, cmp_end,
                                     pk[:, None] >= cmp_end[None, :], t5_table)
        imp = jnp.sum(p_cmp, axis=1)
        chunk_imp = jnp.pad(imp, ((0, 0), (0, 1))) + jnp.pad(imp, ((0, 0), (1, 0)))
        slc_imp = chunk_imp.reshape(qb, n_slc_blocks, SLC_BLOCK // CMP_STRIDE).sum(-1)
        cur = pk[:, None] // SLC_BLOCK
        forced = (blk_ids == 0) | (blk_ids == cur) | (blk_ids == cur - 1)
        score = jnp.where(blk_ids > cur, -1.0, jnp.where(forced, FORCE_SCORE, slc_imp))
        _, idx = lax.top_k(score, n_sel)
        k_sel, v_sel = get_slc(idx)
        kpos = (idx[..., None] * SLC_BLOCK + jnp.arange(SLC_BLOCK)).reshape(qb, n_sel * SLC_BLOCK)
        o_slc, _ = biased_attend(qk[:, None], k_sel.reshape(qb, n_sel * SLC_BLOCK, NSA_HD),
                                 v_sel.reshape(qb, n_sel * SLC_BLOCK, NSA_HD), pk[:, None], kpos,
                                 (pk[:, None] >= kpos)[:, None, :], t5_table)
        return o_cmp, o_slc[:, 0]

    nqb = tq // qb
    o_cmp, o_slc = lax.map(block_fn, (q.reshape(nqb, qb, NSA_HEADS, NSA_HD), qpos.reshape(nqb, qb)))
    return o_cmp.reshape(tq, NSA_HEADS, NSA_HD), o_slc.reshape(tq, NSA_HEADS, NSA_HD)


def nsa_prompt_seq(q, kv, cmp_a, cmp_phi, t5_table):
    t = q.shape[0]
    ns = t // SLC_BLOCK
    k_blocks = kv[:, 2].reshape(ns, SLC_BLOCK, NSA_HD)
    v_blocks = kv[:, 3].reshape(ns, SLC_BLOCK, NSA_HD)

    def get_slc(idx):
        return k_blocks[idx], v_blocks[idx]

    return nsa_cmp_slc(q, jnp.arange(t), kv[:, 0], kv[:, 1], get_slc, ns, cmp_a, cmp_phi, t5_table)


def nsa_window_prompt(q, k_win, v_win, t5_table):
    b, t = q.shape[:2]
    nqb = t // QB
    span = QB + WINDOW
    idx = jnp.arange(nqb)[:, None] * QB + jnp.arange(span)[None, :]
    pad = ((0, 0), (WINDOW, 0), (0, 0))
    kb = jnp.pad(k_win, pad)[:, idx].swapaxes(0, 1)
    vb = jnp.pad(v_win, pad)[:, idx].swapaxes(0, 1)
    kpos = idx - WINDOW
    qpos = jnp.arange(t).reshape(nqb, QB)
    qb = q.reshape(b, nqb, QB, NSA_HEADS, NSA_HD).swapaxes(0, 1)

    def block_fn(args):
        qk, kk, vk, qp, kp = args
        dist = qp[:, None] - kp[None, :]
        valid = (dist >= 0) & (dist <= WINDOW) & (kp[None, :] >= 0)
        o, _ = biased_attend(qk, kk, vk, qp, kp, valid, t5_table)
        return o

    o = lax.map(block_fn, (qb, kb, vb, qpos, kpos))
    return o.swapaxes(0, 1).reshape(b, t, NSA_HEADS, NSA_HD)


def nsa_window_sample(q, kv_win_new, win_buf, t5_table):
    w_buf = win_buf.shape[1]
    s = q.shape[1]
    rows = jnp.concatenate([win_buf, kv_win_new], axis=1)
    kpos = PAST_LEN - w_buf + jnp.arange(w_buf + s)
    qpos = PAST_LEN + jnp.arange(s)
    dist = qpos[:, None] - kpos[None, :]
    valid = (dist >= 0) & (dist <= WINDOW)
    o, _ = biased_attend(q, rows[:, :, 0], rows[:, :, 1], qpos, kpos, valid, t5_table)
    return o, rows[:, -w_buf:]


def nsa_combine(gates, o_cmp, o_slc, o_win):
    o = (gates[..., 0, :, None] * o_cmp + gates[..., 1, :, None] * o_slc
         + gates[..., 2, :, None] * o_win)
    return o.reshape(*o.shape[:-2], NSA_HEADS * NSA_HD)


def rope_angles(pos):
    inv = ROPE_THETA ** (-jnp.arange(0, QK_ROPE, 2, dtype=F32) / QK_ROPE)
    ang = pos.astype(F32)[:, None] * inv[None, :]
    return jnp.cos(ang), jnp.sin(ang)


def apply_rope(x, cos, sin):
    x1, x2 = jnp.split(x, 2, axis=-1)
    c = cos.astype(x.dtype)
    s = sin.astype(x.dtype)
    return jnp.concatenate([x1 * c - x2 * s, x1 * s + x2 * c], axis=-1)


def mla_project(x, pos, w_in, g_q, g_kv, w_q_b, w_kv_b):
    lead = x.shape[:-1]
    h = x @ w_in
    c_q = rms_norm(h[..., :Q_LORA], g_q)
    c_kv = rms_norm(h[..., Q_LORA:Q_LORA + KV_LORA], g_kv)
    k_r = h[..., Q_LORA + KV_LORA:]
    q = (c_q @ w_q_b).reshape(*lead, MLA_HEADS, QK_NOPE + QK_ROPE)
    cos, sin = rope_angles(pos)
    q_rope = apply_rope(q[..., QK_NOPE:], cos[:, None, :], sin[:, None, :])
    k_rope = apply_rope(k_r, cos, sin)
    w_uk = w_kv_b.reshape(KV_LORA, MLA_HEADS, QK_NOPE + V_HD)[..., :QK_NOPE]
    q_lat = jnp.einsum('...thd,chd->...thc', q[..., :QK_NOPE], w_uk)
    rows = jnp.concatenate([c_kv, k_rope], axis=-1)
    return q_lat, q_rope, rows


def mla_attend(q_lat, q_rope, kv_rows, qpos, kpos):
    ckv = kv_rows[..., :KV_LORA]
    kr = kv_rows[..., KV_LORA:]
    logits = (jnp.einsum('...qhc,...kc->...qhk', q_lat, ckv)
              + jnp.einsum('...qhr,...kr->...qhk', q_rope, kr)).astype(F32) * MLA_SCALE
    valid = qpos[..., :, None] >= kpos[..., None, :]
    p = masked_softmax(logits, valid[..., :, None, :])
    return jnp.einsum('...qhk,...kc->...qhc', p.astype(ckv.dtype), ckv)


def mla_output(o_lat, w_kv_b, w_o):
    w_uv = w_kv_b.reshape(KV_LORA, MLA_HEADS, QK_NOPE + V_HD)[..., QK_NOPE:]
    o = jnp.einsum('...thc,chd->...thd', o_lat, w_uv)
    return o.reshape(*o.shape[:-2], MLA_HEADS * V_HD) @ w_o


def swiglu(x, w_in, w_out):
    h = x @ w_in
    return (jax.nn.silu(h[..., :D_FF]) * h[..., D_FF:]) @ w_out


def setup_inputs(seed: int = 0) -> dict:
    key = jax.random.key(seed)
    ks = jax.random.split(key, 32)
    n_pages = PAST_LEN // PAGE_SIZE
    n_pool = (DEC_BATCH * n_pages * 5) // 4
    w_buf = min(WINDOW, PAST_LEN)

    def nrm(k, shape, scale):
        return jax.random.normal(k, shape, F32) * scale

    page_table = jax.random.permutation(ks[7], n_pool)[:DEC_BATCH * n_pages]
    page_table = page_table.reshape(DEC_BATCH, n_pages).astype(jnp.int32)
    return {
        "x_prompt": nrm(ks[0], (BATCH, SEQ, D_MODEL), 1.0),
        "x_sample": nrm(ks[1], (DEC_BATCH, DEC_SEQ, D_MODEL), 1.0),
        "p_prompt": nrm(ks[2], (DEPTH, BATCH, SEQ, PE_DIM), 1.0),
        "p_sample": nrm(ks[3], (DEPTH, DEC_BATCH, DEC_SEQ, PE_DIM), 1.0),
        "cache_nsa_kv": nrm(ks[4], (N_NSA, n_pool, PAGE_SIZE, 4, NSA_HD), 1.0),
        "state_nsa_win": nrm(ks[5], (N_NSA, DEC_BATCH, w_buf, 2, NSA_HD), 1.0),
        "cache_mla_kv": nrm(ks[6], (N_MLA, n_pool, PAGE_SIZE, MLA_ROW), 1.0),
        "page_table": page_table,
        "t5_table": nrm(ks[8], (T5_BUCKETS, NSA_HEADS), 0.5),
        "ln_g": 1.0 + nrm(ks[9], (DEPTH, 2, D_MODEL), 0.05),
        "ln_b": nrm(ks[10], (DEPTH, 2, D_MODEL), 0.02),
        "nsa_w_in": nrm(ks[11], (N_NSA, D_MODEL, NSA_IN_COLS), D_MODEL ** -0.5),
        "nsa_cmp_a": (1.0 + nrm(ks[12], (N_NSA, 2, CMP_BLOCK, NSA_HD), 0.1)) * CMP_BLOCK ** -0.5,
        "nsa_cmp_phi": nrm(ks[13], (N_NSA, 2, NSA_HD, NSA_HD), NSA_HD ** -0.5),
        "nsa_w_o": nrm(ks[14], (N_NSA, NSA_Q_COLS, D_MODEL), NSA_Q_COLS ** -0.5 * DN_BETA),
        "mla_w_in": nrm(ks[15], (N_MLA, D_MODEL, MLA_IN_COLS), D_MODEL ** -0.5),
        "mla_q_norm": 1.0 + nrm(ks[16], (N_MLA, Q_LORA), 0.05),
        "mla_kv_norm": 1.0 + nrm(ks[17], (N_MLA, KV_LORA), 0.05),
        "mla_w_q_b": nrm(ks[18], (N_MLA, Q_LORA, MLA_HEADS * (QK_NOPE + QK_ROPE)), Q_LORA ** -0.5),
        "mla_w_kv_b": nrm(ks[19], (N_MLA, KV_LORA, MLA_HEADS * (QK_NOPE + V_HD)), KV_LORA ** -0.5),
        "mla_w_o": nrm(ks[20], (N_MLA, MLA_HEADS * V_HD, D_MODEL), (MLA_HEADS * V_HD) ** -0.5 * DN_BETA),
        "ffn_w_in": nrm(ks[21], (DEPTH, D_MODEL, 2 * D_FF), D_MODEL ** -0.5),
        "ffn_w_out": nrm(ks[22], (DEPTH, D_FF, D_MODEL), D_FF ** -0.5 * DN_BETA),
        "ple_w_gate": nrm(ks[23], (DEPTH, D_MODEL, D_MODEL), D_MODEL ** -0.5),
        "ple_w_proj": nrm(ks[24], (DEPTH, PE_DIM, D_MODEL), PE_DIM ** -0.5),
    }


def reference(x_prompt, x_sample, p_prompt, p_sample, cache_nsa_kv, state_nsa_win, cache_mla_kv,
              page_table, t5_table, ln_g, ln_b, nsa_w_in, nsa_cmp_a, nsa_cmp_phi, nsa_w_o,
              mla_w_in, mla_q_norm, mla_kv_norm, mla_w_q_b, mla_w_kv_b, mla_w_o,
              ffn_w_in, ffn_w_out, ple_w_gate, ple_w_proj):
    n_pages = page_table.shape[1]
    past_len = n_pages * PAGE_SIZE
    bpp = PAGE_SIZE // SLC_BLOCK

    def nsa_prompt(x, li):
        q, kv, gates = nsa_project(x, nsa_w_in[li])
        o_cmp, o_slc = jax.vmap(nsa_prompt_seq, in_axes=(0, 0, None, None, None))(
            q, kv, nsa_cmp_a[li], nsa_cmp_phi[li], t5_table)
        o_win = nsa_window_prompt(q, kv[:, :, 4], kv[:, :, 5], t5_table)
        out = nsa_combine(gates, o_cmp, o_slc, o_win) @ nsa_w_o[li]
        w_buf = state_nsa_win.shape[2]
        win_rows = jnp.pad(kv[:, :, 4:6], ((0, 0), (w_buf, 0), (0, 0), (0, 0)))[:, -w_buf:]
        return out, kv[:, :, :4], win_rows

    def nsa_sample(x, li):
        s = x.shape[1]
        q, kv, gates = nsa_project(x, nsa_w_in[li])
        npb = past_len // SLC_BLOCK
        n_new = -(-s // SLC_BLOCK)
        qpos = past_len + jnp.arange(s)

        def per_seq(args):
            qs, kvs, pt_row = args
            past = cache_nsa_kv[li, pt_row[:, None], jnp.arange(PAGE_SIZE)[None, :], :2]
            rows = jnp.concatenate([past.reshape(past_len, 2, NSA_HD), kvs[:, :2]], axis=0)
            new_blocks = jnp.pad(kvs[:, 2:4], ((0, n_new * SLC_BLOCK - s), (0, 0), (0, 0)))
            new_blocks = new_blocks.reshape(n_new, SLC_BLOCK, 2, NSA_HD)

            def get_slc(idx):
                from_past = idx < npb
                pi = jnp.minimum(idx, npb - 1)
                page = pt_row[pi // bpp]
                offs = (pi % bpp)[..., None] * SLC_BLOCK + jnp.arange(SLC_BLOCK)
                past_blk = cache_nsa_kv[li, page[..., None], offs, 2:4]
                new_blk = new_blocks[jnp.clip(idx - npb, 0, n_new - 1)]
                blk = jnp.where(from_past[..., None, None, None], past_blk, new_blk)
                return blk[..., 0, :], blk[..., 1, :]

            return nsa_cmp_slc(qs, qpos, rows[:, 0], rows[:, 1], get_slc, npb + n_new,
                               nsa_cmp_a[li], nsa_cmp_phi[li], t5_table)

        o_cmp, o_slc = lax.map(per_seq, (q, kv, page_table))
        o_win, new_win = nsa_window_sample(q, kv[:, :, 4:6], state_nsa_win[li], t5_table)
        out = nsa_combine(gates, o_cmp, o_slc, o_win) @ nsa_w_o[li]
        return out, kv[:, :, :4], new_win

    def mla_prompt(x, li):
        b, t = x.shape[:2]
        pos = jnp.arange(t)
        q_lat, q_rope, rows = mla_project(x, pos, mla_w_in[li], mla_q_norm[li], mla_kv_norm[li],
                                          mla_w_q_b[li], mla_w_kv_b[li])
        nqb = t // QB
        qlb = q_lat.reshape(b, nqb, QB, MLA_HEADS, KV_LORA).swapaxes(0, 1)
        qrb = q_rope.reshape(b, nqb, QB, MLA_HEADS, QK_ROPE).swapaxes(0, 1)

        def block_fn(args):
            ql, qr, qp = args
            return mla_attend(ql, qr, rows, qp, pos)

        o_lat = lax.map(block_fn, (qlb, qrb, pos.reshape(nqb, QB)))
        o_lat = o_lat.swapaxes(0, 1).reshape(b, t, MLA_HEADS, KV_LORA)
        return mla_output(o_lat, mla_w_kv_b[li], mla_w_o[li]), rows

    def mla_sample(x, li):
        s = x.shape[1]
        qpos = past_len + jnp.arange(s)
        kpos = jnp.arange(past_len + s)
        q_lat, q_rope, rows = mla_project(x, qpos, mla_w_in[li], mla_q_norm[li], mla_kv_norm[li],
                                          mla_w_q_b[li], mla_w_kv_b[li])

        def per_seq(args):
            ql, qr, rn, pt_row = args
            past = cache_mla_kv[li, pt_row].reshape(past_len, MLA_ROW)
            return mla_attend(ql, qr, jnp.concatenate([past, rn], axis=0), qpos, kpos)

        o_lat = lax.map(per_seq, (q_lat, q_rope, rows, page_table))
        return mla_output(o_lat, mla_w_kv_b[li], mla_w_o[li]), rows

    def run_trunk(x, p, is_prompt):
        nsa_kv_rows, nsa_win_rows, mla_rows = [], [], []
        for i in range(DEPTH):
            li = i // N_MIXERS
            if i % N_MIXERS == 0:
                mixed, kv_rows, win_rows = (nsa_prompt if is_prompt else nsa_sample)(x, li)
                nsa_kv_rows.append(kv_rows)
                nsa_win_rows.append(win_rows)
            else:
                mixed, rows = (mla_prompt if is_prompt else mla_sample)(x, li)
                mla_rows.append(rows)
            x = layer_norm(DN_ALPHA * x + mixed, ln_g[i, 0], ln_b[i, 0])
            x = layer_norm(DN_ALPHA * x + swiglu(x, ffn_w_in[i], ffn_w_out[i]), ln_g[i, 1], ln_b[i, 1])
            x = x + jax.nn.sigmoid(x @ ple_w_gate[i]) * (p[i] @ ple_w_proj[i])
        return x, jnp.stack(nsa_kv_rows), jnp.stack(nsa_win_rows), jnp.stack(mla_rows)

    y_prompt, nsa_kv_prompt, nsa_win_prompt, mla_kv_prompt = run_trunk(x_prompt, p_prompt, True)
    y_sample, nsa_kv_sample, nsa_win_sample, mla_kv_sample = run_trunk(x_sample, p_sample, False)
    return (y_prompt, y_sample, nsa_kv_prompt, nsa_win_prompt, mla_kv_prompt,
            nsa_kv_sample, nsa_win_sample, mla_kv_sample)
```

```python
import functools
import math

import numpy as np
import jax
import jax.numpy as jnp
from jax import lax
from jax.experimental import pallas as pl
from jax.experimental.pallas import tpu as pltpu

F32 = jnp.float32
MXU_DT = jnp.bfloat16
HI = lax.Precision.HIGHEST

HEADS = 16
HD = 128
CMP_STRIDE = 16
CMP_BLOCK = 32
SLC_BLOCK = 64
N_SEL = 16
WINDOW = 512
NSA_KV_SLOTS = 6
NSA_CACHE_SLOTS = 4
NSA_SCALE = HD ** -0.5
Q_LORA = 768
KV_LORA = 512
QK_ROPE = 64
MLA_ROW = KV_LORA + QK_ROPE
MLA_SCALE = (HD + QK_ROPE) ** -0.5
ROPE_THETA = 10000.0
T5_BUCKETS = 32
T5_MAX_DIST = 128
LN_EPS = 1e-5
RMS_EPS = 1e-6
NEG_INF = -1e30
FORCE_SCORE = 1e4
QB = 128
LANES = 128

VMEM_LIMIT = 56 * 1024 * 1024


def _cparams(n_axes):
    return pltpu.CompilerParams(dimension_semantics=("arbitrary",) * n_axes,
                                vmem_limit_bytes=VMEM_LIMIT)


def _dot(a, b):
    return jnp.dot(a, b, preferred_element_type=F32)


def _dot_nt(a, b):
    return lax.dot_general(a, b, (((1,), (1,)), ((), ())), preferred_element_type=F32)


def _dot_hi(a, b):
    return jnp.dot(a, b, precision=HI, preferred_element_type=F32)


def _sigmoid(x):
    return 1.0 / (1.0 + jnp.exp(-x))


def _mm_kernel(x_ref, w_ref, o_ref, xb_ref):
    @pl.when(pl.program_id(1) == 0)
    def _():
        xb_ref[...] = x_ref[...].astype(MXU_DT)
    o_ref[...] = _dot(xb_ref[...], w_ref[...].astype(MXU_DT)).astype(o_ref.dtype)


def matmul(x, w, *, tn, out_dtype=F32, tm=1024):
    m, k = x.shape
    n = w.shape[1]
    tm = min(tm, m)
    return pl.pallas_call(
        _mm_kernel,
        grid=(pl.cdiv(m, tm), pl.cdiv(n, tn)),
        in_specs=[pl.BlockSpec((tm, k), lambda i, j: (i, 0)),
                  pl.BlockSpec((k, tn), lambda i, j: (0, j))],
        out_specs=pl.BlockSpec((tm, tn), lambda i, j: (i, j)),
        out_shape=jax.ShapeDtypeStruct((m, n), out_dtype),
        scratch_shapes=[pltpu.VMEM((tm, k), MXU_DT)],
        compiler_params=_cparams(2),
        name="matmul",
    )(x, w)


def _swiglu_kernel(x_ref, w1_ref, w2_ref, o_ref, xb_ref):
    @pl.when(pl.program_id(1) == 0)
    def _():
        xb_ref[...] = x_ref[...].astype(MXU_DT)
    xb = xb_ref[...]
    h1 = _dot(xb, w1_ref[...].astype(MXU_DT))
    h2 = _dot(xb, w2_ref[...].astype(MXU_DT))
    o_ref[...] = (h1 * _sigmoid(h1) * h2).astype(o_ref.dtype)


def swiglu_in(x, w_in, *, tn=512, tm=1024):
    m, k = x.shape
    d_ff = w_in.shape[1] // 2
    assert d_ff % tn == 0
    nff = d_ff // tn
    tm = min(tm, m)
    return pl.pallas_call(
        _swiglu_kernel,
        grid=(pl.cdiv(m, tm), nff),
        in_specs=[pl.BlockSpec((tm, k), lambda i, j: (i, 0)),
                  pl.BlockSpec((k, tn), lambda i, j: (0, j)),
                  pl.BlockSpec((k, tn), lambda i, j: (0, j + nff))],
        out_specs=pl.BlockSpec((tm, tn), lambda i, j: (i, j)),
        out_shape=jax.ShapeDtypeStruct((m, d_ff), MXU_DT),
        scratch_shapes=[pltpu.VMEM((tm, k), MXU_DT)],
        compiler_params=_cparams(2),
        name="swiglu_in",
    )(x, w_in, w_in)


def _mm_ln_kernel(x_ref, w_ref, r_ref, g_ref, b_ref, o_ref, acc_ref, *, alpha, nk):
    k = pl.program_id(1)

    @pl.when(k == 0)
    def _():
        acc_ref[...] = jnp.zeros_like(acc_ref)

    acc_ref[...] += _dot(x_ref[...].astype(MXU_DT), w_ref[...].astype(MXU_DT))

    @pl.when(k == nk - 1)
    def _():
        y = alpha * r_ref[...] + acc_ref[...]
        mu = jnp.mean(y, axis=-1, keepdims=True)
        yc = y - mu
        var = jnp.mean(yc * yc, axis=-1, keepdims=True)
        o_ref[...] = yc * lax.rsqrt(var + LN_EPS) * g_ref[...] + b_ref[...]


def matmul_residual_ln(x, w, resid, g, b, *, alpha, tk=512, tm=512):
    m, k = x.shape
    n = w.shape[1]
    tm = min(tm, m)
    assert k % tk == 0
    nk = k // tk
    return pl.pallas_call(
        functools.partial(_mm_ln_kernel, alpha=alpha, nk=nk),
        grid=(pl.cdiv(m, tm), nk),
        in_specs=[pl.BlockSpec((tm, tk), lambda i, kk: (i, kk)),
                  pl.BlockSpec((tk, n), lambda i, kk: (kk, 0)),
                  pl.BlockSpec((tm, n), lambda i, kk: (i, 0)),
                  pl.BlockSpec((1, n), lambda i, kk: (0, 0)),
                  pl.BlockSpec((1, n), lambda i, kk: (0, 0))],
        out_specs=pl.BlockSpec((tm, n), lambda i, kk: (i, 0)),
        out_shape=jax.ShapeDtypeStruct((m, n), F32),
        scratch_shapes=[pltpu.VMEM((tm, n), F32)],
        compiler_params=_cparams(2),
        name="matmul_residual_ln",
    )(x, w, resid, g.reshape(1, n), b.reshape(1, n))


def _ple_kernel(x_ref, wg_ref, p_ref, wp_ref, xt_ref, o_ref, xb_ref):
    @pl.when(pl.program_id(1) == 0)
    def _():
        xb_ref[...] = x_ref[...].astype(MXU_DT)
    gate = _sigmoid(_dot(xb_ref[...], wg_ref[...].astype(MXU_DT)))
    proj = _dot(p_ref[...].astype(MXU_DT), wp_ref[...].astype(MXU_DT))
    o_ref[...] = xt_ref[...] + gate * proj


def ple(x, w_gate, p, w_proj, *, tn=512, tm=1024):
    m, k = x.shape
    n = w_gate.shape[1]
    pe = p.shape[1]
    tm = min(tm, m)
    return pl.pallas_call(
        _ple_kernel,
        grid=(pl.cdiv(m, tm), pl.cdiv(n, tn)),
        in_specs=[pl.BlockSpec((tm, k), lambda i, j: (i, 0)),
                  pl.BlockSpec((k, tn), lambda i, j: (0, j)),
                  pl.BlockSpec((tm, pe), lambda i, j: (i, 0)),
                  pl.BlockSpec((pe, tn), lambda i, j: (0, j)),
                  pl.BlockSpec((tm, tn), lambda i, j: (i, j))],
        out_specs=pl.BlockSpec((tm, tn), lambda i, j: (i, j)),
        out_shape=jax.ShapeDtypeStruct((m, n), F32),
        scratch_shapes=[pltpu.VMEM((tm, k), MXU_DT)],
        compiler_params=_cparams(2),
        name="ple",
    )(x, w_gate, p, w_proj, x)


def _t5_bucket_np(dist):
    n = np.maximum(np.asarray(dist, np.int64), 0)
    max_exact = T5_BUCKETS // 2
    nf = np.maximum(n, 1).astype(np.float32)
    scaled = (np.log(nf / np.float32(max_exact)) / np.float32(math.log(T5_MAX_DIST / max_exact))
              * np.float32(T5_BUCKETS - max_exact))
    large = np.minimum(max_exact + scaled.astype(np.int32), T5_BUCKETS - 1)
    return np.where(n < max_exact, n, large).astype(np.int32)


def _bias_kernel(tab_ref, ids_ref, o_ref):
    ids = ids_ref[...]
    for h in range(HEADS):
        acc = jnp.zeros(ids.shape, F32)
        for bkt in range(T5_BUCKETS):
            acc = jnp.where(ids == bkt, tab_ref[bkt, h], acc)
        o_ref[h] = acc


def t5_bias_lookup(table, ids, *, tr=256):
    r = ids.shape[0]
    assert r % tr == 0
    return pl.pallas_call(
        _bias_kernel,
        grid=(r // tr,),
        in_specs=[pl.BlockSpec(memory_space=pltpu.SMEM),
                  pl.BlockSpec((tr, LANES), lambda i: (i, 0))],
        out_specs=pl.BlockSpec((HEADS, tr, LANES), lambda i: (0, i, 0)),
        out_shape=jax.ShapeDtypeStruct((HEADS, r, LANES), F32),
        compiler_params=_cparams(1),
        name="t5_bias_lookup",
    )(table, ids)


def _softmax_rows(s):
    m = jnp.max(s, axis=-1, keepdims=True)
    e = jnp.exp(s - m)
    l = jnp.sum(e, axis=-1, keepdims=True)
    return jnp.where(m > 0.5 * NEG_INF, e / l, 0.0)


def _band_attention(qh, k_scr, v_scr, s_scr, *, i, lo, far_bias, bias_prev, bias_diag,
                    far_mask, tri):
    f0 = jnp.zeros((QB, QB), F32)

    def far_score(kt, mx):
        s = _dot_nt(qh, k_scr[kt]) + far_bias + far_mask(kt)
        s_scr[kt] = s
        return jnp.maximum(mx, s)

    far_hi = jnp.maximum(i - 1, lo)
    mx = lax.fori_loop(lo, far_hi, far_score, f0 + NEG_INF)
    kp = jnp.maximum(i - 1, 0)
    s_prev = _dot_nt(qh, k_scr[kp]) + bias_prev + far_mask(kp) + jnp.where(i > 0, 0.0, NEG_INF)
    s_diag = _dot_nt(qh, k_scr[i]) + bias_diag + far_mask(i) + tri
    mx = jnp.maximum(mx, jnp.maximum(s_prev, s_diag))
    m = jnp.max(mx, axis=-1, keepdims=True)

    def far_pv(kt, carry):
        ls, acc = carry
        p = jnp.exp(s_scr[kt] - m)
        return ls + p, acc + _dot(p.astype(MXU_DT), v_scr[kt])

    ls, acc = lax.fori_loop(lo, far_hi, far_pv, (f0, jnp.zeros((QB, v_scr.shape[-1]), F32)))
    p_prev = jnp.exp(s_prev - m)
    p_diag = jnp.exp(s_diag - m)
    ls = ls + p_prev + p_diag
    acc = acc + _dot(p_prev.astype(MXU_DT), v_scr[kp]) + _dot(p_diag.astype(MXU_DT), v_scr[i])
    return acc / jnp.sum(ls, axis=-1, keepdims=True)


def _rope_lanes(x, cos_t, sin_t):
    lane = lax.broadcasted_iota(jnp.int32, x.shape, 1)
    first_half = (lane % QK_ROPE) < (QK_ROPE // 2)
    rot = jnp.where(first_half, pltpu.roll(x, LANES - QK_ROPE // 2, 1), pltpu.roll(x, QK_ROPE // 2, 1))
    return x * cos_t + rot * sin_t


def _nsa_prompt_kernel(tab_ref, q_ref, g_ref, kcr_ref, vcr_ref, ksr_ref, vsr_ref, kwr_ref, vwr_ref,
                       a_ref, phi_ref, bc_ref, bd_ref, amat_ref, emat_ref, o_ref,
                       kc_scr, vc_scr, ks_scr, vs_scr, kw_scr, vw_scr, q_scr, s_scr, a_scr,
                       oc_scr, os_scr, ow_scr, *, nt, n_blk):
    i = pl.program_id(1)
    n_cmp = nt * (QB // CMP_STRIDE) - 1

    @pl.when(i == 0)
    def _():
        for which, (rows_ref, dst) in enumerate(((kcr_ref, kc_scr), (vcr_ref, vc_scr))):
            ng = nt * QB // CMP_STRIDE
            first = jnp.zeros((ng, HD), F32)
            second = jnp.zeros((ng, HD), F32)
            for j in range(CMP_STRIDE):
                xj = rows_ref[pl.ds(j, ng, stride=CMP_STRIDE), :]
                first = first + xj * a_ref[which, j:j + 1, :]
                second = second + xj * a_ref[which, CMP_STRIDE + j:CMP_STRIDE + j + 1, :]
            pre = first + pltpu.roll(second, ng - 1, 0)
            dst[...] = _dot_hi(pre, phi_ref[which]).astype(MXU_DT)
        for kt in range(nt):
            rows = slice(kt * QB, (kt + 1) * QB)
            ks_scr[kt] = ksr_ref[rows, :].astype(MXU_DT)
            vs_scr[kt] = vsr_ref[rows, :].astype(MXU_DT)
            kw_scr[kt] = kwr_ref[rows, :].astype(MXU_DT)
            vw_scr[kt] = vwr_ref[rows, :].astype(MXU_DT)

    for h in range(HEADS):
        q_scr[h] = (q_ref[:, h * HD:(h + 1) * HD] * NSA_SCALE).astype(MXU_DT)

    row = lax.broadcasted_iota(jnp.int32, (QB, QB), 0)
    lane = lax.broadcasted_iota(jnp.int32, (QB, QB), 1)
    qpos = i * QB + row
    tri = jnp.where(lane <= row, 0.0, NEG_INF)
    cmp_valid = (qpos >= lane * CMP_STRIDE + (CMP_BLOCK - 1)) & (lane < n_cmp)
    cmp_mask = jnp.where(cmp_valid, 0.0, NEG_INF)

    def cmp_head(h, imp):
        s = _dot_nt(q_scr[h], kc_scr[...]) + bc_ref[0, h] + cmp_mask
        p = _softmax_rows(s)
        oc_scr[h] = _dot(p.astype(MXU_DT), vc_scr[...])
        return imp + p

    imp = lax.fori_loop(0, HEADS, cmp_head, jnp.zeros((QB, QB), F32))

    slc_imp = _dot_hi(imp, amat_ref[...])
    cur = qpos // SLC_BLOCK
    forced = (lane == 0) | (lane == cur) | (lane == cur - 1)
    score = jnp.where(lane > cur, -1.0, jnp.where(forced, FORCE_SCORE, slc_imp))
    score = jnp.where(lane < n_blk, score, -2.0)
    rank = jnp.zeros((QB, QB), F32)
    for b2 in range(n_blk):
        col = score[:, b2:b2 + 1]
        beats = (col > score) | ((col == score) & (lane > b2))
        rank = rank + jnp.where(beats, 1.0, 0.0)
    sel = jnp.where((rank < min(N_SEL, n_blk)) & (lane < n_blk), 1.0, 0.0)
    sel_keys = _dot(sel.astype(MXU_DT), emat_ref[...])
    for kt in range(nt):
        a_scr[kt] = jnp.where(sel_keys[:, kt * QB:(kt + 1) * QB] > 0.5, 0.0, NEG_INF)

    win_lo = jnp.maximum(i - WINDOW // QB, 0)
    band = jnp.where(lane >= row, 0.0, NEG_INF)

    def slc_win_head(h, carry):
        far = tab_ref[T5_BUCKETS - 1, h]
        qh = q_scr[h]
        os_scr[h] = _band_attention(
            qh, ks_scr, vs_scr, s_scr, i=i, lo=0, far_bias=far, bias_prev=bd_ref[h, 0],
            bias_diag=bd_ref[h, 1], far_mask=lambda kt: a_scr[kt], tri=tri)
        ow_scr[h] = _band_attention(
            qh, kw_scr, vw_scr, s_scr, i=i, lo=win_lo, far_bias=far, bias_prev=bd_ref[h, 0],
            bias_diag=bd_ref[h, 1],
            far_mask=lambda kt: jnp.where(kt == i - WINDOW // QB, band, 0.0), tri=tri)
        return carry

    lax.fori_loop(0, HEADS, slc_win_head, 0)

    gates = _sigmoid(g_ref[...])
    for h in range(HEADS):
        o = (gates[:, h:h + 1] * oc_scr[h] + gates[:, HEADS + h:HEADS + h + 1] * os_scr[h]
             + gates[:, 2 * HEADS + h:2 * HEADS + h + 1] * ow_scr[h])
        o_ref[:, h * HD:(h + 1) * HD] = o.astype(o_ref.dtype)


def nsa_prompt_attention(h, t5_table, cmp_a, cmp_phi, bias_c, bias_d, *, batch, seq):
    nt = seq // QB
    n_blk = seq // SLC_BLOCK
    assert seq // CMP_STRIDE == QB and n_blk <= QB and WINDOW % QB == 0
    q_cols = HEADS * HD
    kv_blk0 = q_cols // HD
    gate_blk = kv_blk0 + NSA_KV_SLOTS
    c = np.arange(QB)[:, None]
    b = np.arange(QB)[None, :]
    ratio = SLC_BLOCK // CMP_STRIDE
    amat = (((c >= ratio * b) & (c <= ratio * b + ratio - 1)).astype(np.float32)
            + ((c + 1 >= ratio * b) & (c + 1 <= ratio * b + ratio - 1)).astype(np.float32))
    amat = amat * (b < n_blk)
    emat = (np.arange(seq)[None, :] // SLC_BLOCK == np.arange(QB)[:, None]).astype(np.float32)

    kv_spec = lambda s: pl.BlockSpec((seq, HD), lambda bb, i, s=s: (bb, kv_blk0 + s))
    full = lambda shape: pl.BlockSpec(shape, lambda bb, i: (0,) * len(shape))
    tile_scr = lambda: pltpu.VMEM((nt, QB, HD), MXU_DT)
    head_scr = lambda dt: pltpu.VMEM((HEADS, QB, HD), dt)
    return pl.pallas_call(
        functools.partial(_nsa_prompt_kernel, nt=nt, n_blk=n_blk),
        grid=(batch, nt),
        in_specs=[pl.BlockSpec(memory_space=pltpu.SMEM),
                  pl.BlockSpec((QB, q_cols), lambda bb, i: (bb * nt + i, 0)),
                  pl.BlockSpec((QB, LANES), lambda bb, i: (bb * nt + i, gate_blk))]
                 + [kv_spec(s) for s in range(NSA_KV_SLOTS)]
                 + [full((2, CMP_BLOCK, HD)), full((2, HD, HD)),
                    pl.BlockSpec((1, HEADS, QB, QB), lambda bb, i: (i, 0, 0, 0)),
                    full((HEADS, 2, QB, QB)), full((QB, QB)), full((QB, seq))],
        out_specs=pl.BlockSpec((QB, q_cols), lambda bb, i: (bb * nt + i, 0)),
        out_shape=jax.ShapeDtypeStruct((batch * seq, q_cols), MXU_DT),
        scratch_shapes=[pltpu.VMEM((QB, HD), MXU_DT), pltpu.VMEM((QB, HD), MXU_DT),
                        tile_scr(), tile_scr(), tile_scr(), tile_scr(),
                        head_scr(MXU_DT), pltpu.VMEM((nt, QB, QB), F32), pltpu.VMEM((nt, QB, QB), F32),
                        head_scr(F32), head_scr(F32), head_scr(F32)],
        compiler_params=_cparams(2),
        name="nsa_prompt_attention",
    )(t5_table, h, h, h, h, h, h, h, h, cmp_a, cmp_phi, bias_c, bias_d,
      jnp.asarray(amat), jnp.asarray(emat, MXU_DT))


def _nsa_sample_cmp_kernel(pt_ref, q_ref, a_ref, phi_ref, bias_ref, cache_ref, oc_ref, imp_ref,
                           buf, sem, kc_scr, vc_scr, *, n_chunks, pages, past_len):
    b = pl.program_id(0)
    nb = pl.num_programs(0)
    prow = LANES * NSA_CACHE_SLOTS
    rows = pages * prow
    tail = CMP_STRIDE * NSA_CACHE_SLOTS
    groups = pages * LANES // CMP_STRIDE

    def chunk_copies(bb, ch, slot):
        cps = []
        for p in range(pages):
            page = pt_ref[bb, ch * pages + p]
            cps.append(pltpu.make_async_copy(cache_ref.at[page],
                                             buf.at[slot, pl.ds(p * prow, prow), :], sem.at[slot]))
        if ch + 1 < n_chunks:
            page = pt_ref[bb, (ch + 1) * pages]
            cps.append(pltpu.make_async_copy(cache_ref.at[page, pl.ds(0, tail), :],
                                             buf.at[slot, pl.ds(rows, tail), :], sem.at[slot]))
        return cps

    @pl.when(b == 0)
    def _():
        for cp in chunk_copies(0, 0, 0):
            cp.start()

    for ch in range(n_chunks):
        slot = ch % 2
        if ch + 1 < n_chunks:
            for cp in chunk_copies(b, ch + 1, 1 - slot):
                cp.start()
        else:
            @pl.when(b + 1 < nb)
            def _():
                for cp in chunk_copies(b + 1, 0, 1 - slot):
                    cp.start()
        for cp in chunk_copies(b, ch, slot):
            cp.wait()
        if ch == n_chunks - 1:
            buf[slot, rows:rows + tail, :] = jnp.zeros((tail, HD), F32)
        for kv, dst in enumerate((kc_scr, vc_scr)):
            pre = jnp.zeros((groups, HD), F32)
            for j in range(CMP_BLOCK):
                pre = pre + (buf[slot, pl.ds(NSA_CACHE_SLOTS * j + kv, groups,
                                             stride=CMP_STRIDE * NSA_CACHE_SLOTS), :]
                             * a_ref[kv, j:j + 1, :])
            dst[ch * groups:(ch + 1) * groups, :] = _dot_hi(pre, phi_ref[kv]).astype(MXU_DT)

    n_tok = n_chunks * groups
    q = (q_ref[0] * NSA_SCALE).astype(MXU_DT)
    tok = lax.broadcasted_iota(jnp.int32, (HEADS, n_tok), 1)
    valid = tok * CMP_STRIDE + (CMP_BLOCK - 1) <= past_len
    s = _dot_nt(q, kc_scr[...]) + bias_ref[...] + jnp.where(valid, 0.0, NEG_INF)
    p = _softmax_rows(s)
    oc_ref[0] = _dot(p.astype(MXU_DT), vc_scr[...])
    imp_ref[0] = jnp.sum(p, axis=0, keepdims=True)


def nsa_sample_compressed(q, page_table, cache_rows, cmp_a, cmp_phi, bias_c, *, past_len, pages=16):
    db, n_pages = page_table.shape
    assert n_pages % pages == 0
    n_chunks = n_pages // pages
    assert n_chunks % 2 == 0
    rows = (pages * LANES + CMP_STRIDE) * NSA_CACHE_SLOTS
    n_tok = n_pages * LANES // CMP_STRIDE
    grid_spec = pltpu.PrefetchScalarGridSpec(
        num_scalar_prefetch=1,
        grid=(db,),
        in_specs=[pl.BlockSpec((1, HEADS, HD), lambda b, pt: (b, 0, 0)),
                  pl.BlockSpec((2, CMP_BLOCK, HD), lambda b, pt: (0, 0, 0)),
                  pl.BlockSpec((2, HD, HD), lambda b, pt: (0, 0, 0)),
                  pl.BlockSpec((HEADS, n_tok), lambda b, pt: (0, 0)),
                  pl.BlockSpec(memory_space=pl.ANY)],
        out_specs=[pl.BlockSpec((1, HEADS, HD), lambda b, pt: (b, 0, 0)),
                   pl.BlockSpec((1, 1, n_tok), lambda b, pt: (b, 0, 0))],
        scratch_shapes=[pltpu.VMEM((2, rows, HD), F32),
                        pltpu.SemaphoreType.DMA((2,)),
                        pltpu.VMEM((n_tok, HD), MXU_DT), pltpu.VMEM((n_tok, HD), MXU_DT)])
    return pl.pallas_call(
        functools.partial(_nsa_sample_cmp_kernel, n_chunks=n_chunks, pages=pages, past_len=past_len),
        grid_spec=grid_spec,
        out_shape=[jax.ShapeDtypeStruct((db, HEADS, HD), F32),
                   jax.ShapeDtypeStruct((db, 1, n_tok), F32)],
        compiler_params=_cparams(1),
        name="nsa_sample_compressed",
    )(page_table, q, cmp_a, cmp_phi, bias_c, cache_rows)


def _select_kernel(imp_ref, amat_ref, idx_ref, *, npb, n_pick):
    slc_imp = _dot_hi(imp_ref[...], amat_ref[...])
    lane = lax.broadcasted_iota(jnp.int32, slc_imp.shape, 1).astype(F32)
    forced = (lane == 0.0) | (lane == float(npb - 1))
    score = jnp.where(forced, FORCE_SCORE, slc_imp)
    score = jnp.where(lane < float(npb), score, -2.0)
    out_lane = lax.broadcasted_iota(jnp.int32, idx_ref.shape, 1)
    out = jnp.zeros(idx_ref.shape, F32)
    for t in range(n_pick):
        best = jnp.max(score, axis=-1, keepdims=True)
        first = jnp.min(jnp.where(score == best, lane, 1e9), axis=-1, keepdims=True)
        out = jnp.where(out_lane == t, first, out)
        score = jnp.where(lane == first, -3.0, score)
    idx_ref[...] = out.astype(jnp.int32)


def nsa_sample_select(imp, *, npb):
    db, n_tok = imp.shape
    assert npb >= N_SEL
    nbp = -(-npb // LANES) * LANES
    c = np.arange(n_tok)[:, None]
    b = np.arange(nbp)[None, :]
    ratio = SLC_BLOCK // CMP_STRIDE
    amat = (((c >= ratio * b) & (c <= ratio * b + ratio - 1)).astype(np.float32)
            + ((c + 1 >= ratio * b) & (c + 1 <= ratio * b + ratio - 1)).astype(np.float32))
    amat = amat * (b < npb)
    return pl.pallas_call(
        functools.partial(_select_kernel, npb=npb, n_pick=N_SEL - 1),
        out_shape=jax.ShapeDtypeStruct((db, LANES), jnp.int32),
        compiler_params=pltpu.CompilerParams(vmem_limit_bytes=VMEM_LIMIT),
        name="nsa_sample_select",
    )(imp, jnp.asarray(amat))


def _nsa_sample_slc_win_kernel(idx_ref, pt_ref, q_ref, kvn_ref, win_ref, bsl_ref, b0_ref, bw_ref,
                               cache_ref, osl_ref, owin_ref, kbuf, sem, *, n_pick, npb):
    b = pl.program_id(0)
    nb = pl.num_programs(0)
    bpp = LANES // SLC_BLOCK
    brow = SLC_BLOCK * NSA_CACHE_SLOTS

    def block_copies(bb, slot):
        cps = []
        for t in range(n_pick):
            blk = idx_ref[bb, t]
            page = pt_ref[bb, blk // bpp]
            off = pl.multiple_of((blk % bpp) * brow, brow)
            cps.append(pltpu.make_async_copy(cache_ref.at[page, pl.ds(off, brow), :],
                                             kbuf.at[slot, pl.ds(t * brow, brow), :], sem.at[slot]))
        return cps

    slot = b % 2

    @pl.when(b == 0)
    def _():
        for cp in block_copies(0, 0):
            cp.start()

    @pl.when(b + 1 < nb)
    def _():
        for cp in block_copies(b + 1, 1 - slot):
            cp.start()

    for cp in block_copies(b, slot):
        cp.wait()

    n_keys = (n_pick + 1) * SLC_BLOCK
    kvn = kvn_ref[0]
    rowi = lax.broadcasted_iota(jnp.int32, (SLC_BLOCK, HD), 0)

    def gathered(cache_slot):
        parts = [kbuf[slot, pl.ds(t * brow + cache_slot, SLC_BLOCK, stride=NSA_CACHE_SLOTS), :]
                 for t in range(n_pick)]
        new_row = kvn[:, cache_slot * HD:(cache_slot + 1) * HD]
        parts.append(jnp.where(rowi == 0, new_row, 0.0))
        return jnp.concatenate(parts, axis=0).astype(MXU_DT)

    q = (q_ref[0] * NSA_SCALE).astype(MXU_DT)
    keys = gathered(2)
    vals = gathered(3)
    lane = lax.broadcasted_iota(jnp.int32, (HEADS, LANES), 1)

    def slot_bias(t):
        if t == n_pick:
            return b0_ref[...]
        blk = idx_ref[b, t]
        return jnp.where(blk == npb - 1, bsl_ref[0], jnp.where(blk == npb - 2, bsl_ref[1], bsl_ref[2]))

    tiles = []
    for u in range(n_keys // LANES):
        tiles.append(jnp.where(lane < SLC_BLOCK, slot_bias(2 * u), slot_bias(2 * u + 1)))
    bias = jnp.concatenate(tiles, axis=1)
    col = lax.broadcasted_iota(jnp.int32, (HEADS, n_keys), 1)
    s = _dot_nt(q, keys) + bias + jnp.where(col <= n_pick * SLC_BLOCK, 0.0, NEG_INF)
    p = _softmax_rows(s)
    osl_ref[0] = _dot(p.astype(MXU_DT), vals)

    w_buf = win_ref.shape[1] // 2
    wk = win_ref[0, pl.ds(0, w_buf, stride=2), :].astype(MXU_DT)
    wv = win_ref[0, pl.ds(1, w_buf, stride=2), :].astype(MXU_DT)
    sw = _dot_nt(q, wk) + bw_ref[...]
    new_k = kvn[:, 4 * HD:5 * HD].astype(MXU_DT).astype(F32)
    new_v = kvn[:, 5 * HD:6 * HD].astype(MXU_DT).astype(F32)
    s_new = jnp.sum(q.astype(F32) * new_k, axis=-1, keepdims=True) + b0_ref[:, 0:1]
    m = jnp.maximum(jnp.max(sw, axis=-1, keepdims=True), s_new)
    pw = jnp.exp(sw - m)
    pn = jnp.exp(s_new - m)
    l = jnp.sum(pw, axis=-1, keepdims=True) + pn
    pn_r = pn.astype(MXU_DT).astype(F32)
    owin_ref[0] = (_dot(pw.astype(MXU_DT), wv) + pn_r * new_v) / l


def nsa_sample_slc_win(q, kv_new, win_state, idx, page_table, cache_rows, bias_slc, bias_0, bias_w,
                       *, npb):
    db = q.shape[0]
    n_pick = N_SEL - 1
    w_buf = win_state.shape[1] // 2
    assert w_buf <= WINDOW and ((n_pick + 1) * SLC_BLOCK) % LANES == 0
    n_keys = (n_pick + 1) * SLC_BLOCK
    grid_spec = pltpu.PrefetchScalarGridSpec(
        num_scalar_prefetch=2,
        grid=(db,),
        in_specs=[pl.BlockSpec((1, HEADS, HD), lambda b, ix, pt: (b, 0, 0)),
                  pl.BlockSpec((1, 1, NSA_KV_SLOTS * HD), lambda b, ix, pt: (b, 0, 0)),
                  pl.BlockSpec((1, 2 * w_buf, HD), lambda b, ix, pt: (b, 0, 0)),
                  pl.BlockSpec((3, HEADS, LANES), lambda b, ix, pt: (0, 0, 0)),
                  pl.BlockSpec((HEADS, LANES), lambda b, ix, pt: (0, 0)),
                  pl.BlockSpec((HEADS, w_buf), lambda b, ix, pt: (0, 0)),
                  pl.BlockSpec(memory_space=pl.ANY)],
        out_specs=[pl.BlockSpec((1, HEADS, HD), lambda b, ix, pt: (b, 0, 0)),
                   pl.BlockSpec((1, HEADS, HD), lambda b, ix, pt: (b, 0, 0))],
        scratch_shapes=[pltpu.VMEM((2, n_pick * SLC_BLOCK * NSA_CACHE_SLOTS, HD), F32),
                        pltpu.SemaphoreType.DMA((2,))])
    return pl.pallas_call(
        functools.partial(_nsa_sample_slc_win_kernel, n_pick=n_pick, npb=npb),
        grid_spec=grid_spec,
        out_shape=[jax.ShapeDtypeStruct((db, HEADS, HD), F32),
                   jax.ShapeDtypeStruct((db, HEADS, HD), F32)],
        compiler_params=_cparams(1),
        name="nsa_sample_slc_win",
    )(idx, page_table, q, kv_new, win_state, bias_slc, bias_0, bias_w, cache_rows)


def _gate_combine_kernel(g_ref, oc_ref, os_ref, ow_ref, o_ref):
    g = _sigmoid(g_ref[...])
    o_ref[...] = (g[0] * oc_ref[...] + g[1] * os_ref[...] + g[2] * ow_ref[...]).astype(o_ref.dtype)


def nsa_gate_combine(gate_logits, o_cmp, o_slc, o_win):
    r = o_cmp.shape[0]
    return pl.pallas_call(
        _gate_combine_kernel,
        out_shape=jax.ShapeDtypeStruct((r, HD), MXU_DT),
        compiler_params=pltpu.CompilerParams(vmem_limit_bytes=VMEM_LIMIT),
        name="nsa_gate_combine",
    )(gate_logits, o_cmp, o_slc, o_win)


def _mla_prep_kernel(h_ref, gq_ref, gkv_ref, cos_ref, sin_ref, dup_ref,
                     cq_ref, ckv_ref, rows_ref, kr_ref):
    h = h_ref[...]

    def rms(x, g):
        return x * lax.rsqrt(jnp.mean(x * x, axis=-1, keepdims=True) + RMS_EPS) * g

    cq_ref[...] = rms(h[:, :Q_LORA], gq_ref[...]).astype(cq_ref.dtype)
    ckv = rms(h[:, Q_LORA:Q_LORA + KV_LORA], gkv_ref[...])
    ckv_ref[...] = ckv.astype(ckv_ref.dtype)
    kr2 = _dot_hi(h[:, Q_LORA + KV_LORA:Q_LORA + KV_LORA + QK_ROPE], dup_ref[...])
    kr2 = _rope_lanes(kr2, cos_ref[...], sin_ref[...])
    kr_ref[...] = kr2
    rows_ref[:, :KV_LORA] = ckv
    rows_ref[:, KV_LORA:] = kr2[:, :QK_ROPE]


def mla_prep(h, g_q, g_kv, cos_t, sin_t, *, pos_blocks, tm=512):
    m = h.shape[0]
    tm = min(tm, m)
    dup = np.concatenate([np.eye(QK_ROPE, dtype=np.float32)] * 2, axis=1)
    return pl.pallas_call(
        _mla_prep_kernel,
        grid=(m // tm,),
        in_specs=[pl.BlockSpec((tm, h.shape[1]), lambda i: (i, 0)),
                  pl.BlockSpec((1, Q_LORA), lambda i: (0, 0)),
                  pl.BlockSpec((1, KV_LORA), lambda i: (0, 0)),
                  pl.BlockSpec((tm, LANES), lambda i: (i % pos_blocks, 0)),
                  pl.BlockSpec((tm, LANES), lambda i: (i % pos_blocks, 0)),
                  pl.BlockSpec((QK_ROPE, LANES), lambda i: (0, 0))],
        out_specs=[pl.BlockSpec((tm, Q_LORA), lambda i: (i, 0)),
                   pl.BlockSpec((tm, KV_LORA), lambda i: (i, 0)),
                   pl.BlockSpec((tm, MLA_ROW), lambda i: (i, 0)),
                   pl.BlockSpec((tm, LANES), lambda i: (i, 0))],
        out_shape=[jax.ShapeDtypeStruct((m, Q_LORA), MXU_DT),
                   jax.ShapeDtypeStruct((m, KV_LORA), MXU_DT),
                   jax.ShapeDtypeStruct((m, MLA_ROW), F32),
                   jax.ShapeDtypeStruct((m, LANES), F32)],
        compiler_params=_cparams(1),
        name="mla_prep",
    )(h, g_q.reshape(1, -1), g_kv.reshape(1, -1), cos_t, sin_t, jnp.asarray(dup))


def _rope_q_kernel(q_ref, cos_ref, sin_ref, o_ref):
    cos_t = cos_ref[...]
    sin_t = sin_ref[...]
    for g in range(q_ref.shape[1] // LANES):
        cols = slice(g * LANES, (g + 1) * LANES)
        o_ref[:, cols] = _rope_lanes(q_ref[:, cols], cos_t, sin_t)


def rope_q(q, cos_t, sin_t, *, pos_blocks, tm=512):
    m = q.shape[0]
    tm = min(tm, m)
    wr = HEADS * QK_ROPE
    assert (HEADS * HD) % wr == 0
    col_blk = HEADS * HD // wr
    return pl.pallas_call(
        _rope_q_kernel,
        grid=(m // tm,),
        in_specs=[pl.BlockSpec((tm, wr), lambda i: (i, col_blk)),
                  pl.BlockSpec((tm, LANES), lambda i: (i % pos_blocks, 0)),
                  pl.BlockSpec((tm, LANES), lambda i: (i % pos_blocks, 0))],
        out_specs=pl.BlockSpec((tm, wr), lambda i: (i, 0)),
        out_shape=jax.ShapeDtypeStruct((m, wr), F32),
        compiler_params=_cparams(1),
        name="rope_q",
    )(q, cos_t, sin_t)


def _mla_prompt_kernel(qn_ref, qr_ref, kv_ref, kr_ref, o_ref, kc_scr, v_scr, s_scr, *, nt):
    h = pl.program_id(1)
    i = pl.program_id(2)

    @pl.when(i == 0)
    def _():
        for kt in range(nt):
            rows = slice(kt * QB, (kt + 1) * QB)
            kc_scr[kt, :, 0:HD] = kv_ref[rows, 0:HD].astype(MXU_DT)
            kc_scr[kt, :, HD:2 * HD] = kr_ref[rows, :].astype(MXU_DT)
            v_scr[kt] = kv_ref[rows, HD:2 * HD].astype(MXU_DT)

    row = lax.broadcasted_iota(jnp.int32, (QB, QB), 0)
    lane = lax.broadcasted_iota(jnp.int32, (QB, QB), 1)
    tri = jnp.where(lane <= row, 0.0, NEG_INF)
    qr = jnp.where((lane // QK_ROPE) == (h % 2), qr_ref[...], 0.0)
    qc = (jnp.concatenate([qn_ref[...], qr], axis=1) * MLA_SCALE).astype(MXU_DT)
    zero = jnp.zeros((QB, QB), F32)
    o = _band_attention(qc, kc_scr, v_scr, s_scr, i=i, lo=0, far_bias=0.0, bias_prev=zero,
                        bias_diag=zero, far_mask=lambda kt: 0.0, tri=tri)
    o_ref[...] = o.astype(o_ref.dtype)


def mla_prompt_attention(q, qr, kvx, kr2, *, batch, seq):
    nt = seq // QB
    rope_blk0 = 0
    return pl.pallas_call(
        functools.partial(_mla_prompt_kernel, nt=nt),
        grid=(batch, HEADS, nt),
        in_specs=[pl.BlockSpec((QB, HD), lambda b, h, i: (b * nt + i, h)),
                  pl.BlockSpec((QB, LANES), lambda b, h, i: (b * nt + i, rope_blk0 + h // 2)),
                  pl.BlockSpec((seq, 2 * HD), lambda b, h, i: (b, h)),
                  pl.BlockSpec((seq, LANES), lambda b, h, i: (b, 0))],
        out_specs=pl.BlockSpec((QB, HD), lambda b, h, i: (b * nt + i, h)),
        out_shape=jax.ShapeDtypeStruct((batch * seq, HEADS * HD), MXU_DT),
        scratch_shapes=[pltpu.VMEM((nt, QB, 2 * HD), MXU_DT), pltpu.VMEM((nt, QB, HD), MXU_DT),
                        pltpu.VMEM((nt, QB, QB), F32)],
        compiler_params=_cparams(3),
        name="mla_prompt_attention",
    )(q, qr, kvx, kr2)


def _absorb_q_kernel(q_ref, w_ref, o_ref):
    o_ref[0] = _dot_nt(q_ref[...].astype(MXU_DT), w_ref[...].astype(MXU_DT))


def mla_absorb_q(q, w_kv_b):
    m = q.shape[0]
    return pl.pallas_call(
        _absorb_q_kernel,
        grid=(HEADS,),
        in_specs=[pl.BlockSpec((m, HD), lambda h: (0, h)),
                  pl.BlockSpec((KV_LORA, HD), lambda h: (0, 2 * h))],
        out_specs=pl.BlockSpec((1, m, KV_LORA), lambda h: (h, 0, 0)),
        out_shape=jax.ShapeDtypeStruct((HEADS, m, KV_LORA), F32),
        compiler_params=_cparams(1),
        name="mla_absorb_q",
    )(q, w_kv_b)


def _absorb_o_kernel(o_ref, w_ref, out_ref):
    out_ref[...] = _dot(o_ref[0].astype(MXU_DT), w_ref[...].astype(MXU_DT)).astype(out_ref.dtype)


def mla_absorb_o(o_lat, w_kv_b):
    m = o_lat.shape[1]
    return pl.pallas_call(
        _absorb_o_kernel,
        grid=(HEADS,),
        in_specs=[pl.BlockSpec((1, m, KV_LORA), lambda h: (h, 0, 0)),
                  pl.BlockSpec((KV_LORA, HD), lambda h: (0, 2 * h + 1))],
        out_specs=pl.BlockSpec((m, HD), lambda h: (0, h)),
        out_shape=jax.ShapeDtypeStruct((m, HEADS * HD), MXU_DT),
        compiler_params=_cparams(1),
        name="mla_absorb_o",
    )(o_lat, w_kv_b)


def _mla_decode_kernel(pt_ref, ql_ref, qr_ref, new_ref, cache_ref, o_ref, buf, sem,
                       *, n_chunks, pages):
    b = pl.program_id(0)
    nb = pl.num_programs(0)

    def chunk_copies(bb, ch, slot):
        cps = []
        for p in range(pages):
            page = pt_ref[bb, ch * pages + p]
            cps.append(pltpu.make_async_copy(cache_ref.at[page], buf.at[slot, p], sem.at[slot]))
        return cps

    @pl.when(b == 0)
    def _():
        for cp in chunk_copies(0, 0, 0):
            cp.start()

    ql = (ql_ref[0] * MLA_SCALE).astype(MXU_DT)
    qr = (qr_ref[0] * MLA_SCALE).astype(MXU_DT)

    def chunk_step(ch, carry):
        m, l, acc = carry
        slot = ch % 2

        @pl.when(ch + 1 < n_chunks)
        def _():
            for cp in chunk_copies(b, ch + 1, 1 - slot):
                cp.start()

        @pl.when((ch + 1 == n_chunks) & (b + 1 < nb))
        def _():
            for cp in chunk_copies(b + 1, 0, 1 - slot):
                cp.start()

        for cp in chunk_copies(b, ch, slot):
            cp.wait()
        ckv_t = [buf[slot, p, 0:KV_LORA, :].astype(MXU_DT) for p in range(pages)]
        s = jnp.concatenate(
            [_dot(ql, ckv_t[p]) + _dot(qr, buf[slot, p, KV_LORA:MLA_ROW, :].astype(MXU_DT))
             for p in range(pages)], axis=1)
        m_new = jnp.maximum(m, jnp.max(s, axis=-1, keepdims=True))
        alpha = jnp.exp(m - m_new)
        pr = jnp.exp(s - m_new)
        l = l * alpha + jnp.sum(pr, axis=-1, keepdims=True)
        pr = pr.astype(MXU_DT)
        acc = acc * alpha
        for p in range(pages):
            acc = acc + _dot_nt(pr[:, p * LANES:(p + 1) * LANES], ckv_t[p])
        return m_new, l, acc

    m, l, acc = lax.fori_loop(
        0, n_chunks, chunk_step,
        (jnp.full((HEADS, 1), NEG_INF, F32), jnp.zeros((HEADS, 1), F32),
         jnp.zeros((HEADS, KV_LORA), F32)))

    new = new_ref[0].astype(MXU_DT).astype(F32)
    s_new = (jnp.sum(ql.astype(F32) * new[:, :KV_LORA], axis=-1, keepdims=True)
             + jnp.sum(qr.astype(F32) * new[:, KV_LORA:], axis=-1, keepdims=True))
    m_f = jnp.maximum(m, s_new)
    alpha = jnp.exp(m - m_f)
    pn = jnp.exp(s_new - m_f)
    pn_r = pn.astype(MXU_DT).astype(F32)
    o_ref[0] = (acc * alpha + pn_r * new[:, :KV_LORA]) / (l * alpha + pn)


def mla_decode(q_lat, q_rope, rows_new, page_table, cache, *, pages=8):
    db, n_pages = page_table.shape
    assert n_pages % pages == 0
    n_chunks = n_pages // pages
    assert n_chunks % 2 == 0
    grid_spec = pltpu.PrefetchScalarGridSpec(
        num_scalar_prefetch=1,
        grid=(db,),
        in_specs=[pl.BlockSpec((1, HEADS, KV_LORA), lambda b, pt: (b, 0, 0)),
                  pl.BlockSpec((1, HEADS, QK_ROPE), lambda b, pt: (b, 0, 0)),
                  pl.BlockSpec((1, 1, MLA_ROW), lambda b, pt: (b, 0, 0)),
                  pl.BlockSpec(memory_space=pl.ANY)],
        out_specs=pl.BlockSpec((1, HEADS, KV_LORA), lambda b, pt: (b, 0, 0)),
        scratch_shapes=[pltpu.VMEM((2, pages, MLA_ROW, LANES), F32),
                        pltpu.SemaphoreType.DMA((2,))])
    return pl.pallas_call(
        functools.partial(_mla_decode_kernel, n_chunks=n_chunks, pages=pages),
        grid_spec=grid_spec,
        out_shape=jax.ShapeDtypeStruct((db, HEADS, KV_LORA), F32),
        compiler_params=_cparams(1),
        name="mla_decode",
    )(page_table, q_lat, q_rope, rows_new, cache)


def _rope_tables(pos):
    inv = ROPE_THETA ** (-jnp.arange(0, QK_ROPE, 2, dtype=F32) / QK_ROPE)
    ang = pos.astype(F32)[:, None] * inv[None, :]
    cos, sin = jnp.cos(ang), jnp.sin(ang)
    return jnp.tile(jnp.concatenate([cos, cos], axis=-1), (1, 2)), \
        jnp.tile(jnp.concatenate([-sin, sin], axis=-1), (1, 2))


def _bias_ids(seq, past_len, w_buf):
    nt = seq // QB
    r = np.arange(QB)[:, None]
    c = np.arange(QB)[None, :]
    prev = _t5_bucket_np(r - c + QB)
    diag = _t5_bucket_np(r - c)
    qi = np.arange(nt)[:, None, None]
    cmp_p = _t5_bucket_np(qi * QB + r[None] - (c[None] * CMP_STRIDE + CMP_BLOCK - 1)).reshape(nt * QB, QB)
    n_tok = past_len // CMP_STRIDE
    cmp_s = _t5_bucket_np(past_len - (np.arange(n_tok) * CMP_STRIDE + CMP_BLOCK - 1)).reshape(-1, LANES)
    off = np.arange(LANES) % SLC_BLOCK
    slc_s = np.stack([_t5_bucket_np(SLC_BLOCK - off), _t5_bucket_np(2 * SLC_BLOCK - off),
                      _t5_bucket_np(np.full(LANES, 3 * SLC_BLOCK)), _t5_bucket_np(np.zeros(LANES))])
    win_s = _t5_bucket_np(w_buf - np.arange(w_buf)).reshape(-1, LANES)
    parts = [prev, diag, cmp_p, cmp_s, slc_s, win_s]
    rows = sum(p.shape[0] for p in parts)
    pad = -rows % 256
    ids = np.concatenate(parts + [np.zeros((pad, LANES), np.int32)], axis=0).astype(np.int32)
    offs = np.cumsum([0] + [p.shape[0] for p in parts])
    return ids, offs


def kernel(x_prompt, x_sample, p_prompt, p_sample, cache_nsa_kv, state_nsa_win, cache_mla_kv, page_table, t5_table, ln_g, ln_b, nsa_w_in, nsa_cmp_a, nsa_cmp_phi, nsa_w_o, mla_w_in, mla_q_norm, mla_kv_norm, mla_w_q_b, mla_w_kv_b, mla_w_o, ffn_w_in, ffn_w_out, ple_w_gate, ple_w_proj):
    batch, seq, d = x_prompt.shape
    db = x_sample.shape[0]
    depth = ln_g.shape[0]
    assert depth == 2 and x_sample.shape[1] == 1
    n_pages = page_table.shape[1]
    page_size = cache_nsa_kv.shape[2]
    assert page_size == LANES
    past_len = n_pages * page_size
    npb = past_len // SLC_BLOCK
    n_pool = cache_nsa_kv.shape[1]
    w_buf = state_nsa_win.shape[2]
    alpha = (2 * depth) ** 0.25
    q_cols = HEADS * HD
    kv_cols = NSA_KV_SLOTS * HD
    mp = batch * seq

    ids, offs = _bias_ids(seq, past_len, w_buf)
    bias = t5_bias_lookup(t5_table, jnp.asarray(ids))
    sect = lambda k: bias[:, offs[k]:offs[k + 1]]
    bias_d = jnp.stack([sect(0), sect(1)], axis=1)
    bias_c = sect(2).reshape(HEADS, seq // QB, QB, QB).transpose(1, 0, 2, 3)
    bias_cs = sect(3).reshape(HEADS, -1)
    bias_ss = sect(4)
    bias_slc = bias_ss[:, :3].transpose(1, 0, 2)
    bias_0 = bias_ss[:, 3]
    bias_w = sect(5).reshape(HEADS, w_buf)

    def dense_tail(x, mixed_in, w_o, p, i):
        x = matmul_residual_ln(mixed_in, w_o, x, ln_g[i, 0], ln_b[i, 0], alpha=alpha)
        hmid = swiglu_in(x, ffn_w_in[i])
        x = matmul_residual_ln(hmid, ffn_w_out[i], x, ln_g[i, 1], ln_b[i, 1], alpha=alpha)
        return ple(x, ple_w_gate[i], p, ple_w_proj[i])

    xp = x_prompt.reshape(mp, d)
    xs = x_sample.reshape(db, d)
    w_in0 = nsa_w_in[0]
    hp = matmul(xp, w_in0, tn=256)
    hs = matmul(xs, w_in0, tn=256)

    att_p = nsa_prompt_attention(hp, t5_table, nsa_cmp_a[0], nsa_cmp_phi[0], bias_c, bias_d,
                                 batch=batch, seq=seq)

    hp3 = hp.reshape(batch, seq, -1)
    nsa_kv_prompt = hp3[:, :, q_cols:q_cols + 4 * HD].reshape(1, batch, seq, 4, HD)
    win_rows_p = hp3[:, :, q_cols + 4 * HD:q_cols + kv_cols].reshape(batch, seq, 2, HD)
    if seq >= w_buf:
        nsa_win_prompt = win_rows_p[:, seq - w_buf:][None]
    else:
        nsa_win_prompt = jnp.pad(win_rows_p, ((0, 0), (w_buf - seq, 0), (0, 0), (0, 0)))[None]

    qs = hs[:, :q_cols].reshape(db, HEADS, HD)
    kvn = hs[:, q_cols:q_cols + kv_cols].reshape(db, 1, kv_cols)
    cache_rows = cache_nsa_kv.reshape(cache_nsa_kv.shape[0] * n_pool, page_size * NSA_CACHE_SLOTS, HD)
    o_cmp_s, imp_s = nsa_sample_compressed(qs, page_table, cache_rows, nsa_cmp_a[0], nsa_cmp_phi[0],
                                           bias_cs, past_len=past_len)
    idx = nsa_sample_select(imp_s.reshape(db, -1), npb=npb)
    win_state = state_nsa_win[0].reshape(db, w_buf * 2, HD)
    o_slc_s, o_win_s = nsa_sample_slc_win(qs, kvn, win_state, idx, page_table, cache_rows,
                                          bias_slc, bias_0, bias_w, npb=npb)
    gate_logits = hs[:, q_cols + kv_cols:].reshape(db, 3, HEADS).transpose(1, 0, 2).reshape(3, db * HEADS, 1)
    att_s = nsa_gate_combine(gate_logits, o_cmp_s.reshape(db * HEADS, HD),
                             o_slc_s.reshape(db * HEADS, HD), o_win_s.reshape(db * HEADS, HD))
    att_s = att_s.reshape(db, q_cols)

    nsa_kv_sample = hs[:, q_cols:q_cols + 4 * HD].reshape(1, db, 1, 4, HD)
    new_win = hs[:, q_cols + 4 * HD:q_cols + kv_cols].reshape(db, 1, 2, HD)
    nsa_win_sample = jnp.concatenate([state_nsa_win[0], new_win], axis=1)[:, -w_buf:][None]

    xp = dense_tail(xp, att_p, nsa_w_o[0], p_prompt[0].reshape(mp, -1), 0)
    xs = dense_tail(xs, att_s, nsa_w_o[0], p_sample[0].reshape(db, -1), 0)

    wqb = mla_w_q_b[0].reshape(Q_LORA, HEADS, HD + QK_ROPE)
    wqb = jnp.concatenate([wqb[:, :, :HD].reshape(Q_LORA, HEADS * HD),
                           wqb[:, :, HD:].reshape(Q_LORA, HEADS * QK_ROPE)], axis=1)
    w_kv_b = mla_w_kv_b[0]
    cos_p, sin_p = _rope_tables(jnp.arange(seq))
    cos_s, sin_s = _rope_tables(jnp.full((db,), past_len))

    tmp = 512
    hp = matmul(xp, mla_w_in[0], tn=256)
    cq_p, ckv_p, rows_p, kr2_p = mla_prep(hp, mla_q_norm[0], mla_kv_norm[0], cos_p, sin_p,
                                          pos_blocks=seq // tmp, tm=tmp)
    q_p = matmul(cq_p, wqb, tn=512)
    qr_p = rope_q(q_p, cos_p, sin_p, pos_blocks=seq // tmp, tm=tmp)
    kvx_p = matmul(ckv_p, w_kv_b, tn=512, out_dtype=MXU_DT)
    att_p = mla_prompt_attention(q_p, qr_p, kvx_p, kr2_p, batch=batch, seq=seq)

    hs = matmul(xs, mla_w_in[0], tn=256)
    cq_s, _, rows_s, _ = mla_prep(hs, mla_q_norm[0], mla_kv_norm[0], cos_s, sin_s, pos_blocks=1, tm=db)
    q_s = matmul(cq_s, wqb, tn=512)
    qr_s = rope_q(q_s, cos_s, sin_s, pos_blocks=1, tm=db)
    q_lat = mla_absorb_q(q_s, w_kv_b).transpose(1, 0, 2)
    cache_mla = jnp.swapaxes(cache_mla_kv, 2, 3).reshape(cache_mla_kv.shape[0] * n_pool, MLA_ROW, page_size)
    o_lat = mla_decode(q_lat, qr_s.reshape(db, HEADS, QK_ROPE), rows_s.reshape(db, 1, MLA_ROW),
                       page_table, cache_mla)
    att_s = mla_absorb_o(o_lat.transpose(1, 0, 2), w_kv_b)

    xp = dense_tail(xp, att_p, mla_w_o[0], p_prompt[1].reshape(mp, -1), 1)
    xs = dense_tail(xs, att_s, mla_w_o[0], p_sample[1].reshape(db, -1), 1)

    return (xp.reshape(batch, seq, d), xs.reshape(db, 1, d),
            nsa_kv_prompt, nsa_win_prompt, rows_p.reshape(1, batch, seq, MLA_ROW),
            nsa_kv_sample, nsa_win_sample, rows_s.reshape(1, db, 1, MLA_ROW))
```

```python
import functools
import math

import numpy as np
import jax
import jax.numpy as jnp
from jax import lax
from jax.experimental import pallas as pl
from jax.experimental.pallas import tpu as pltpu

F32 = jnp.float32
MXU_DT = jnp.bfloat16
HI = lax.Precision.HIGHEST

HEADS = 16
HD = 128
CMP_STRIDE = 16
CMP_BLOCK = 32
SLC_BLOCK = 64
N_SEL = 16
WINDOW = 512
NSA_KV_SLOTS = 6
NSA_CACHE_SLOTS = 4
NSA_SCALE = HD ** -0.5
Q_LORA = 768
KV_LORA = 512
QK_ROPE = 64
MLA_ROW = KV_LORA + QK_ROPE
MLA_SCALE = (HD + QK_ROPE) ** -0.5
ROPE_THETA = 10000.0
T5_BUCKETS = 32
T5_MAX_DIST = 128
LN_EPS = 1e-5
RMS_EPS = 1e-6
NEG_INF = -1e30
FORCE_SCORE = 1e4
QB = 128
LANES = 128
HEAD_GROUP = 8
FAR_CHUNK = 512
MLA_QBLK = 512

VMEM_LIMIT = 56 * 1024 * 1024


def _cparams(n_axes):
    return pltpu.CompilerParams(dimension_semantics=("arbitrary",) * n_axes,
                                vmem_limit_bytes=VMEM_LIMIT)


def _dot(a, b):
    return jnp.dot(a, b, preferred_element_type=F32)


def _dot_nt(a, b):
    return lax.dot_general(a, b, (((1,), (1,)), ((), ())), preferred_element_type=F32)


def _dot_hi(a, b):
    return jnp.dot(a, b, precision=HI, preferred_element_type=F32)


def _sigmoid(x):
    return 1.0 / (1.0 + jnp.exp(-x))


def _mm_kernel(x_ref, w_ref, o_ref, xb_ref):
    @pl.when(pl.program_id(1) == 0)
    def _():
        xb_ref[...] = x_ref[...].astype(MXU_DT)
    o_ref[...] = _dot(xb_ref[...], w_ref[...].astype(MXU_DT)).astype(o_ref.dtype)


def matmul(x, w, *, tn, out_dtype=F32, tm=1024):
    m, k = x.shape
    n = w.shape[1]
    tm = min(tm, m)
    return pl.pallas_call(
        _mm_kernel,
        grid=(pl.cdiv(m, tm), pl.cdiv(n, tn)),
        in_specs=[pl.BlockSpec((tm, k), lambda i, j: (i, 0)),
                  pl.BlockSpec((k, tn), lambda i, j: (0, j))],
        out_specs=pl.BlockSpec((tm, tn), lambda i, j: (i, j)),
        out_shape=jax.ShapeDtypeStruct((m, n), out_dtype),
        scratch_shapes=[pltpu.VMEM((tm, k), MXU_DT)],
        compiler_params=_cparams(2),
        name="matmul",
    )(x, w)


def _swiglu_kernel(x_ref, w1_ref, w2_ref, o_ref, xb_ref):
    @pl.when(pl.program_id(1) == 0)
    def _():
        xb_ref[...] = x_ref[...].astype(MXU_DT)
    xb = xb_ref[...]
    h1 = _dot(xb, w1_ref[...].astype(MXU_DT))
    h2 = _dot(xb, w2_ref[...].astype(MXU_DT))
    o_ref[...] = (h1 * _sigmoid(h1) * h2).astype(o_ref.dtype)


def swiglu_in(x, w_in, layer, *, tn=512, tm=1024):
    m, k = x.shape
    d_ff = w_in.shape[2] // 2
    assert d_ff % tn == 0
    nff = d_ff // tn
    tm = min(tm, m)
    return pl.pallas_call(
        _swiglu_kernel,
        grid=(pl.cdiv(m, tm), nff),
        in_specs=[pl.BlockSpec((tm, k), lambda i, j: (i, 0)),
                  pl.BlockSpec((None, k, tn), lambda i, j: (layer, 0, j)),
                  pl.BlockSpec((None, k, tn), lambda i, j: (layer, 0, j + nff))],
        out_specs=pl.BlockSpec((tm, tn), lambda i, j: (i, j)),
        out_shape=jax.ShapeDtypeStruct((m, d_ff), MXU_DT),
        scratch_shapes=[pltpu.VMEM((tm, k), MXU_DT)],
        compiler_params=_cparams(2),
        name="swiglu_in",
    )(x, w_in, w_in)


def _mm_ln_kernel(x_ref, w_ref, r_ref, g_ref, b_ref, o_ref, acc_ref, *, alpha, nk):
    k = pl.program_id(1)

    @pl.when(k == 0)
    def _():
        acc_ref[...] = jnp.zeros_like(acc_ref)

    acc_ref[...] += _dot(x_ref[...].astype(MXU_DT), w_ref[...].astype(MXU_DT))

    @pl.when(k == nk - 1)
    def _():
        y = alpha * r_ref[...] + acc_ref[...]
        mu = jnp.mean(y, axis=-1, keepdims=True)
        yc = y - mu
        var = jnp.mean(yc * yc, axis=-1, keepdims=True)
        o_ref[...] = yc * lax.rsqrt(var + LN_EPS) * g_ref[...] + b_ref[...]


def matmul_residual_ln(x, w, layer, resid, g, b, *, alpha, tk=512, tm=512):
    m, k = x.shape
    n = w.shape[2]
    tm = min(tm, m)
    assert k % tk == 0
    nk = k // tk
    return pl.pallas_call(
        functools.partial(_mm_ln_kernel, alpha=alpha, nk=nk),
        grid=(pl.cdiv(m, tm), nk),
        in_specs=[pl.BlockSpec((tm, tk), lambda i, kk: (i, kk)),
                  pl.BlockSpec((None, tk, n), lambda i, kk: (layer, kk, 0)),
                  pl.BlockSpec((tm, n), lambda i, kk: (i, 0)),
                  pl.BlockSpec((1, n), lambda i, kk: (0, 0)),
                  pl.BlockSpec((1, n), lambda i, kk: (0, 0))],
        out_specs=pl.BlockSpec((tm, n), lambda i, kk: (i, 0)),
        out_shape=jax.ShapeDtypeStruct((m, n), F32),
        scratch_shapes=[pltpu.VMEM((tm, n), F32)],
        compiler_params=_cparams(2),
        name="matmul_residual_ln",
    )(x, w, resid, g.reshape(1, n), b.reshape(1, n))


def _ple_kernel(x_ref, wg_ref, p_ref, wp_ref, xt_ref, o_ref, xb_ref):
    @pl.when(pl.program_id(1) == 0)
    def _():
        xb_ref[...] = x_ref[...].astype(MXU_DT)
    gate = _sigmoid(_dot(xb_ref[...], wg_ref[...].astype(MXU_DT)))
    proj = _dot(p_ref[...].astype(MXU_DT), wp_ref[...].astype(MXU_DT))
    o_ref[...] = xt_ref[...] + gate * proj


def ple(x, w_gate, p, w_proj, layer, *, tn=512, tm=1024):
    m, k = x.shape
    n = w_gate.shape[2]
    pe = p.shape[2]
    tm = min(tm, m)
    return pl.pallas_call(
        _ple_kernel,
        grid=(pl.cdiv(m, tm), pl.cdiv(n, tn)),
        in_specs=[pl.BlockSpec((tm, k), lambda i, j: (i, 0)),
                  pl.BlockSpec((None, k, tn), lambda i, j: (layer, 0, j)),
                  pl.BlockSpec((None, tm, pe), lambda i, j: (layer, i, 0)),
                  pl.BlockSpec((None, pe, tn), lambda i, j: (layer, 0, j)),
                  pl.BlockSpec((tm, tn), lambda i, j: (i, j))],
        out_specs=pl.BlockSpec((tm, tn), lambda i, j: (i, j)),
        out_shape=jax.ShapeDtypeStruct((m, n), F32),
        scratch_shapes=[pltpu.VMEM((tm, k), MXU_DT)],
        compiler_params=_cparams(2),
        name="ple",
    )(x, w_gate, p, w_proj, x)


def _t5_bucket_np(dist):
    n = np.maximum(np.asarray(dist, np.int64), 0)
    max_exact = T5_BUCKETS // 2
    nf = np.maximum(n, 1).astype(np.float32)
    scaled = (np.log(nf / np.float32(max_exact)) / np.float32(math.log(T5_MAX_DIST / max_exact))
              * np.float32(T5_BUCKETS - max_exact))
    large = np.minimum(max_exact + scaled.astype(np.int32), T5_BUCKETS - 1)
    return np.where(n < max_exact, n, large).astype(np.int32)


def _bias_kernel(tab_ref, ids_ref, o_ref, *, n_rel):
    ids = ids_ref[...]
    relative = pl.program_id(0) < n_rel
    for h in range(HEADS):
        acc = jnp.zeros(ids.shape, F32)
        for bkt in range(T5_BUCKETS):
            acc = jnp.where(ids == bkt, tab_ref[bkt, h], acc)
        o_ref[h] = acc - jnp.where(relative, tab_ref[T5_BUCKETS - 1, h], 0.0)


def t5_bias_lookup(table, ids, *, n_rel, tr=256):
    r = ids.shape[0]
    assert r % tr == 0
    return pl.pallas_call(
        functools.partial(_bias_kernel, n_rel=n_rel),
        grid=(r // tr,),
        in_specs=[pl.BlockSpec(memory_space=pltpu.SMEM),
                  pl.BlockSpec((tr, LANES), lambda i: (i, 0))],
        out_specs=pl.BlockSpec((HEADS, tr, LANES), lambda i: (0, i, 0)),
        out_shape=jax.ShapeDtypeStruct((HEADS, r, LANES), F32),
        compiler_params=_cparams(1),
        name="t5_bias_lookup",
    )(table, ids)


def _softmax_rows(s):
    m = jnp.max(s, axis=-1, keepdims=True)
    e = jnp.exp(s - m)
    l = jnp.sum(e, axis=-1, keepdims=True)
    return jnp.where(m > 0.5 * NEG_INF, e / l, 0.0)


def _lane_tiles(x):
    return [x[..., t * LANES:(t + 1) * LANES] for t in range(x.shape[-1] // LANES)]


def _fold(op, tiles):
    out = tiles[0]
    for t in tiles[1:]:
        out = op(out, t)
    return out


def _rope_lanes(x, cos_t, sin_t):
    lane = lax.broadcasted_iota(jnp.int32, x.shape, 1)
    first_half = (lane % QK_ROPE) < (QK_ROPE // 2)
    rot = jnp.where(first_half, pltpu.roll(x, LANES - QK_ROPE // 2, 1), pltpu.roll(x, QK_ROPE // 2, 1))
    return x * cos_t + rot * sin_t


def _nsa_prompt_kernel(q_ref, g_ref, kcr_ref, vcr_ref, ksr_ref, vsr_ref, kwr_ref, vwr_ref,
                       a_ref, phi_ref, bc_ref, bn_ref, amat_ref, emat_ref, o_ref,
                       kc_scr, vc_scr, ks_scr, vs_scr, kw_scr, vw_scr, q_scr, selk_scr, a_scr,
                       mx_scr, mb_scr, l_scr, acc_scr, sn_scr, oc_scr, os_scr, ow_scr,
                       *, nt, n_blk):
    i = pl.program_id(1)
    seq = nt * QB
    n_cmp = nt * (QB // CMP_STRIDE) - 1
    wt = WINDOW // QB
    hg = HEAD_GROUP
    gm = hg * QB
    fc = FAR_CHUNK
    tpc = fc // QB

    @pl.when(i == 0)
    def _():
        ks_scr[0:QB, :] = jnp.zeros((QB, HD), MXU_DT)
        vs_scr[0:QB, :] = jnp.zeros((QB, HD), MXU_DT)
        kw_scr[0:wt * QB, :] = jnp.zeros((wt * QB, HD), MXU_DT)
        vw_scr[0:wt * QB, :] = jnp.zeros((wt * QB, HD), MXU_DT)
        ks_scr[QB:QB + seq, :] = ksr_ref[...].astype(MXU_DT)
        vs_scr[QB:QB + seq, :] = vsr_ref[...].astype(MXU_DT)
        kw_scr[wt * QB:wt * QB + seq, :] = kwr_ref[...].astype(MXU_DT)
        vw_scr[wt * QB:wt * QB + seq, :] = vwr_ref[...].astype(MXU_DT)
        for which, (rows_ref, dst) in enumerate(((kcr_ref, kc_scr), (vcr_ref, vc_scr))):
            ng = nt * QB // CMP_STRIDE
            first = jnp.zeros((ng, HD), F32)
            second = jnp.zeros((ng, HD), F32)
            for j in range(CMP_STRIDE):
                xj = rows_ref[pl.ds(j, ng, stride=CMP_STRIDE), :]
                first = first + xj * a_ref[which, j:j + 1, :]
                second = second + xj * a_ref[which, CMP_STRIDE + j:CMP_STRIDE + j + 1, :]
            pre = first + pltpu.roll(second, ng - 1, 0)
            dst[...] = _dot_hi(pre, phi_ref[which]).astype(MXU_DT)

    for h in range(HEADS):
        q_scr[h * QB:(h + 1) * QB, :] = (q_ref[:, h * HD:(h + 1) * HD] * NSA_SCALE).astype(MXU_DT)

    row = lax.broadcasted_iota(jnp.int32, (QB, QB), 0)
    lane = lax.broadcasted_iota(jnp.int32, (QB, QB), 1)
    qpos = i * QB + row
    tri = jnp.where(lane <= row, 0.0, NEG_INF)
    cmp_valid = (qpos >= lane * CMP_STRIDE + (CMP_BLOCK - 1)) & (lane < n_cmp)
    cmp_mask = jnp.where(cmp_valid, 0.0, NEG_INF)

    s = _dot_nt(q_scr[...], kc_scr[...]).reshape(HEADS, QB, QB) + bc_ref[0] + cmp_mask[None]
    p = _softmax_rows(s)
    imp = jnp.sum(p, axis=0)
    oc_scr[...] = _dot(p.reshape(HEADS * QB, QB).astype(MXU_DT), vc_scr[...])

    slc_imp = _dot_hi(imp, amat_ref[...])
    cur = qpos // SLC_BLOCK
    forced = (lane == 0) | (lane == cur) | (lane == cur - 1)
    score = jnp.where(lane > cur, -1.0, jnp.where(forced, FORCE_SCORE, slc_imp))
    score = jnp.where(lane < n_blk, score, -2.0)
    rank = jnp.zeros((QB, QB), F32)
    for b2 in range(n_blk):
        col = score[:, b2:b2 + 1]
        beats = (col > score) | ((col == score) & (lane > b2))
        rank = rank + jnp.where(beats, 1.0, 0.0)
    sel = jnp.where((rank < min(N_SEL, n_blk)) & (lane < n_blk), 1.0, 0.0)
    sel_keys = _dot(sel.astype(MXU_DT), emat_ref[...])
    for kt in range(nt):
        tile = jnp.where(sel_keys[:, kt * QB:(kt + 1) * QB] > 0.5, 0.0, NEG_INF)
        selk_scr[kt] = tile
        a_scr[kt // tpc, :, (kt % tpc) * QB:(kt % tpc + 1) * QB] = jnp.where(kt < i - 1, tile, NEG_INF)
    prev_ok = jnp.where(i > 0, 0.0, NEG_INF)
    near_mask = jnp.concatenate([selk_scr[jnp.maximum(i - 1, 0)] + prev_ok, selk_scr[i] + tri], axis=1)

    zeros = jnp.zeros((QB, QB), F32)
    band = jnp.where(lane >= row, 0.0, NEG_INF)

    def tile_ok(t):
        return jnp.where(i - wt + t >= 0, 0.0, NEG_INF)

    win_far_mask = jnp.concatenate(
        [band + tile_ok(0)] + [zeros + tile_ok(t) for t in range(1, wt - 1)], axis=1)
    win_near_mask = jnp.concatenate([zeros + tile_ok(wt - 1), tri], axis=1)
    n_far = (i + tpc - 2) // tpc
    n0 = pl.multiple_of(i * QB, QB)

    def group(g, carry):
        r0 = pl.multiple_of(g * gm, gm)
        qg = q_scr[pl.ds(r0, gm), :]
        bn = bn_ref[pl.ds(g * hg, hg)]

        def far_scores(c):
            k0 = pl.multiple_of(QB + c * fc, QB)
            s = _dot_nt(qg, ks_scr[pl.ds(k0, fc), :]).reshape(hg, QB, fc) + a_scr[c][None]
            return k0, s

        mx_scr[...] = jnp.full((hg, QB, QB), NEG_INF, F32)

        def far_max(c, carry2):
            _, s = far_scores(c)
            mx_scr[...] = jnp.maximum(mx_scr[...], _fold(jnp.maximum, _lane_tiles(s)))
            return carry2

        lax.fori_loop(0, n_far, far_max, 0)
        sn = _dot_nt(qg, ks_scr[pl.ds(n0, 2 * QB), :]).reshape(hg, QB, 2 * QB) + bn + near_mask[None]
        sn_scr[...] = sn
        mfold = jnp.maximum(mx_scr[...], _fold(jnp.maximum, _lane_tiles(sn)))
        mb_scr[...] = jnp.broadcast_to(jnp.max(mfold, axis=-1, keepdims=True), (hg, QB, QB))
        l_scr[...] = jnp.zeros((hg, QB, QB), F32)
        acc_scr[...] = jnp.zeros((gm, HD), F32)

        def far_pv(c, carry2):
            k0, s = far_scores(c)
            mbv = mb_scr[...]
            ps = [jnp.exp(t - mbv) for t in _lane_tiles(s)]
            l_scr[...] += _fold(jnp.add, ps)
            pm = jnp.concatenate(ps, axis=-1).reshape(gm, fc).astype(MXU_DT)
            acc_scr[...] += _dot(pm, vs_scr[pl.ds(k0, fc), :])
            return carry2

        lax.fori_loop(0, n_far, far_pv, 0)
        mbv = mb_scr[...]
        pn = [jnp.exp(t - mbv) for t in _lane_tiles(sn_scr[...])]
        l = l_scr[...] + _fold(jnp.add, pn)
        pm = jnp.concatenate(pn, axis=-1).reshape(gm, 2 * QB).astype(MXU_DT)
        acc = acc_scr[...] + _dot(pm, vs_scr[pl.ds(n0, 2 * QB), :])
        o_slc = acc.reshape(hg, QB, HD) / jnp.sum(l, axis=-1, keepdims=True)
        os_scr[pl.ds(r0, gm), :] = o_slc.reshape(gm, HD)

        nw = (wt + 1) * QB
        nf = (wt - 1) * QB
        sw = _dot_nt(qg, kw_scr[pl.ds(n0, nw), :]).reshape(hg, QB, nw)
        s_far = sw[:, :, :nf] + win_far_mask[None]
        s_near = sw[:, :, nf:] + bn + win_near_mask[None]
        m = jnp.maximum(jnp.max(s_far, axis=-1, keepdims=True), jnp.max(s_near, axis=-1, keepdims=True))
        e_far = jnp.exp(s_far - m)
        e_near = jnp.exp(s_near - m)
        l = jnp.sum(e_far, axis=-1, keepdims=True) + jnp.sum(e_near, axis=-1, keepdims=True)
        pm = jnp.concatenate([e_far, e_near], axis=-1).reshape(gm, nw).astype(MXU_DT)
        o_win = _dot(pm, vw_scr[pl.ds(n0, nw), :]).reshape(hg, QB, HD) / l
        ow_scr[pl.ds(r0, gm), :] = o_win.reshape(gm, HD)
        return carry

    lax.fori_loop(0, HEADS // hg, group, 0)

    gates = _sigmoid(g_ref[...])
    for h in range(HEADS):
        rows = slice(h * QB, (h + 1) * QB)
        o = (gates[:, h:h + 1] * oc_scr[rows, :] + gates[:, HEADS + h:HEADS + h + 1] * os_scr[rows, :]
             + gates[:, 2 * HEADS + h:2 * HEADS + h + 1] * ow_scr[rows, :])
        o_ref[:, h * HD:(h + 1) * HD] = o.astype(o_ref.dtype)


def nsa_prompt_attention(h, cmp_a, cmp_phi, bias_c, bias_near, *, batch, seq):
    nt = seq // QB
    n_blk = seq // SLC_BLOCK
    wt = WINDOW // QB
    assert seq // CMP_STRIDE == QB and n_blk <= QB and WINDOW % QB == 0 and wt >= 2
    assert FAR_CHUNK % QB == 0 and seq % FAR_CHUNK == 0 and HEADS % HEAD_GROUP == 0
    q_cols = HEADS * HD
    kv_blk0 = q_cols // HD
    gate_blk = kv_blk0 + NSA_KV_SLOTS
    c = np.arange(QB)[:, None]
    b = np.arange(QB)[None, :]
    ratio = SLC_BLOCK // CMP_STRIDE
    amat = (((c >= ratio * b) & (c <= ratio * b + ratio - 1)).astype(np.float32)
            + ((c + 1 >= ratio * b) & (c + 1 <= ratio * b + ratio - 1)).astype(np.float32))
    amat = amat * (b < n_blk)
    emat = (np.arange(seq)[None, :] // SLC_BLOCK == np.arange(QB)[:, None]).astype(np.float32)

    kv_spec = lambda s: pl.BlockSpec((seq, HD), lambda bb, i, s=s: (bb, kv_blk0 + s))
    full = lambda shape: pl.BlockSpec(shape, lambda bb, i: (0,) * len(shape))
    keys_scr = lambda pad_tiles: pltpu.VMEM(((nt + pad_tiles) * QB, HD), MXU_DT)
    heads_scr = lambda dt: pltpu.VMEM((HEADS * QB, HD), dt)
    group_scr = lambda: pltpu.VMEM((HEAD_GROUP, QB, QB), F32)
    return pl.pallas_call(
        functools.partial(_nsa_prompt_kernel, nt=nt, n_blk=n_blk),
        grid=(batch, nt),
        in_specs=[pl.BlockSpec((QB, q_cols), lambda bb, i: (bb * nt + i, 0)),
                  pl.BlockSpec((QB, LANES), lambda bb, i: (bb * nt + i, gate_blk))]
                 + [kv_spec(s) for s in range(NSA_KV_SLOTS)]
                 + [full((2, CMP_BLOCK, HD)), full((2, HD, HD)),
                    pl.BlockSpec((1, HEADS, QB, QB), lambda bb, i: (i, 0, 0, 0)),
                    full((HEADS, QB, 2 * QB)), full((QB, QB)), full((QB, seq))],
        out_specs=pl.BlockSpec((QB, q_cols), lambda bb, i: (bb * nt + i, 0)),
        out_shape=jax.ShapeDtypeStruct((batch * seq, q_cols), MXU_DT),
        scratch_shapes=[pltpu.VMEM((QB, HD), MXU_DT), pltpu.VMEM((QB, HD), MXU_DT),
                        keys_scr(1), keys_scr(1), keys_scr(wt), keys_scr(wt),
                        heads_scr(MXU_DT), pltpu.VMEM((nt, QB, QB), F32),
                        pltpu.VMEM((seq // FAR_CHUNK, QB, FAR_CHUNK), F32),
                        group_scr(), group_scr(), group_scr(),
                        pltpu.VMEM((HEAD_GROUP * QB, HD), F32),
                        pltpu.VMEM((HEAD_GROUP, QB, 2 * QB), F32),
                        heads_scr(F32), heads_scr(F32), heads_scr(F32)],
        compiler_params=_cparams(2),
        name="nsa_prompt_attention",
    )(h, h, h, h, h, h, h, h, cmp_a, cmp_phi, bias_c, bias_near,
      jnp.asarray(amat), jnp.asarray(emat, MXU_DT))


def _nsa_sample_cmp_kernel(pt_ref, q_ref, w_ref, phi_ref, bias_ref, cache_ref, oc_ref, imp_ref,
                           buf, sem, part_scr, kc_scr, vc_scr, *, n_chunks, pages, past_len):
    b = pl.program_id(0)
    nb = pl.num_programs(0)
    prow = LANES * NSA_CACHE_SLOTS
    rows = pages * prow
    grow = CMP_STRIDE * NSA_CACHE_SLOTS
    groups = pages * LANES // CMP_STRIDE
    sub = 8
    nv = grow // sub

    def chunk_copies(bb, ch, slot):
        cps = []
        for p in range(pages):
            page = pt_ref[bb, ch * pages + p]
            cps.append(pltpu.make_async_copy(cache_ref.at[page],
                                             buf.at[slot, pl.ds(p * prow, prow), :], sem.at[slot]))
        if ch + 1 < n_chunks:
            page = pt_ref[bb, (ch + 1) * pages]
            cps.append(pltpu.make_async_copy(cache_ref.at[page, pl.ds(0, grow), :],
                                             buf.at[slot, pl.ds(rows, grow), :], sem.at[slot]))
        return cps

    @pl.when(b == 0)
    def _():
        for cp in chunk_copies(0, 0, 0):
            cp.start()

    for ch in range(n_chunks):
        slot = (b * n_chunks + ch) % 2
        if ch + 1 < n_chunks:
            for cp in chunk_copies(b, ch + 1, 1 - slot):
                cp.start()
        else:
            @pl.when(b + 1 < nb)
            def _():
                for cp in chunk_copies(b + 1, 0, 1 - slot):
                    cp.start()
        for cp in chunk_copies(b, ch, slot):
            cp.wait()
        if ch == n_chunks - 1:
            buf[slot, rows:rows + grow, :] = jnp.zeros((grow, HD), F32)
        x0 = buf[slot, 0:rows, :].reshape(groups, nv, sub, HD)
        x1 = buf[slot, grow:rows + grow, :].reshape(groups, nv, sub, HD)
        part = x0[:, 0] * w_ref[0, 0] + x1[:, 0] * w_ref[1, 0]
        for t in range(1, nv):
            part = part + x0[:, t] * w_ref[0, t] + x1[:, t] * w_ref[1, t]
        part_scr[...] = part.reshape(groups * sub, HD)
        for kv, dst in enumerate((kc_scr, vc_scr)):
            pre = (part_scr[pl.ds(kv, groups, stride=sub), :]
                   + part_scr[pl.ds(NSA_CACHE_SLOTS + kv, groups, stride=sub), :])
            dst[ch * groups:(ch + 1) * groups, :] = _dot_hi(pre, phi_ref[kv]).astype(MXU_DT)

    n_tok = n_chunks * groups
    q = (q_ref[0] * NSA_SCALE).astype(MXU_DT)
    tok = lax.broadcasted_iota(jnp.int32, (HEADS, n_tok), 1)
    valid = tok * CMP_STRIDE + (CMP_BLOCK - 1) <= past_len
    s = _dot_nt(q, kc_scr[...]) + bias_ref[...] + jnp.where(valid, 0.0, NEG_INF)
    p = _softmax_rows(s)
    oc_ref[0] = _dot(p.astype(MXU_DT), vc_scr[...])
    imp_ref[0] = jnp.sum(p, axis=0, keepdims=True)


def nsa_sample_compressed(q, page_table, cache_rows, cmp_a, cmp_phi, bias_c, *, past_len, pages=16):
    db, n_pages = page_table.shape
    assert n_pages % pages == 0
    n_chunks = n_pages // pages
    rows = (pages * LANES + CMP_STRIDE) * NSA_CACHE_SLOTS
    n_tok = n_pages * LANES // CMP_STRIDE
    nv = CMP_STRIDE // 2
    w = cmp_a.reshape(2, 2, nv, 2, HD).transpose(1, 2, 3, 0, 4)
    w = jnp.pad(w, ((0, 0), (0, 0), (0, 0), (0, NSA_CACHE_SLOTS - 2), (0, 0)))
    w = w.reshape(2, nv, 2 * NSA_CACHE_SLOTS, HD)
    grid_spec = pltpu.PrefetchScalarGridSpec(
        num_scalar_prefetch=1,
        grid=(db,),
        in_specs=[pl.BlockSpec((1, HEADS, HD), lambda b, pt: (b, 0, 0)),
                  pl.BlockSpec((2, nv, 2 * NSA_CACHE_SLOTS, HD), lambda b, pt: (0, 0, 0, 0)),
                  pl.BlockSpec((2, HD, HD), lambda b, pt: (0, 0, 0)),
                  pl.BlockSpec((HEADS, n_tok), lambda b, pt: (0, 0)),
                  pl.BlockSpec(memory_space=pl.ANY)],
        out_specs=[pl.BlockSpec((1, HEADS, HD), lambda b, pt: (b, 0, 0)),
                   pl.BlockSpec((1, 1, n_tok), lambda b, pt: (b, 0, 0))],
        scratch_shapes=[pltpu.VMEM((2, rows, HD), F32),
                        pltpu.SemaphoreType.DMA((2,)),
                        pltpu.VMEM((pages * LANES // CMP_STRIDE * 8, HD), F32),
                        pltpu.VMEM((n_tok, HD), MXU_DT), pltpu.VMEM((n_tok, HD), MXU_DT)])
    return pl.pallas_call(
        functools.partial(_nsa_sample_cmp_kernel, n_chunks=n_chunks, pages=pages, past_len=past_len),
        grid_spec=grid_spec,
        out_shape=[jax.ShapeDtypeStruct((db, HEADS, HD), F32),
                   jax.ShapeDtypeStruct((db, 1, n_tok), F32)],
        compiler_params=_cparams(1),
        name="nsa_sample_compressed",
    )(page_table, q, w, cmp_phi, bias_c, cache_rows)


def _select_kernel(imp_ref, amat_ref, idx_ref, *, npb, n_pick):
    slc_imp = _dot_hi(imp_ref[...], amat_ref[...])
    lane = lax.broadcasted_iota(jnp.int32, slc_imp.shape, 1).astype(F32)
    forced = (lane == 0.0) | (lane == float(npb - 1))
    score = jnp.where(forced, FORCE_SCORE, slc_imp)
    score = jnp.where(lane < float(npb), score, -2.0)
    out_lane = lax.broadcasted_iota(jnp.int32, idx_ref.shape, 1)
    out = jnp.zeros(idx_ref.shape, F32)
    for t in range(n_pick):
        best = jnp.max(score, axis=-1, keepdims=True)
        first = jnp.min(jnp.where(score == best, lane, 1e9), axis=-1, keepdims=True)
        out = jnp.where(out_lane == t, first, out)
        score = jnp.where(lane == first, -3.0, score)
    idx_ref[...] = out.astype(jnp.int32)


def nsa_sample_select(imp, *, npb):
    db, n_tok = imp.shape
    assert npb >= N_SEL
    nbp = -(-npb // LANES) * LANES
    c = np.arange(n_tok)[:, None]
    b = np.arange(nbp)[None, :]
    ratio = SLC_BLOCK // CMP_STRIDE
    amat = (((c >= ratio * b) & (c <= ratio * b + ratio - 1)).astype(np.float32)
            + ((c + 1 >= ratio * b) & (c + 1 <= ratio * b + ratio - 1)).astype(np.float32))
    amat = amat * (b < npb)
    return pl.pallas_call(
        functools.partial(_select_kernel, npb=npb, n_pick=N_SEL - 1),
        out_shape=jax.ShapeDtypeStruct((db, LANES), jnp.int32),
        compiler_params=pltpu.CompilerParams(vmem_limit_bytes=VMEM_LIMIT),
        name="nsa_sample_select",
    )(imp, jnp.asarray(amat))


def _nsa_sample_slc_win_kernel(idx_ref, pt_ref, q_ref, kvn_ref, win_ref, bsl_ref, b0_ref, bw_ref,
                               cache_ref, osl_ref, owin_ref, kbuf, sem, *, n_pick, npb):
    b = pl.program_id(0)
    nb = pl.num_programs(0)
    bpp = LANES // SLC_BLOCK
    brow = SLC_BLOCK * NSA_CACHE_SLOTS

    def block_copies(bb, slot):
        cps = []
        for t in range(n_pick):
            blk = idx_ref[bb, t]
            page = pt_ref[bb, blk // bpp]
            off = pl.multiple_of((blk % bpp) * brow, brow)
            cps.append(pltpu.make_async_copy(cache_ref.at[page, pl.ds(off, brow), :],
                                             kbuf.at[slot, pl.ds(t * brow, brow), :], sem.at[slot]))
        return cps

    slot = b % 2

    @pl.when(b == 0)
    def _():
        for cp in block_copies(0, 0):
            cp.start()

    @pl.when(b + 1 < nb)
    def _():
        for cp in block_copies(b + 1, 1 - slot):
            cp.start()

    for cp in block_copies(b, slot):
        cp.wait()

    n_keys = (n_pick + 1) * SLC_BLOCK
    kvn = kvn_ref[0]
    rowi = lax.broadcasted_iota(jnp.int32, (SLC_BLOCK, HD), 0)

    def gathered(cache_slot):
        parts = [kbuf[slot, pl.ds(t * brow + cache_slot, SLC_BLOCK, stride=NSA_CACHE_SLOTS), :]
                 for t in range(n_pick)]
        new_row = kvn[:, cache_slot * HD:(cache_slot + 1) * HD]
        parts.append(jnp.where(rowi == 0, new_row, 0.0))
        return jnp.concatenate(parts, axis=0).astype(MXU_DT)

    q = (q_ref[0] * NSA_SCALE).astype(MXU_DT)
    keys = gathered(2)
    vals = gathered(3)
    lane = lax.broadcasted_iota(jnp.int32, (HEADS, LANES), 1)

    def slot_bias(t):
        if t == n_pick:
            return b0_ref[...]
        blk = idx_ref[b, t]
        return jnp.where(blk == npb - 1, bsl_ref[0], jnp.where(blk == npb - 2, bsl_ref[1], bsl_ref[2]))

    tiles = []
    for u in range(n_keys // LANES):
        tiles.append(jnp.where(lane < SLC_BLOCK, slot_bias(2 * u), slot_bias(2 * u + 1)))
    bias = jnp.concatenate(tiles, axis=1)
    col = lax.broadcasted_iota(jnp.int32, (HEADS, n_keys), 1)
    s = _dot_nt(q, keys) + bias + jnp.where(col <= n_pick * SLC_BLOCK, 0.0, NEG_INF)
    p = _softmax_rows(s)
    osl_ref[0] = _dot(p.astype(MXU_DT), vals)

    w_buf = win_ref.shape[1] // 2
    wk = win_ref[0, pl.ds(0, w_buf, stride=2), :].astype(MXU_DT)
    wv = win_ref[0, pl.ds(1, w_buf, stride=2), :].astype(MXU_DT)
    sw = _dot_nt(q, wk) + bw_ref[...]
    new_k = kvn[:, 4 * HD:5 * HD].astype(MXU_DT).astype(F32)
    new_v = kvn[:, 5 * HD:6 * HD].astype(MXU_DT).astype(F32)
    s_new = jnp.sum(q.astype(F32) * new_k, axis=-1, keepdims=True) + b0_ref[:, 0:1]
    m = jnp.maximum(jnp.max(sw, axis=-1, keepdims=True), s_new)
    pw = jnp.exp(sw - m)
    pn = jnp.exp(s_new - m)
    l = jnp.sum(pw, axis=-1, keepdims=True) + pn
    pn_r = pn.astype(MXU_DT).astype(F32)
    owin_ref[0] = (_dot(pw.astype(MXU_DT), wv) + pn_r * new_v) / l


def nsa_sample_slc_win(q, kv_new, win_state, idx, page_table, cache_rows, bias_slc, bias_0, bias_w,
                       *, npb):
    db = q.shape[0]
    n_pick = N_SEL - 1
    w_buf = win_state.shape[1] // 2
    assert w_buf <= WINDOW and ((n_pick + 1) * SLC_BLOCK) % LANES == 0
    n_keys = (n_pick + 1) * SLC_BLOCK
    grid_spec = pltpu.PrefetchScalarGridSpec(
        num_scalar_prefetch=2,
        grid=(db,),
        in_specs=[pl.BlockSpec((1, HEADS, HD), lambda b, ix, pt: (b, 0, 0)),
                  pl.BlockSpec((1, 1, NSA_KV_SLOTS * HD), lambda b, ix, pt: (b, 0, 0)),
                  pl.BlockSpec((1, 2 * w_buf, HD), lambda b, ix, pt: (b, 0, 0)),
                  pl.BlockSpec((3, HEADS, LANES), lambda b, ix, pt: (0, 0, 0)),
                  pl.BlockSpec((HEADS, LANES), lambda b, ix, pt: (0, 0)),
                  pl.BlockSpec((HEADS, w_buf), lambda b, ix, pt: (0, 0)),
                  pl.BlockSpec(memory_space=pl.ANY)],
        out_specs=[pl.BlockSpec((1, HEADS, HD), lambda b, ix, pt: (b, 0, 0)),
                   pl.BlockSpec((1, HEADS, HD), lambda b, ix, pt: (b, 0, 0))],
        scratch_shapes=[pltpu.VMEM((2, n_pick * SLC_BLOCK * NSA_CACHE_SLOTS, HD), F32),
                        pltpu.SemaphoreType.DMA((2,))])
    return pl.pallas_call(
        functools.partial(_nsa_sample_slc_win_kernel, n_pick=n_pick, npb=npb),
        grid_spec=grid_spec,
        out_shape=[jax.ShapeDtypeStruct((db, HEADS, HD), F32),
                   jax.ShapeDtypeStruct((db, HEADS, HD), F32)],
        compiler_params=_cparams(1),
        name="nsa_sample_slc_win",
    )(idx, page_table, q, kv_new, win_state, bias_slc, bias_0, bias_w, cache_rows)


def _gate_combine_kernel(g_ref, oc_ref, os_ref, ow_ref, o_ref):
    g = _sigmoid(g_ref[...])
    o_ref[...] = (g[0] * oc_ref[...] + g[1] * os_ref[...] + g[2] * ow_ref[...]).astype(o_ref.dtype)


def nsa_gate_combine(gate_logits, o_cmp, o_slc, o_win):
    r = o_cmp.shape[0]
    return pl.pallas_call(
        _gate_combine_kernel,
        out_shape=jax.ShapeDtypeStruct((r, HD), MXU_DT),
        compiler_params=pltpu.CompilerParams(vmem_limit_bytes=VMEM_LIMIT),
        name="nsa_gate_combine",
    )(gate_logits, o_cmp, o_slc, o_win)


def _mla_prep_kernel(h_ref, gq_ref, gkv_ref, cos_ref, sin_ref, dup_ref,
                     cq_ref, ckv_ref, rows_ref, kr_ref):
    h = h_ref[...]

    def rms(x, g):
        return x * lax.rsqrt(jnp.mean(x * x, axis=-1, keepdims=True) + RMS_EPS) * g

    cq_ref[...] = rms(h[:, :Q_LORA], gq_ref[...]).astype(cq_ref.dtype)
    ckv = rms(h[:, Q_LORA:Q_LORA + KV_LORA], gkv_ref[...])
    ckv_ref[...] = ckv.astype(ckv_ref.dtype)
    kr2 = _dot_hi(h[:, Q_LORA + KV_LORA:Q_LORA + KV_LORA + QK_ROPE], dup_ref[...])
    kr2 = _rope_lanes(kr2, cos_ref[...], sin_ref[...])
    kr_ref[...] = kr2
    rows_ref[:, :KV_LORA] = ckv
    rows_ref[:, KV_LORA:] = kr2[:, :QK_ROPE]


def mla_prep(h, g_q, g_kv, cos_t, sin_t, *, pos_blocks, tm=512):
    m = h.shape[0]
    tm = min(tm, m)
    dup = np.concatenate([np.eye(QK_ROPE, dtype=np.float32)] * 2, axis=1)
    return pl.pallas_call(
        _mla_prep_kernel,
        grid=(m // tm,),
        in_specs=[pl.BlockSpec((tm, h.shape[1]), lambda i: (i, 0)),
                  pl.BlockSpec((1, Q_LORA), lambda i: (0, 0)),
                  pl.BlockSpec((1, KV_LORA), lambda i: (0, 0)),
                  pl.BlockSpec((tm, LANES), lambda i: (i % pos_blocks, 0)),
                  pl.BlockSpec((tm, LANES), lambda i: (i % pos_blocks, 0)),
                  pl.BlockSpec((QK_ROPE, LANES), lambda i: (0, 0))],
        out_specs=[pl.BlockSpec((tm, Q_LORA), lambda i: (i, 0)),
                   pl.BlockSpec((tm, KV_LORA), lambda i: (i, 0)),
                   pl.BlockSpec((tm, MLA_ROW), lambda i: (i, 0)),
                   pl.BlockSpec((tm, LANES), lambda i: (i, 0))],
        out_shape=[jax.ShapeDtypeStruct((m, Q_LORA), MXU_DT),
                   jax.ShapeDtypeStruct((m, KV_LORA), MXU_DT),
                   jax.ShapeDtypeStruct((m, MLA_ROW), F32),
                   jax.ShapeDtypeStruct((m, LANES), F32)],
        compiler_params=_cparams(1),
        name="mla_prep",
    )(h, g_q.reshape(1, -1), g_kv.reshape(1, -1), cos_t, sin_t, jnp.asarray(dup))


def _rope_q_kernel(q_ref, cos_ref, sin_ref, o_ref):
    cos_t = cos_ref[...]
    sin_t = sin_ref[...]
    for g in range(q_ref.shape[1] // LANES):
        cols = slice(g * LANES, (g + 1) * LANES)
        o_ref[:, cols] = _rope_lanes(q_ref[:, cols], cos_t, sin_t)


def rope_q(q, cos_t, sin_t, *, pos_blocks, tm=512):
    m = q.shape[0]
    tm = min(tm, m)
    wr = HEADS * QK_ROPE
    assert (HEADS * HD) % wr == 0
    col_blk = HEADS * HD // wr
    return pl.pallas_call(
        _rope_q_kernel,
        grid=(m // tm,),
        in_specs=[pl.BlockSpec((tm, wr), lambda i: (i, col_blk)),
                  pl.BlockSpec((tm, LANES), lambda i: (i % pos_blocks, 0)),
                  pl.BlockSpec((tm, LANES), lambda i: (i % pos_blocks, 0))],
        out_specs=pl.BlockSpec((tm, wr), lambda i: (i, 0)),
        out_shape=jax.ShapeDtypeStruct((m, wr), F32),
        compiler_params=_cparams(1),
        name="rope_q",
    )(q, cos_t, sin_t)


def _mla_prompt_kernel(qn_ref, qr_ref, kv_ref, kr_ref, o_ref, qc_scr, kc_scr, v_scr, *, seq):
    h = pl.program_id(1)
    blk = MLA_QBLK
    nq = seq // blk
    lane = lax.broadcasted_iota(jnp.int32, (blk, LANES), 1)
    for qb in range(nq):
        rows = slice(qb * blk, (qb + 1) * blk)
        qr = jnp.where((lane // QK_ROPE) == (h % 2), qr_ref[rows, :], 0.0)
        qc_scr[rows, 0:HD] = (qn_ref[rows, :] * MLA_SCALE).astype(MXU_DT)
        qc_scr[rows, HD:2 * HD] = (qr * MLA_SCALE).astype(MXU_DT)
        kc_scr[rows, 0:HD] = kv_ref[rows, 0:HD].astype(MXU_DT)
        kc_scr[rows, HD:2 * HD] = kr_ref[rows, :].astype(MXU_DT)
        v_scr[rows, :] = kv_ref[rows, HD:2 * HD].astype(MXU_DT)

    r = lax.broadcasted_iota(jnp.int32, (blk, blk), 0)
    c = lax.broadcasted_iota(jnp.int32, (blk, blk), 1)
    tri = jnp.where(c <= r, 0.0, NEG_INF)
    for qb in range(nq):
        d0 = qb * blk
        q = qc_scr[d0:d0 + blk, :]
        s_d = _dot_nt(q, kc_scr[d0:d0 + blk, :]) + tri
        m = jnp.max(s_d, axis=-1, keepdims=True)
        if qb > 0:
            s_o = _dot_nt(q, kc_scr[0:d0, :])
            m = jnp.maximum(m, jnp.max(s_o, axis=-1, keepdims=True))
        p_d = jnp.exp(s_d - m)
        l = jnp.sum(p_d, axis=-1, keepdims=True)
        o = _dot(p_d.astype(MXU_DT), v_scr[d0:d0 + blk, :])
        if qb > 0:
            p_o = jnp.exp(s_o - m)
            l = l + jnp.sum(p_o, axis=-1, keepdims=True)
            o = o + _dot(p_o.astype(MXU_DT), v_scr[0:d0, :])
        o_ref[d0:d0 + blk, :] = (o / l).astype(o_ref.dtype)


def mla_prompt_attention(q, qr, kvx, kr2, *, batch, seq):
    assert seq % MLA_QBLK == 0
    return pl.pallas_call(
        functools.partial(_mla_prompt_kernel, seq=seq),
        grid=(batch, HEADS),
        in_specs=[pl.BlockSpec((seq, HD), lambda b, h: (b, h)),
                  pl.BlockSpec((seq, LANES), lambda b, h: (b, h // 2)),
                  pl.BlockSpec((seq, 2 * HD), lambda b, h: (b, h)),
                  pl.BlockSpec((seq, LANES), lambda b, h: (b, 0))],
        out_specs=pl.BlockSpec((seq, HD), lambda b, h: (b, h)),
        out_shape=jax.ShapeDtypeStruct((batch * seq, HEADS * HD), MXU_DT),
        scratch_shapes=[pltpu.VMEM((seq, 2 * HD), MXU_DT), pltpu.VMEM((seq, 2 * HD), MXU_DT),
                        pltpu.VMEM((seq, HD), MXU_DT)],
        compiler_params=_cparams(2),
        name="mla_prompt_attention",
    )(q, qr, kvx, kr2)


def _absorb_q_kernel(q_ref, w_ref, o_ref):
    o_ref[0] = _dot_nt(q_ref[...].astype(MXU_DT), w_ref[...].astype(MXU_DT))


def mla_absorb_q(q, w_kv_b):
    m = q.shape[0]
    return pl.pallas_call(
        _absorb_q_kernel,
        grid=(HEADS,),
        in_specs=[pl.BlockSpec((m, HD), lambda h: (0, h)),
                  pl.BlockSpec((KV_LORA, HD), lambda h: (0, 2 * h))],
        out_specs=pl.BlockSpec((1, m, KV_LORA), lambda h: (h, 0, 0)),
        out_shape=jax.ShapeDtypeStruct((HEADS, m, KV_LORA), F32),
        compiler_params=_cparams(1),
        name="mla_absorb_q",
    )(q, w_kv_b)


def _absorb_o_kernel(o_ref, w_ref, out_ref):
    out_ref[...] = _dot(o_ref[0].astype(MXU_DT), w_ref[...].astype(MXU_DT)).astype(out_ref.dtype)


def mla_absorb_o(o_lat, w_kv_b):
    m = o_lat.shape[1]
    return pl.pallas_call(
        _absorb_o_kernel,
        grid=(HEADS,),
        in_specs=[pl.BlockSpec((1, m, KV_LORA), lambda h: (h, 0, 0)),
                  pl.BlockSpec((KV_LORA, HD), lambda h: (0, 2 * h + 1))],
        out_specs=pl.BlockSpec((m, HD), lambda h: (0, h)),
        out_shape=jax.ShapeDtypeStruct((m, HEADS * HD), MXU_DT),
        compiler_params=_cparams(1),
        name="mla_absorb_o",
    )(o_lat, w_kv_b)


def _mla_decode_kernel(pt_ref, ql_ref, qr_ref, new_ref, cache_ref, o_ref, buf, sem,
                       *, n_chunks, pages):
    b = pl.program_id(0)
    nb = pl.num_programs(0)

    def chunk_copies(bb, ch, slot):
        cps = []
        for p in range(pages):
            page = pt_ref[bb, ch * pages + p]
            cps.append(pltpu.make_async_copy(cache_ref.at[page], buf.at[slot, p], sem.at[slot]))
        return cps

    @pl.when(b == 0)
    def _():
        for cp in chunk_copies(0, 0, 0):
            cp.start()

    ql = (ql_ref[0] * MLA_SCALE).astype(MXU_DT)
    qr = (qr_ref[0] * MLA_SCALE).astype(MXU_DT)

    def chunk_step(ch, carry):
        m, l, acc = carry
        slot = (b * n_chunks + ch) % 2

        @pl.when(ch + 1 < n_chunks)
        def _():
            for cp in chunk_copies(b, ch + 1, 1 - slot):
                cp.start()

        @pl.when((ch + 1 == n_chunks) & (b + 1 < nb))
        def _():
            for cp in chunk_copies(b + 1, 0, 1 - slot):
                cp.start()

        for cp in chunk_copies(b, ch, slot):
            cp.wait()
        s = jnp.concatenate(
            [_dot(ql, buf[slot, p, 0:KV_LORA, :].astype(MXU_DT))
             + _dot(qr, buf[slot, p, KV_LORA:MLA_ROW, :].astype(MXU_DT))
             for p in range(pages)], axis=1)
        m_new = jnp.maximum(m, jnp.max(s, axis=-1, keepdims=True))
        alpha = jnp.exp(m - m_new)
        pr = jnp.exp(s - m_new)
        l = l * alpha + jnp.sum(pr, axis=-1, keepdims=True)
        pr = pr.astype(MXU_DT)
        acc = acc * alpha
        for p in range(pages):
            acc = acc + _dot_nt(pr[:, p * LANES:(p + 1) * LANES],
                                buf[slot, p, 0:KV_LORA, :].astype(MXU_DT))
        return m_new, l, acc

    m, l, acc = lax.fori_loop(
        0, n_chunks, chunk_step,
        (jnp.full((HEADS, 1), NEG_INF, F32), jnp.zeros((HEADS, 1), F32),
         jnp.zeros((HEADS, KV_LORA), F32)))

    new = new_ref[0].astype(MXU_DT).astype(F32)
    s_new = (jnp.sum(ql.astype(F32) * new[:, :KV_LORA], axis=-1, keepdims=True)
             + jnp.sum(qr.astype(F32) * new[:, KV_LORA:], axis=-1, keepdims=True))
    m_f = jnp.maximum(m, s_new)
    alpha = jnp.exp(m - m_f)
    pn = jnp.exp(s_new - m_f)
    pn_r = pn.astype(MXU_DT).astype(F32)
    o_ref[0] = (acc * alpha + pn_r * new[:, :KV_LORA]) / (l * alpha + pn)


def mla_decode(q_lat, q_rope, rows_new, page_table, cache, *, pages=32):
    db, n_pages = page_table.shape
    assert n_pages % pages == 0
    n_chunks = n_pages // pages
    grid_spec = pltpu.PrefetchScalarGridSpec(
        num_scalar_prefetch=1,
        grid=(db,),
        in_specs=[pl.BlockSpec((1, HEADS, KV_LORA), lambda b, pt: (b, 0, 0)),
                  pl.BlockSpec((1, HEADS, QK_ROPE), lambda b, pt: (b, 0, 0)),
                  pl.BlockSpec((1, 1, MLA_ROW), lambda b, pt: (b, 0, 0)),
                  pl.BlockSpec(memory_space=pl.ANY)],
        out_specs=pl.BlockSpec((1, HEADS, KV_LORA), lambda b, pt: (b, 0, 0)),
        scratch_shapes=[pltpu.VMEM((2, pages, MLA_ROW, LANES), F32),
                        pltpu.SemaphoreType.DMA((2,))])
    return pl.pallas_call(
        functools.partial(_mla_decode_kernel, n_chunks=n_chunks, pages=pages),
        grid_spec=grid_spec,
        out_shape=jax.ShapeDtypeStruct((db, HEADS, KV_LORA), F32),
        compiler_params=_cparams(1),
        name="mla_decode",
    )(page_table, q_lat, q_rope, rows_new, cache)


def _rope_tables(pos):
    inv = ROPE_THETA ** (-jnp.arange(0, QK_ROPE, 2, dtype=F32) / QK_ROPE)
    ang = pos.astype(F32)[:, None] * inv[None, :]
    cos, sin = jnp.cos(ang), jnp.sin(ang)
    return jnp.tile(jnp.concatenate([cos, cos], axis=-1), (1, 2)), \
        jnp.tile(jnp.concatenate([-sin, sin], axis=-1), (1, 2))


def _bias_ids(seq, past_len, w_buf):
    nt = seq // QB
    r = np.arange(QB)[:, None]
    c = np.arange(QB)[None, :]
    prev = _t5_bucket_np(r - c + QB)
    diag = _t5_bucket_np(r - c)
    qi = np.arange(nt)[:, None, None]
    cmp_p = _t5_bucket_np(qi * QB + r[None] - (c[None] * CMP_STRIDE + CMP_BLOCK - 1)).reshape(nt * QB, QB)
    n_tok = past_len // CMP_STRIDE
    cmp_s = _t5_bucket_np(past_len - (np.arange(n_tok) * CMP_STRIDE + CMP_BLOCK - 1)).reshape(-1, LANES)
    off = np.arange(LANES) % SLC_BLOCK
    slc_s = np.stack([_t5_bucket_np(SLC_BLOCK - off), _t5_bucket_np(2 * SLC_BLOCK - off),
                      _t5_bucket_np(np.full(LANES, 3 * SLC_BLOCK)), _t5_bucket_np(np.zeros(LANES))])
    win_s = _t5_bucket_np(w_buf - np.arange(w_buf)).reshape(-1, LANES)
    parts = [prev, diag, cmp_p, cmp_s, slc_s, win_s]
    rows = sum(p.shape[0] for p in parts)
    pad = -rows % 256
    ids = np.concatenate(parts + [np.zeros((pad, LANES), np.int32)], axis=0).astype(np.int32)
    offs = np.cumsum([0] + [p.shape[0] for p in parts])
    return ids, offs


def kernel(x_prompt, x_sample, p_prompt, p_sample, cache_nsa_kv, state_nsa_win, cache_mla_kv, page_table, t5_table, ln_g, ln_b, nsa_w_in, nsa_cmp_a, nsa_cmp_phi, nsa_w_o, mla_w_in, mla_q_norm, mla_kv_norm, mla_w_q_b, mla_w_kv_b, mla_w_o, ffn_w_in, ffn_w_out, ple_w_gate, ple_w_proj):
    batch, seq, d = x_prompt.shape
    db = x_sample.shape[0]
    depth = ln_g.shape[0]
    assert depth == 2 and x_sample.shape[1] == 1
    n_pages = page_table.shape[1]
    page_size = cache_nsa_kv.shape[2]
    assert page_size == LANES
    past_len = n_pages * page_size
    npb = past_len // SLC_BLOCK
    n_pool = cache_nsa_kv.shape[1]
    w_buf = state_nsa_win.shape[2]
    alpha = (2 * depth) ** 0.25
    q_cols = HEADS * HD
    kv_cols = NSA_KV_SLOTS * HD
    mp = batch * seq

    ids, offs = _bias_ids(seq, past_len, w_buf)
    assert offs[2] == 256
    bias = t5_bias_lookup(t5_table, jnp.asarray(ids), n_rel=1)
    sect = lambda k: bias[:, offs[k]:offs[k + 1]]
    bias_near = jnp.concatenate([sect(0), sect(1)], axis=-1)
    bias_c = sect(2).reshape(HEADS, seq // QB, QB, QB).transpose(1, 0, 2, 3)
    bias_cs = sect(3).reshape(HEADS, -1)
    bias_ss = sect(4)
    bias_slc = bias_ss[:, :3].transpose(1, 0, 2)
    bias_0 = bias_ss[:, 3]
    bias_w = sect(5).reshape(HEADS, w_buf)

    def dense_tail(x, mixed_in, w_o, p, i):
        x = matmul_residual_ln(mixed_in, w_o, 0, x, ln_g[i, 0], ln_b[i, 0], alpha=alpha)
        hmid = swiglu_in(x, ffn_w_in, i)
        x = matmul_residual_ln(hmid, ffn_w_out, i, x, ln_g[i, 1], ln_b[i, 1], alpha=alpha)
        return ple(x, ple_w_gate, p, ple_w_proj, i)

    pp = p_prompt.reshape(depth, mp, -1)
    ps = p_sample.reshape(depth, db, -1)

    xp = x_prompt.reshape(mp, d)
    xs = x_sample.reshape(db, d)
    w_in0 = nsa_w_in[0]
    hp = matmul(xp, w_in0, tn=256)
    hs = matmul(xs, w_in0, tn=256)

    att_p = nsa_prompt_attention(hp, nsa_cmp_a[0], nsa_cmp_phi[0], bias_c, bias_near,
                                 batch=batch, seq=seq)

    hp3 = hp.reshape(batch, seq, -1)
    nsa_kv_prompt = hp3[:, :, q_cols:q_cols + 4 * HD].reshape(1, batch, seq, 4, HD)
    win_rows_p = hp3[:, :, q_cols + 4 * HD:q_cols + kv_cols].reshape(batch, seq, 2, HD)
    if seq >= w_buf:
        nsa_win_prompt = win_rows_p[:, seq - w_buf:][None]
    else:
        nsa_win_prompt = jnp.pad(win_rows_p, ((0, 0), (w_buf - seq, 0), (0, 0), (0, 0)))[None]

    qs = hs[:, :q_cols].reshape(db, HEADS, HD)
    kvn = hs[:, q_cols:q_cols + kv_cols].reshape(db, 1, kv_cols)
    cache_rows = cache_nsa_kv.reshape(cache_nsa_kv.shape[0] * n_pool, page_size * NSA_CACHE_SLOTS, HD)
    o_cmp_s, imp_s = nsa_sample_compressed(qs, page_table, cache_rows, nsa_cmp_a[0], nsa_cmp_phi[0],
                                           bias_cs, past_len=past_len)
    idx = nsa_sample_select(imp_s.reshape(db, -1), npb=npb)
    win_state = state_nsa_win[0].reshape(db, w_buf * 2, HD)
    o_slc_s, o_win_s = nsa_sample_slc_win(qs, kvn, win_state, idx, page_table, cache_rows,
                                          bias_slc, bias_0, bias_w, npb=npb)
    gate_logits = hs[:, q_cols + kv_cols:].reshape(db, 3, HEADS).transpose(1, 0, 2).reshape(3, db * HEADS, 1)
    att_s = nsa_gate_combine(gate_logits, o_cmp_s.reshape(db * HEADS, HD),
                             o_slc_s.reshape(db * HEADS, HD), o_win_s.reshape(db * HEADS, HD))
    att_s = att_s.reshape(db, q_cols)

    nsa_kv_sample = hs[:, q_cols:q_cols + 4 * HD].reshape(1, db, 1, 4, HD)
    new_win = hs[:, q_cols + 4 * HD:q_cols + kv_cols].reshape(db, 1, 2, HD)
    nsa_win_sample = jnp.concatenate([state_nsa_win[0], new_win], axis=1)[:, -w_buf:][None]

    xp = dense_tail(xp, att_p, nsa_w_o, pp, 0)
    xs = dense_tail(xs, att_s, nsa_w_o, ps, 0)

    wqb = mla_w_q_b[0].reshape(Q_LORA, HEADS, HD + QK_ROPE)
    wqb = jnp.concatenate([wqb[:, :, :HD].reshape(Q_LORA, HEADS * HD),
                           wqb[:, :, HD:].reshape(Q_LORA, HEADS * QK_ROPE)], axis=1)
    w_kv_b = mla_w_kv_b[0]
    cos_p, sin_p = _rope_tables(jnp.arange(seq))
    cos_s, sin_s = _rope_tables(jnp.full((db,), past_len))

    tmp = 512
    hp = matmul(xp, mla_w_in[0], tn=256)
    cq_p, ckv_p, rows_p, kr2_p = mla_prep(hp, mla_q_norm[0], mla_kv_norm[0], cos_p, sin_p,
                                          pos_blocks=seq // tmp, tm=tmp)
    q_p = matmul(cq_p, wqb, tn=512)
    qr_p = rope_q(q_p, cos_p, sin_p, pos_blocks=seq // tmp, tm=tmp)
    kvx_p = matmul(ckv_p, w_kv_b, tn=512, out_dtype=MXU_DT)
    att_p = mla_prompt_attention(q_p, qr_p, kvx_p, kr2_p, batch=batch, seq=seq)

    hs = matmul(xs, mla_w_in[0], tn=256)
    cq_s, _, rows_s, _ = mla_prep(hs, mla_q_norm[0], mla_kv_norm[0], cos_s, sin_s, pos_blocks=1, tm=db)
    q_s = matmul(cq_s, wqb, tn=512)
    qr_s = rope_q(q_s, cos_s, sin_s, pos_blocks=1, tm=db)
    q_lat = mla_absorb_q(q_s, w_kv_b).transpose(1, 0, 2)
    cache_mla = jnp.swapaxes(cache_mla_kv, 2, 3).reshape(cache_mla_kv.shape[0] * n_pool, MLA_ROW, page_size)
    o_lat = mla_decode(q_lat, qr_s.reshape(db, HEADS, QK_ROPE), rows_s.reshape(db, 1, MLA_ROW),
                       page_table, cache_mla)
    att_s = mla_absorb_o(o_lat.transpose(1, 0, 2), w_kv_b)

    xp = dense_tail(xp, att_p, mla_w_o, pp, 1)
    xs = dense_tail(xs, att_s, mla_w_o, ps, 1)

    return (xp.reshape(batch, seq, d), xs.reshape(db, 1, d),
            nsa_kv_prompt, nsa_win_prompt, rows_p.reshape(1, batch, seq, MLA_ROW),
            nsa_kv_sample, nsa_win_sample, rows_s.reshape(1, db, 1, MLA_ROW))
```

```python
import functools
import math

import numpy as np
import jax
import jax.numpy as jnp
from jax import lax
from jax.experimental import pallas as pl
from jax.experimental.pallas import tpu as pltpu

F32 = jnp.float32
MXU_DT = jnp.bfloat16
HI = lax.Precision.HIGHEST

HEADS = 16
HD = 128
CMP_STRIDE = 16
CMP_BLOCK = 32
SLC_BLOCK = 64
N_SEL = 16
WINDOW = 512
NSA_KV_SLOTS = 6
NSA_CACHE_SLOTS = 4
NSA_SCALE = HD ** -0.5
Q_LORA = 768
KV_LORA = 512
QK_ROPE = 64
MLA_ROW = KV_LORA + QK_ROPE
MLA_SCALE = (HD + QK_ROPE) ** -0.5
ROPE_THETA = 10000.0
T5_BUCKETS = 32
T5_MAX_DIST = 128
LN_EPS = 1e-5
RMS_EPS = 1e-6
NEG_INF = -1e30
FORCE_SCORE = 1e4
QB = 128
LANES = 128
HEAD_GROUP = 8
FAR_CHUNK = 512
MLA_QBLK = 512
RING = 3

VMEM_LIMIT = 56 * 1024 * 1024


def _cparams(n_axes):
    return pltpu.CompilerParams(dimension_semantics=("arbitrary",) * n_axes,
                                vmem_limit_bytes=VMEM_LIMIT)


def _dot(a, b):
    return jnp.dot(a, b, preferred_element_type=F32)


def _dot_nt(a, b):
    return lax.dot_general(a, b, (((1,), (1,)), ((), ())), preferred_element_type=F32)


def _dot_hi(a, b):
    return jnp.dot(a, b, precision=HI, preferred_element_type=F32)


def _sigmoid(x):
    return 1.0 / (1.0 + jnp.exp(-x))


def _mm_kernel(x_ref, w_ref, o_ref, xb_ref):
    @pl.when(pl.program_id(1) == 0)
    def _():
        xb_ref[...] = x_ref[...].astype(MXU_DT)
    o_ref[...] = _dot(xb_ref[...], w_ref[...].astype(MXU_DT)).astype(o_ref.dtype)


def matmul(x, w, *, tn, out_dtype=F32, tm=1024):
    m, k = x.shape
    n = w.shape[1]
    tm = min(tm, m)
    return pl.pallas_call(
        _mm_kernel,
        grid=(pl.cdiv(m, tm), pl.cdiv(n, tn)),
        in_specs=[pl.BlockSpec((tm, k), lambda i, j: (i, 0)),
                  pl.BlockSpec((k, tn), lambda i, j: (0, j))],
        out_specs=pl.BlockSpec((tm, tn), lambda i, j: (i, j)),
        out_shape=jax.ShapeDtypeStruct((m, n), out_dtype),
        scratch_shapes=[pltpu.VMEM((tm, k), MXU_DT)],
        compiler_params=_cparams(2),
        name="matmul",
    )(x, w)


def _swiglu_kernel(x_ref, w1_ref, w2_ref, o_ref, xb_ref):
    @pl.when(pl.program_id(1) == 0)
    def _():
        xb_ref[...] = x_ref[...].astype(MXU_DT)
    xb = xb_ref[...]
    h1 = _dot(xb, w1_ref[...].astype(MXU_DT))
    h2 = _dot(xb, w2_ref[...].astype(MXU_DT))
    o_ref[...] = (h1 * _sigmoid(h1) * h2).astype(o_ref.dtype)


def swiglu_in(x, w_in, layer, *, tn=512, tm=1024):
    m, k = x.shape
    d_ff = w_in.shape[2] // 2
    assert d_ff % tn == 0
    nff = d_ff // tn
    tm = min(tm, m)
    return pl.pallas_call(
        _swiglu_kernel,
        grid=(pl.cdiv(m, tm), nff),
        in_specs=[pl.BlockSpec((tm, k), lambda i, j: (i, 0)),
                  pl.BlockSpec((None, k, tn), lambda i, j: (layer, 0, j)),
                  pl.BlockSpec((None, k, tn), lambda i, j: (layer, 0, j + nff))],
        out_specs=pl.BlockSpec((tm, tn), lambda i, j: (i, j)),
        out_shape=jax.ShapeDtypeStruct((m, d_ff), MXU_DT),
        scratch_shapes=[pltpu.VMEM((tm, k), MXU_DT)],
        compiler_params=_cparams(2),
        name="swiglu_in",
    )(x, w_in, w_in)


def _mm_ln_kernel(x_ref, w_ref, r_ref, g_ref, b_ref, o_ref, *, alpha, nk):
    k = pl.program_id(1)
    part = _dot(x_ref[...].astype(MXU_DT), w_ref[...].astype(MXU_DT))

    @pl.when(k == 0)
    def _():
        o_ref[...] = part

    @pl.when(k > 0)
    def _():
        o_ref[...] += part

    @pl.when(k == nk - 1)
    def _():
        y = alpha * r_ref[...] + o_ref[...]
        mu = jnp.mean(y, axis=-1, keepdims=True)
        yc = y - mu
        var = jnp.mean(yc * yc, axis=-1, keepdims=True)
        o_ref[...] = yc * lax.rsqrt(var + LN_EPS) * g_ref[...] + b_ref[...]


def matmul_residual_ln(x, w, layer, resid, g, b, *, alpha, tk=512, tm=1024):
    m, k = x.shape
    n = w.shape[2]
    tm = min(tm, m)
    assert k % tk == 0
    nk = k // tk
    return pl.pallas_call(
        functools.partial(_mm_ln_kernel, alpha=alpha, nk=nk),
        grid=(pl.cdiv(m, tm), nk),
        in_specs=[pl.BlockSpec((tm, tk), lambda i, kk: (i, kk)),
                  pl.BlockSpec((None, tk, n), lambda i, kk: (layer, kk, 0)),
                  pl.BlockSpec((tm, n), lambda i, kk: (i, 0), pipeline_mode=pl.Buffered(1)),
                  pl.BlockSpec((1, n), lambda i, kk: (0, 0)),
                  pl.BlockSpec((1, n), lambda i, kk: (0, 0))],
        out_specs=pl.BlockSpec((tm, n), lambda i, kk: (i, 0)),
        out_shape=jax.ShapeDtypeStruct((m, n), F32),
        compiler_params=_cparams(2),
        name="matmul_residual_ln",
    )(x, w, resid, g.reshape(1, n), b.reshape(1, n))


def _ple_kernel(x_ref, wg_ref, p_ref, wp_ref, xt_ref, o_ref, xb_ref):
    @pl.when(pl.program_id(1) == 0)
    def _():
        xb_ref[...] = x_ref[...].astype(MXU_DT)
    gate = _sigmoid(_dot(xb_ref[...], wg_ref[...].astype(MXU_DT)))
    proj = _dot(p_ref[...].astype(MXU_DT), wp_ref[...].astype(MXU_DT))
    o_ref[...] = xt_ref[...] + gate * proj


def ple(x, w_gate, p, w_proj, layer, *, tn=512, tm=1024):
    m, k = x.shape
    n = w_gate.shape[2]
    pe = p.shape[2]
    tm = min(tm, m)
    return pl.pallas_call(
        _ple_kernel,
        grid=(pl.cdiv(m, tm), pl.cdiv(n, tn)),
        in_specs=[pl.BlockSpec((tm, k), lambda i, j: (i, 0)),
                  pl.BlockSpec((None, k, tn), lambda i, j: (layer, 0, j)),
                  pl.BlockSpec((None, tm, pe), lambda i, j: (layer, i, 0)),
                  pl.BlockSpec((None, pe, tn), lambda i, j: (layer, 0, j)),
                  pl.BlockSpec((tm, tn), lambda i, j: (i, j))],
        out_specs=pl.BlockSpec((tm, tn), lambda i, j: (i, j)),
        out_shape=jax.ShapeDtypeStruct((m, n), F32),
        scratch_shapes=[pltpu.VMEM((tm, k), MXU_DT)],
        compiler_params=_cparams(2),
        name="ple",
    )(x, w_gate, p, w_proj, x)


def _t5_bucket_np(dist):
    n = np.maximum(np.asarray(dist, np.int64), 0)
    max_exact = T5_BUCKETS // 2
    nf = np.maximum(n, 1).astype(np.float32)
    scaled = (np.log(nf / np.float32(max_exact)) / np.float32(math.log(T5_MAX_DIST / max_exact))
              * np.float32(T5_BUCKETS - max_exact))
    large = np.minimum(max_exact + scaled.astype(np.int32), T5_BUCKETS - 1)
    return np.where(n < max_exact, n, large).astype(np.int32)


def _bias_kernel(tab_ref, ids_ref, o_ref, *, n_rel):
    ids = ids_ref[...]
    relative = pl.program_id(0) < n_rel
    for h in range(HEADS):
        acc = jnp.zeros(ids.shape, F32)
        for bkt in range(T5_BUCKETS):
            acc = jnp.where(ids == bkt, tab_ref[bkt, h], acc)
        o_ref[h] = acc - jnp.where(relative, tab_ref[T5_BUCKETS - 1, h], 0.0)


def t5_bias_lookup(table, ids, *, n_rel, tr=256):
    r = ids.shape[0]
    assert r % tr == 0
    return pl.pallas_call(
        functools.partial(_bias_kernel, n_rel=n_rel),
        grid=(r // tr,),
        in_specs=[pl.BlockSpec(memory_space=pltpu.SMEM),
                  pl.BlockSpec((tr, LANES), lambda i: (i, 0))],
        out_specs=pl.BlockSpec((HEADS, tr, LANES), lambda i: (0, i, 0)),
        out_shape=jax.ShapeDtypeStruct((HEADS, r, LANES), F32),
        compiler_params=_cparams(1),
        name="t5_bias_lookup",
    )(table, ids)


def _softmax_rows(s):
    m = jnp.max(s, axis=-1, keepdims=True)
    e = jnp.exp(s - m)
    l = jnp.sum(e, axis=-1, keepdims=True)
    return jnp.where(m > 0.5 * NEG_INF, e / l, 0.0)


def _lane_tiles(x):
    return [x[..., t * LANES:(t + 1) * LANES] for t in range(x.shape[-1] // LANES)]


def _fold(op, tiles):
    out = tiles[0]
    for t in tiles[1:]:
        out = op(out, t)
    return out


def _rope_lanes(x, cos_t, sin_t):
    lane = lax.broadcasted_iota(jnp.int32, x.shape, 1)
    first_half = (lane % QK_ROPE) < (QK_ROPE // 2)
    rot = jnp.where(first_half, pltpu.roll(x, LANES - QK_ROPE // 2, 1), pltpu.roll(x, QK_ROPE // 2, 1))
    return x * cos_t + rot * sin_t


def _nsa_prompt_kernel(q_ref, g_ref, kcr_ref, vcr_ref, ksr_ref, vsr_ref, kwr_ref, vwr_ref,
                       a_ref, phi_ref, bc_ref, bn_ref, amat_ref, emat_ref, o_ref,
                       kc_scr, vc_scr, ks_scr, vs_scr, kw_scr, vw_scr, q_scr, selk_scr, a_scr,
                       mx_scr, mb_scr, l_scr, acc_scr, sn_scr, oc_scr, os_scr, ow_scr,
                       *, nt, n_blk):
    i = pl.program_id(1)
    seq = nt * QB
    n_cmp = nt * (QB // CMP_STRIDE) - 1
    wt = WINDOW // QB
    hg = HEAD_GROUP
    gm = hg * QB
    fc = FAR_CHUNK
    tpc = fc // QB

    @pl.when(i == 0)
    def _():
        ks_scr[0:QB, :] = jnp.zeros((QB, HD), MXU_DT)
        vs_scr[0:QB, :] = jnp.zeros((QB, HD), MXU_DT)
        kw_scr[0:wt * QB, :] = jnp.zeros((wt * QB, HD), MXU_DT)
        vw_scr[0:wt * QB, :] = jnp.zeros((wt * QB, HD), MXU_DT)
        ks_scr[QB:QB + seq, :] = ksr_ref[...].astype(MXU_DT)
        vs_scr[QB:QB + seq, :] = vsr_ref[...].astype(MXU_DT)
        kw_scr[wt * QB:wt * QB + seq, :] = kwr_ref[...].astype(MXU_DT)
        vw_scr[wt * QB:wt * QB + seq, :] = vwr_ref[...].astype(MXU_DT)
        for which, (rows_ref, dst) in enumerate(((kcr_ref, kc_scr), (vcr_ref, vc_scr))):
            ng = nt * QB // CMP_STRIDE
            first = jnp.zeros((ng, HD), F32)
            second = jnp.zeros((ng, HD), F32)
            for j in range(CMP_STRIDE):
                xj = rows_ref[pl.ds(j, ng, stride=CMP_STRIDE), :]
                first = first + xj * a_ref[which, j:j + 1, :]
                second = second + xj * a_ref[which, CMP_STRIDE + j:CMP_STRIDE + j + 1, :]
            pre = first + pltpu.roll(second, ng - 1, 0)
            dst[...] = _dot_hi(pre, phi_ref[which]).astype(MXU_DT)

    for h in range(HEADS):
        q_scr[h * QB:(h + 1) * QB, :] = (q_ref[:, h * HD:(h + 1) * HD] * NSA_SCALE).astype(MXU_DT)

    row = lax.broadcasted_iota(jnp.int32, (QB, QB), 0)
    lane = lax.broadcasted_iota(jnp.int32, (QB, QB), 1)
    qpos = i * QB + row
    tri = jnp.where(lane <= row, 0.0, NEG_INF)
    cmp_valid = (qpos >= lane * CMP_STRIDE + (CMP_BLOCK - 1)) & (lane < n_cmp)
    cmp_mask = jnp.where(cmp_valid, 0.0, NEG_INF)

    s = _dot_nt(q_scr[...], kc_scr[...]).reshape(HEADS, QB, QB) + bc_ref[0] + cmp_mask[None]
    p = _softmax_rows(s)
    imp = jnp.sum(p, axis=0)
    oc_scr[...] = _dot(p.reshape(HEADS * QB, QB).astype(MXU_DT), vc_scr[...])

    slc_imp = _dot_hi(imp, amat_ref[...])
    cur = qpos // SLC_BLOCK
    forced = (lane == 0) | (lane == cur) | (lane == cur - 1)
    score = jnp.where(lane > cur, -1.0, jnp.where(forced, FORCE_SCORE, slc_imp))
    score = jnp.where(lane < n_blk, score, -2.0)
    rank = jnp.zeros((QB, QB), F32)
    for b2 in range(n_blk):
        col = score[:, b2:b2 + 1]
        beats = (col > score) | ((col == score) & (lane > b2))
        rank = rank + jnp.where(beats, 1.0, 0.0)
    sel = jnp.where((rank < min(N_SEL, n_blk)) & (lane < n_blk), 1.0, 0.0)
    sel_keys = _dot(sel.astype(MXU_DT), emat_ref[...])
    for kt in range(nt):
        tile = jnp.where(sel_keys[:, kt * QB:(kt + 1) * QB] > 0.5, 0.0, NEG_INF)
        selk_scr[kt] = tile
        a_scr[kt // tpc, :, (kt % tpc) * QB:(kt % tpc + 1) * QB] = jnp.where(kt < i - 1, tile, NEG_INF)
    prev_ok = jnp.where(i > 0, 0.0, NEG_INF)
    near_mask = jnp.concatenate([selk_scr[jnp.maximum(i - 1, 0)] + prev_ok, selk_scr[i] + tri], axis=1)

    zeros = jnp.zeros((QB, QB), F32)
    band = jnp.where(lane >= row, 0.0, NEG_INF)

    def tile_ok(t):
        return jnp.where(i - wt + t >= 0, 0.0, NEG_INF)

    win_far_mask = jnp.concatenate(
        [band + tile_ok(0)] + [zeros + tile_ok(t) for t in range(1, wt - 1)], axis=1)
    win_near_mask = jnp.concatenate([zeros + tile_ok(wt - 1), tri], axis=1)
    n_far = (i + tpc - 2) // tpc
    n0 = pl.multiple_of(i * QB, QB)

    def group(g, carry):
        r0 = pl.multiple_of(g * gm, gm)
        qg = q_scr[pl.ds(r0, gm), :]
        bn = bn_ref[pl.ds(g * hg, hg)]

        def far_scores(c):
            k0 = pl.multiple_of(QB + c * fc, QB)
            s = _dot_nt(qg, ks_scr[pl.ds(k0, fc), :]).reshape(hg, QB, fc) + a_scr[c][None]
            return k0, s

        mx_scr[...] = jnp.full((hg, QB, QB), NEG_INF, F32)

        def far_max(c, carry2):
            _, s = far_scores(c)
            mx_scr[...] = jnp.maximum(mx_scr[...], _fold(jnp.maximum, _lane_tiles(s)))
            return carry2

        lax.fori_loop(0, n_far, far_max, 0)
        sn = _dot_nt(qg, ks_scr[pl.ds(n0, 2 * QB), :]).reshape(hg, QB, 2 * QB) + bn + near_mask[None]
        sn_scr[...] = sn
        mfold = jnp.maximum(mx_scr[...], _fold(jnp.maximum, _lane_tiles(sn)))
        mb_scr[...] = jnp.broadcast_to(jnp.max(mfold, axis=-1, keepdims=True), (hg, QB, QB))
        l_scr[...] = jnp.zeros((hg, QB, QB), F32)
        acc_scr[...] = jnp.zeros((gm, HD), F32)

        def far_pv(c, carry2):
            k0, s = far_scores(c)
            mbv = mb_scr[...]
            ps = [jnp.exp(t - mbv) for t in _lane_tiles(s)]
            l_scr[...] += _fold(jnp.add, ps)
            pm = jnp.concatenate(ps, axis=-1).reshape(gm, fc).astype(MXU_DT)
            acc_scr[...] += _dot(pm, vs_scr[pl.ds(k0, fc), :])
            return carry2

        lax.fori_loop(0, n_far, far_pv, 0)
        mbv = mb_scr[...]
        pn = [jnp.exp(t - mbv) for t in _lane_tiles(sn_scr[...])]
        l = l_scr[...] + _fold(jnp.add, pn)
        pm = jnp.concatenate(pn, axis=-1).reshape(gm, 2 * QB).astype(MXU_DT)
        acc = acc_scr[...] + _dot(pm, vs_scr[pl.ds(n0, 2 * QB), :])
        o_slc = acc.reshape(hg, QB, HD) / jnp.sum(l, axis=-1, keepdims=True)
        os_scr[pl.ds(r0, gm), :] = o_slc.reshape(gm, HD)

        nw = (wt + 1) * QB
        nf = (wt - 1) * QB
        sw = _dot_nt(qg, kw_scr[pl.ds(n0, nw), :]).reshape(hg, QB, nw)
        s_far = sw[:, :, :nf] + win_far_mask[None]
        s_near = sw[:, :, nf:] + bn + win_near_mask[None]
        m = jnp.maximum(jnp.max(s_far, axis=-1, keepdims=True), jnp.max(s_near, axis=-1, keepdims=True))
        e_far = jnp.exp(s_far - m)
        e_near = jnp.exp(s_near - m)
        l = jnp.sum(e_far, axis=-1, keepdims=True) + jnp.sum(e_near, axis=-1, keepdims=True)
        pm = jnp.concatenate([e_far, e_near], axis=-1).reshape(gm, nw).astype(MXU_DT)
        o_win = _dot(pm, vw_scr[pl.ds(n0, nw), :]).reshape(hg, QB, HD) / l
        ow_scr[pl.ds(r0, gm), :] = o_win.reshape(gm, HD)
        return carry

    lax.fori_loop(0, HEADS // hg, group, 0)

    gates = _sigmoid(g_ref[...])
    for h in range(HEADS):
        rows = slice(h * QB, (h + 1) * QB)
        o = (gates[:, h:h + 1] * oc_scr[rows, :] + gates[:, HEADS + h:HEADS + h + 1] * os_scr[rows, :]
             + gates[:, 2 * HEADS + h:2 * HEADS + h + 1] * ow_scr[rows, :])
        o_ref[:, h * HD:(h + 1) * HD] = o.astype(o_ref.dtype)


def nsa_prompt_attention(h, cmp_a, cmp_phi, bias_c, bias_near, *, batch, seq):
    nt = seq // QB
    n_blk = seq // SLC_BLOCK
    wt = WINDOW // QB
    assert seq // CMP_STRIDE == QB and n_blk <= QB and WINDOW % QB == 0 and wt >= 2
    assert FAR_CHUNK % QB == 0 and seq % FAR_CHUNK == 0 and HEADS % HEAD_GROUP == 0
    q_cols = HEADS * HD
    kv_blk0 = q_cols // HD
    gate_blk = kv_blk0 + NSA_KV_SLOTS
    c = np.arange(QB)[:, None]
    b = np.arange(QB)[None, :]
    ratio = SLC_BLOCK // CMP_STRIDE
    amat = (((c >= ratio * b) & (c <= ratio * b + ratio - 1)).astype(np.float32)
            + ((c + 1 >= ratio * b) & (c + 1 <= ratio * b + ratio - 1)).astype(np.float32))
    amat = amat * (b < n_blk)
    emat = (np.arange(seq)[None, :] // SLC_BLOCK == np.arange(QB)[:, None]).astype(np.float32)

    kv_spec = lambda s: pl.BlockSpec((seq, HD), lambda bb, i, s=s: (bb, kv_blk0 + s))
    full = lambda shape: pl.BlockSpec(shape, lambda bb, i: (0,) * len(shape))
    keys_scr = lambda pad_tiles: pltpu.VMEM(((nt + pad_tiles) * QB, HD), MXU_DT)
    heads_scr = lambda dt: pltpu.VMEM((HEADS * QB, HD), dt)
    group_scr = lambda: pltpu.VMEM((HEAD_GROUP, QB, QB), F32)
    return pl.pallas_call(
        functools.partial(_nsa_prompt_kernel, nt=nt, n_blk=n_blk),
        grid=(batch, nt),
        in_specs=[pl.BlockSpec((QB, q_cols), lambda bb, i: (bb * nt + i, 0)),
                  pl.BlockSpec((QB, LANES), lambda bb, i: (bb * nt + i, gate_blk))]
                 + [kv_spec(s) for s in range(NSA_KV_SLOTS)]
                 + [full((2, CMP_BLOCK, HD)), full((2, HD, HD)),
                    pl.BlockSpec((1, HEADS, QB, QB), lambda bb, i: (i, 0, 0, 0)),
                    full((HEADS, QB, 2 * QB)), full((QB, QB)), full((QB, seq))],
        out_specs=pl.BlockSpec((QB, q_cols), lambda bb, i: (bb * nt + i, 0)),
        out_shape=jax.ShapeDtypeStruct((batch * seq, q_cols), MXU_DT),
        scratch_shapes=[pltpu.VMEM((QB, HD), MXU_DT), pltpu.VMEM((QB, HD), MXU_DT),
                        keys_scr(1), keys_scr(1), keys_scr(wt), keys_scr(wt),
                        heads_scr(MXU_DT), pltpu.VMEM((nt, QB, QB), F32),
                        pltpu.VMEM((seq // FAR_CHUNK, QB, FAR_CHUNK), F32),
                        group_scr(), group_scr(), group_scr(),
                        pltpu.VMEM((HEAD_GROUP * QB, HD), F32),
                        pltpu.VMEM((HEAD_GROUP, QB, 2 * QB), F32),
                        heads_scr(F32), heads_scr(F32), heads_scr(F32)],
        compiler_params=_cparams(2),
        name="nsa_prompt_attention",
    )(h, h, h, h, h, h, h, h, cmp_a, cmp_phi, bias_c, bias_near,
      jnp.asarray(amat), jnp.asarray(emat, MXU_DT))


def _nsa_sample_cmp_kernel(pt_ref, q_ref, w_ref, phi_ref, bias_ref, cache_ref, oc_ref, imp_ref,
                           buf, sem, part_scr, kc_scr, vc_scr, *, n_chunks, pages, past_len):
    b = pl.program_id(0)
    nb = pl.num_programs(0)
    prow = LANES * NSA_CACHE_SLOTS
    rows = pages * prow
    grow = CMP_STRIDE * NSA_CACHE_SLOTS
    groups = pages * LANES // CMP_STRIDE
    sub = 8
    nv = grow // sub

    def chunk_copies(bb, ch, slot):
        cps = []
        for p in range(pages):
            page = pt_ref[bb, ch * pages + p]
            cps.append(pltpu.make_async_copy(cache_ref.at[page],
                                             buf.at[slot, pl.ds(p * prow, prow), :], sem.at[slot]))
        if ch + 1 < n_chunks:
            page = pt_ref[bb, (ch + 1) * pages]
            cps.append(pltpu.make_async_copy(cache_ref.at[page, pl.ds(0, grow), :],
                                             buf.at[slot, pl.ds(rows, grow), :], sem.at[slot]))
        return cps

    ahead = RING - 1

    @pl.when(b == 0)
    def _():
        for g0 in range(ahead):
            for cp in chunk_copies(0, g0, g0):
                cp.start()

    for ch in range(n_chunks):
        g = b * n_chunks + ch
        slot = g % RING
        nxt = (g + ahead) % RING
        if ch + ahead < n_chunks:
            for cp in chunk_copies(b, ch + ahead, nxt):
                cp.start()
        else:
            @pl.when(b + 1 < nb)
            def _():
                for cp in chunk_copies(b + 1, ch + ahead - n_chunks, nxt):
                    cp.start()
        for cp in chunk_copies(b, ch, slot):
            cp.wait()
        if ch == n_chunks - 1:
            buf[slot, rows:rows + grow, :] = jnp.zeros((grow, HD), F32)
        x0 = buf[slot, 0:rows, :].reshape(groups, nv, sub, HD)
        x1 = buf[slot, grow:rows + grow, :].reshape(groups, nv, sub, HD)
        part = x0[:, 0] * w_ref[0, 0] + x1[:, 0] * w_ref[1, 0]
        for t in range(1, nv):
            part = part + x0[:, t] * w_ref[0, t] + x1[:, t] * w_ref[1, t]
        part_scr[...] = part.reshape(groups * sub, HD)
        for kv, dst in enumerate((kc_scr, vc_scr)):
            pre = (part_scr[pl.ds(kv, groups, stride=sub), :]
                   + part_scr[pl.ds(NSA_CACHE_SLOTS + kv, groups, stride=sub), :])
            dst[ch * groups:(ch + 1) * groups, :] = _dot_hi(pre, phi_ref[kv]).astype(MXU_DT)

    n_tok = n_chunks * groups
    q = (q_ref[0] * NSA_SCALE).astype(MXU_DT)
    tok = lax.broadcasted_iota(jnp.int32, (HEADS, n_tok), 1)
    valid = tok * CMP_STRIDE + (CMP_BLOCK - 1) <= past_len
    s = _dot_nt(q, kc_scr[...]) + bias_ref[...] + jnp.where(valid, 0.0, NEG_INF)
    p = _softmax_rows(s)
    oc_ref[0] = _dot(p.astype(MXU_DT), vc_scr[...])
    imp_ref[0] = jnp.sum(p, axis=0, keepdims=True)


def nsa_sample_compressed(q, page_table, cache_rows, cmp_a, cmp_phi, bias_c, *, past_len, pages=16):
    db, n_pages = page_table.shape
    assert n_pages % pages == 0
    n_chunks = n_pages // pages
    assert n_chunks >= RING - 1
    rows = (pages * LANES + CMP_STRIDE) * NSA_CACHE_SLOTS
    n_tok = n_pages * LANES // CMP_STRIDE
    nv = CMP_STRIDE // 2
    w = cmp_a.reshape(2, 2, nv, 2, HD).transpose(1, 2, 3, 0, 4)
    w = jnp.pad(w, ((0, 0), (0, 0), (0, 0), (0, NSA_CACHE_SLOTS - 2), (0, 0)))
    w = w.reshape(2, nv, 2 * NSA_CACHE_SLOTS, HD)
    grid_spec = pltpu.PrefetchScalarGridSpec(
        num_scalar_prefetch=1,
        grid=(db,),
        in_specs=[pl.BlockSpec((1, HEADS, HD), lambda b, pt: (b, 0, 0)),
                  pl.BlockSpec((2, nv, 2 * NSA_CACHE_SLOTS, HD), lambda b, pt: (0, 0, 0, 0)),
                  pl.BlockSpec((2, HD, HD), lambda b, pt: (0, 0, 0)),
                  pl.BlockSpec((HEADS, n_tok), lambda b, pt: (0, 0)),
                  pl.BlockSpec(memory_space=pl.ANY)],
        out_specs=[pl.BlockSpec((1, HEADS, HD), lambda b, pt: (b, 0, 0)),
                   pl.BlockSpec((1, 1, n_tok), lambda b, pt: (b, 0, 0))],
        scratch_shapes=[pltpu.VMEM((RING, rows, HD), F32),
                        pltpu.SemaphoreType.DMA((RING,)),
                        pltpu.VMEM((pages * LANES // CMP_STRIDE * 8, HD), F32),
                        pltpu.VMEM((n_tok, HD), MXU_DT), pltpu.VMEM((n_tok, HD), MXU_DT)])
    return pl.pallas_call(
        functools.partial(_nsa_sample_cmp_kernel, n_chunks=n_chunks, pages=pages, past_len=past_len),
        grid_spec=grid_spec,
        out_shape=[jax.ShapeDtypeStruct((db, HEADS, HD), F32),
                   jax.ShapeDtypeStruct((db, 1, n_tok), F32)],
        compiler_params=_cparams(1),
        name="nsa_sample_compressed",
    )(page_table, q, w, cmp_phi, bias_c, cache_rows)


def _select_kernel(imp_ref, amat_ref, idx_ref, *, npb, n_pick):
    slc_imp = _dot_hi(imp_ref[...], amat_ref[...])
    lane = lax.broadcasted_iota(jnp.int32, slc_imp.shape, 1).astype(F32)
    forced = (lane == 0.0) | (lane == float(npb - 1))
    score = jnp.where(forced, FORCE_SCORE, slc_imp)
    score = jnp.where(lane < float(npb), score, -2.0)
    out_lane = lax.broadcasted_iota(jnp.int32, idx_ref.shape, 1)
    out = jnp.zeros(idx_ref.shape, F32)
    for t in range(n_pick):
        best = jnp.max(score, axis=-1, keepdims=True)
        first = jnp.min(jnp.where(score == best, lane, 1e9), axis=-1, keepdims=True)
        out = jnp.where(out_lane == t, first, out)
        score = jnp.where(lane == first, -3.0, score)
    idx_ref[...] = out.astype(jnp.int32)


def nsa_sample_select(imp, *, npb):
    db, n_tok = imp.shape
    assert npb >= N_SEL
    nbp = -(-npb // LANES) * LANES
    c = np.arange(n_tok)[:, None]
    b = np.arange(nbp)[None, :]
    ratio = SLC_BLOCK // CMP_STRIDE
    amat = (((c >= ratio * b) & (c <= ratio * b + ratio - 1)).astype(np.float32)
            + ((c + 1 >= ratio * b) & (c + 1 <= ratio * b + ratio - 1)).astype(np.float32))
    amat = amat * (b < npb)
    return pl.pallas_call(
        functools.partial(_select_kernel, npb=npb, n_pick=N_SEL - 1),
        out_shape=jax.ShapeDtypeStruct((db, LANES), jnp.int32),
        compiler_params=pltpu.CompilerParams(vmem_limit_bytes=VMEM_LIMIT),
        name="nsa_sample_select",
    )(imp, jnp.asarray(amat))


def _nsa_sample_slc_win_kernel(idx_ref, pt_ref, q_ref, kvn_ref, win_ref, bsl_ref, b0_ref, bw_ref,
                               cache_ref, osl_ref, owin_ref, kbuf, sem, *, n_pick, npb):
    b = pl.program_id(0)
    nb = pl.num_programs(0)
    bpp = LANES // SLC_BLOCK
    brow = SLC_BLOCK * NSA_CACHE_SLOTS

    def block_copies(bb, slot):
        cps = []
        for t in range(n_pick):
            blk = idx_ref[bb, t]
            page = pt_ref[bb, blk // bpp]
            off = pl.multiple_of((blk % bpp) * brow, brow)
            cps.append(pltpu.make_async_copy(cache_ref.at[page, pl.ds(off, brow), :],
                                             kbuf.at[slot, pl.ds(t * brow, brow), :], sem.at[slot]))
        return cps

    slot = b % 2

    @pl.when(b == 0)
    def _():
        for cp in block_copies(0, 0):
            cp.start()

    @pl.when(b + 1 < nb)
    def _():
        for cp in block_copies(b + 1, 1 - slot):
            cp.start()

    for cp in block_copies(b, slot):
        cp.wait()

    n_keys = (n_pick + 1) * SLC_BLOCK
    kvn = kvn_ref[0]
    rowi = lax.broadcasted_iota(jnp.int32, (SLC_BLOCK, HD), 0)

    def gathered(cache_slot):
        parts = [kbuf[slot, pl.ds(t * brow + cache_slot, SLC_BLOCK, stride=NSA_CACHE_SLOTS), :]
                 for t in range(n_pick)]
        new_row = kvn[:, cache_slot * HD:(cache_slot + 1) * HD]
        parts.append(jnp.where(rowi == 0, new_row, 0.0))
        return jnp.concatenate(parts, axis=0).astype(MXU_DT)

    q = (q_ref[0] * NSA_SCALE).astype(MXU_DT)
    keys = gathered(2)
    vals = gathered(3)
    lane = lax.broadcasted_iota(jnp.int32, (HEADS, LANES), 1)

    def slot_bias(t):
        if t == n_pick:
            return b0_ref[...]
        blk = idx_ref[b, t]
        return jnp.where(blk == npb - 1, bsl_ref[0], jnp.where(blk == npb - 2, bsl_ref[1], bsl_ref[2]))

    tiles = []
    for u in range(n_keys // LANES):
        tiles.append(jnp.where(lane < SLC_BLOCK, slot_bias(2 * u), slot_bias(2 * u + 1)))
    bias = jnp.concatenate(tiles, axis=1)
    col = lax.broadcasted_iota(jnp.int32, (HEADS, n_keys), 1)
    s = _dot_nt(q, keys) + bias + jnp.where(col <= n_pick * SLC_BLOCK, 0.0, NEG_INF)
    p = _softmax_rows(s)
    osl_ref[0] = _dot(p.astype(MXU_DT), vals)

    w_buf = win_ref.shape[1] // 2
    wk = win_ref[0, pl.ds(0, w_buf, stride=2), :].astype(MXU_DT)
    wv = win_ref[0, pl.ds(1, w_buf, stride=2), :].astype(MXU_DT)
    sw = _dot_nt(q, wk) + bw_ref[...]
    new_k = kvn[:, 4 * HD:5 * HD].astype(MXU_DT).astype(F32)
    new_v = kvn[:, 5 * HD:6 * HD].astype(MXU_DT).astype(F32)
    s_new = jnp.sum(q.astype(F32) * new_k, axis=-1, keepdims=True) + b0_ref[:, 0:1]
    m = jnp.maximum(jnp.max(sw, axis=-1, keepdims=True), s_new)
    pw = jnp.exp(sw - m)
    pn = jnp.exp(s_new - m)
    l = jnp.sum(pw, axis=-1, keepdims=True) + pn
    pn_r = pn.astype(MXU_DT).astype(F32)
    owin_ref[0] = (_dot(pw.astype(MXU_DT), wv) + pn_r * new_v) / l


def nsa_sample_slc_win(q, kv_new, win_state, idx, page_table, cache_rows, bias_slc, bias_0, bias_w,
                       *, npb):
    db = q.shape[0]
    n_pick = N_SEL - 1
    w_buf = win_state.shape[1] // 2
    assert w_buf <= WINDOW and ((n_pick + 1) * SLC_BLOCK) % LANES == 0
    n_keys = (n_pick + 1) * SLC_BLOCK
    grid_spec = pltpu.PrefetchScalarGridSpec(
        num_scalar_prefetch=2,
        grid=(db,),
        in_specs=[pl.BlockSpec((1, HEADS, HD), lambda b, ix, pt: (b, 0, 0)),
                  pl.BlockSpec((1, 1, NSA_KV_SLOTS * HD), lambda b, ix, pt: (b, 0, 0)),
                  pl.BlockSpec((1, 2 * w_buf, HD), lambda b, ix, pt: (b, 0, 0)),
                  pl.BlockSpec((3, HEADS, LANES), lambda b, ix, pt: (0, 0, 0)),
                  pl.BlockSpec((HEADS, LANES), lambda b, ix, pt: (0, 0)),
                  pl.BlockSpec((HEADS, w_buf), lambda b, ix, pt: (0, 0)),
                  pl.BlockSpec(memory_space=pl.ANY)],
        out_specs=[pl.BlockSpec((1, HEADS, HD), lambda b, ix, pt: (b, 0, 0)),
                   pl.BlockSpec((1, HEADS, HD), lambda b, ix, pt: (b, 0, 0))],
        scratch_shapes=[pltpu.VMEM((2, n_pick * SLC_BLOCK * NSA_CACHE_SLOTS, HD), F32),
                        pltpu.SemaphoreType.DMA((2,))])
    return pl.pallas_call(
        functools.partial(_nsa_sample_slc_win_kernel, n_pick=n_pick, npb=npb),
        grid_spec=grid_spec,
        out_shape=[jax.ShapeDtypeStruct((db, HEADS, HD), F32),
                   jax.ShapeDtypeStruct((db, HEADS, HD), F32)],
        compiler_params=_cparams(1),
        name="nsa_sample_slc_win",
    )(idx, page_table, q, kv_new, win_state, bias_slc, bias_0, bias_w, cache_rows)


def _gate_combine_kernel(g_ref, oc_ref, os_ref, ow_ref, o_ref):
    g = _sigmoid(g_ref[...])
    o_ref[...] = (g[0] * oc_ref[...] + g[1] * os_ref[...] + g[2] * ow_ref[...]).astype(o_ref.dtype)


def nsa_gate_combine(gate_logits, o_cmp, o_slc, o_win):
    r = o_cmp.shape[0]
    return pl.pallas_call(
        _gate_combine_kernel,
        out_shape=jax.ShapeDtypeStruct((r, HD), MXU_DT),
        compiler_params=pltpu.CompilerParams(vmem_limit_bytes=VMEM_LIMIT),
        name="nsa_gate_combine",
    )(gate_logits, o_cmp, o_slc, o_win)


def _mla_prep_kernel(h_ref, gq_ref, gkv_ref, cos_ref, sin_ref, dup_ref,
                     cq_ref, ckv_ref, rows_ref, kr_ref):
    h = h_ref[...]

    def rms(x, g):
        return x * lax.rsqrt(jnp.mean(x * x, axis=-1, keepdims=True) + RMS_EPS) * g

    cq_ref[...] = rms(h[:, :Q_LORA], gq_ref[...]).astype(cq_ref.dtype)
    ckv = rms(h[:, Q_LORA:Q_LORA + KV_LORA], gkv_ref[...])
    ckv_ref[...] = ckv.astype(ckv_ref.dtype)
    kr2 = _dot_hi(h[:, Q_LORA + KV_LORA:Q_LORA + KV_LORA + QK_ROPE], dup_ref[...])
    kr2 = _rope_lanes(kr2, cos_ref[...], sin_ref[...])
    kr_ref[...] = kr2
    rows_ref[:, :KV_LORA] = ckv
    rows_ref[:, KV_LORA:] = kr2[:, :QK_ROPE]


def mla_prep(h, g_q, g_kv, cos_t, sin_t, *, pos_blocks, tm=512):
    m = h.shape[0]
    tm = min(tm, m)
    dup = np.concatenate([np.eye(QK_ROPE, dtype=np.float32)] * 2, axis=1)
    return pl.pallas_call(
        _mla_prep_kernel,
        grid=(m // tm,),
        in_specs=[pl.BlockSpec((tm, h.shape[1]), lambda i: (i, 0)),
                  pl.BlockSpec((1, Q_LORA), lambda i: (0, 0)),
                  pl.BlockSpec((1, KV_LORA), lambda i: (0, 0)),
                  pl.BlockSpec((tm, LANES), lambda i: (i % pos_blocks, 0)),
                  pl.BlockSpec((tm, LANES), lambda i: (i % pos_blocks, 0)),
                  pl.BlockSpec((QK_ROPE, LANES), lambda i: (0, 0))],
        out_specs=[pl.BlockSpec((tm, Q_LORA), lambda i: (i, 0)),
                   pl.BlockSpec((tm, KV_LORA), lambda i: (i, 0)),
                   pl.BlockSpec((tm, MLA_ROW), lambda i: (i, 0)),
                   pl.BlockSpec((tm, LANES), lambda i: (i, 0))],
        out_shape=[jax.ShapeDtypeStruct((m, Q_LORA), MXU_DT),
                   jax.ShapeDtypeStruct((m, KV_LORA), MXU_DT),
                   jax.ShapeDtypeStruct((m, MLA_ROW), F32),
                   jax.ShapeDtypeStruct((m, LANES), F32)],
        compiler_params=_cparams(1),
        name="mla_prep",
    )(h, g_q.reshape(1, -1), g_kv.reshape(1, -1), cos_t, sin_t, jnp.asarray(dup))


def _rope_q_kernel(q_ref, cos_ref, sin_ref, o_ref):
    cos_t = cos_ref[...]
    sin_t = sin_ref[...]
    for g in range(q_ref.shape[1] // LANES):
        cols = slice(g * LANES, (g + 1) * LANES)
        o_ref[:, cols] = _rope_lanes(q_ref[:, cols], cos_t, sin_t)


def rope_q(q, cos_t, sin_t, *, pos_blocks, tm=512):
    m = q.shape[0]
    tm = min(tm, m)
    wr = HEADS * QK_ROPE
    assert (HEADS * HD) % wr == 0
    col_blk = HEADS * HD // wr
    return pl.pallas_call(
        _rope_q_kernel,
        grid=(m // tm,),
        in_specs=[pl.BlockSpec((tm, wr), lambda i: (i, col_blk)),
                  pl.BlockSpec((tm, LANES), lambda i: (i % pos_blocks, 0)),
                  pl.BlockSpec((tm, LANES), lambda i: (i % pos_blocks, 0))],
        out_specs=pl.BlockSpec((tm, wr), lambda i: (i, 0)),
        out_shape=jax.ShapeDtypeStruct((m, wr), F32),
        compiler_params=_cparams(1),
        name="rope_q",
    )(q, cos_t, sin_t)


def _mla_prompt_kernel(qn_ref, qr_ref, kv_ref, kr_ref, o_ref, qc_scr, kc_scr, v_scr, *, seq):
    h = pl.program_id(1)
    blk = MLA_QBLK
    nq = seq // blk
    lane = lax.broadcasted_iota(jnp.int32, (blk, LANES), 1)
    for qb in range(nq):
        rows = slice(qb * blk, (qb + 1) * blk)
        qr = jnp.where((lane // QK_ROPE) == (h % 2), qr_ref[rows, :], 0.0)
        qc_scr[rows, 0:HD] = (qn_ref[rows, :] * MLA_SCALE).astype(MXU_DT)
        qc_scr[rows, HD:2 * HD] = (qr * MLA_SCALE).astype(MXU_DT)
        kc_scr[rows, 0:HD] = kv_ref[rows, 0:HD].astype(MXU_DT)
        kc_scr[rows, HD:2 * HD] = kr_ref[rows, :].astype(MXU_DT)
        v_scr[rows, :] = kv_ref[rows, HD:2 * HD].astype(MXU_DT)

    r = lax.broadcasted_iota(jnp.int32, (blk, blk), 0)
    c = lax.broadcasted_iota(jnp.int32, (blk, blk), 1)
    tri = jnp.where(c <= r, 0.0, NEG_INF)
    for qb in range(nq):
        d0 = qb * blk
        q = qc_scr[d0:d0 + blk, :]
        s_d = _dot_nt(q, kc_scr[d0:d0 + blk, :]) + tri
        m = jnp.max(s_d, axis=-1, keepdims=True)
        if qb > 0:
            s_o = _dot_nt(q, kc_scr[0:d0, :])
            m = jnp.maximum(m, jnp.max(s_o, axis=-1, keepdims=True))
        p_d = jnp.exp(s_d - m)
        l = jnp.sum(p_d, axis=-1, keepdims=True)
        o = _dot(p_d.astype(MXU_DT), v_scr[d0:d0 + blk, :])
        if qb > 0:
            p_o = jnp.exp(s_o - m)
            l = l + jnp.sum(p_o, axis=-1, keepdims=True)
            o = o + _dot(p_o.astype(MXU_DT), v_scr[0:d0, :])
        o_ref[d0:d0 + blk, :] = (o / l).astype(o_ref.dtype)


def mla_prompt_attention(q, qr, kvx, kr2, *, batch, seq):
    assert seq % MLA_QBLK == 0
    return pl.pallas_call(
        functools.partial(_mla_prompt_kernel, seq=seq),
        grid=(batch, HEADS),
        in_specs=[pl.BlockSpec((seq, HD), lambda b, h: (b, h)),
                  pl.BlockSpec((seq, LANES), lambda b, h: (b, h // 2)),
                  pl.BlockSpec((seq, 2 * HD), lambda b, h: (b, h)),
                  pl.BlockSpec((seq, LANES), lambda b, h: (b, 0))],
        out_specs=pl.BlockSpec((seq, HD), lambda b, h: (b, h)),
        out_shape=jax.ShapeDtypeStruct((batch * seq, HEADS * HD), MXU_DT),
        scratch_shapes=[pltpu.VMEM((seq, 2 * HD), MXU_DT), pltpu.VMEM((seq, 2 * HD), MXU_DT),
                        pltpu.VMEM((seq, HD), MXU_DT)],
        compiler_params=_cparams(2),
        name="mla_prompt_attention",
    )(q, qr, kvx, kr2)


def _absorb_q_kernel(q_ref, w_ref, o_ref):
    o_ref[0] = _dot_nt(q_ref[...].astype(MXU_DT), w_ref[...].astype(MXU_DT))


def mla_absorb_q(q, w_kv_b):
    m = q.shape[0]
    return pl.pallas_call(
        _absorb_q_kernel,
        grid=(HEADS,),
        in_specs=[pl.BlockSpec((m, HD), lambda h: (0, h)),
                  pl.BlockSpec((KV_LORA, HD), lambda h: (0, 2 * h))],
        out_specs=pl.BlockSpec((1, m, KV_LORA), lambda h: (h, 0, 0)),
        out_shape=jax.ShapeDtypeStruct((HEADS, m, KV_LORA), F32),
        compiler_params=_cparams(1),
        name="mla_absorb_q",
    )(q, w_kv_b)


def _absorb_o_kernel(o_ref, w_ref, out_ref):
    out_ref[...] = _dot(o_ref[0].astype(MXU_DT), w_ref[...].astype(MXU_DT)).astype(out_ref.dtype)


def mla_absorb_o(o_lat, w_kv_b):
    m = o_lat.shape[1]
    return pl.pallas_call(
        _absorb_o_kernel,
        grid=(HEADS,),
        in_specs=[pl.BlockSpec((1, m, KV_LORA), lambda h: (h, 0, 0)),
                  pl.BlockSpec((KV_LORA, HD), lambda h: (0, 2 * h + 1))],
        out_specs=pl.BlockSpec((m, HD), lambda h: (0, h)),
        out_shape=jax.ShapeDtypeStruct((m, HEADS * HD), MXU_DT),
        compiler_params=_cparams(1),
        name="mla_absorb_o",
    )(o_lat, w_kv_b)


def _mla_decode_kernel(pt_ref, ql_ref, qr_ref, new_ref, cache_ref, o_ref, buf, sem,
                       *, n_chunks, pages):
    b = pl.program_id(0)
    nb = pl.num_programs(0)

    def chunk_copies(bb, ch, slot):
        cps = []
        for p in range(pages):
            page = pt_ref[bb, ch * pages + p]
            cps.append(pltpu.make_async_copy(cache_ref.at[page], buf.at[slot, p], sem.at[slot]))
        return cps

    ahead = RING - 1

    @pl.when(b == 0)
    def _():
        for g0 in range(ahead):
            for cp in chunk_copies(0, g0, g0):
                cp.start()

    ql = (ql_ref[0] * MLA_SCALE).astype(MXU_DT)
    qr = (qr_ref[0] * MLA_SCALE).astype(MXU_DT)

    def chunk_step(ch, carry):
        m, l, acc = carry
        g = b * n_chunks + ch
        slot = g % RING
        nxt = (g + ahead) % RING

        @pl.when(ch + ahead < n_chunks)
        def _():
            for cp in chunk_copies(b, ch + ahead, nxt):
                cp.start()

        @pl.when((ch + ahead >= n_chunks) & (b + 1 < nb))
        def _():
            for cp in chunk_copies(b + 1, ch + ahead - n_chunks, nxt):
                cp.start()

        for cp in chunk_copies(b, ch, slot):
            cp.wait()
        s = jnp.concatenate(
            [_dot(ql, buf[slot, p, 0:KV_LORA, :].astype(MXU_DT))
             + _dot(qr, buf[slot, p, KV_LORA:MLA_ROW, :].astype(MXU_DT))
             for p in range(pages)], axis=1)
        m_new = jnp.maximum(m, jnp.max(s, axis=-1, keepdims=True))
        alpha = jnp.exp(m - m_new)
        pr = jnp.exp(s - m_new)
        l = l * alpha + jnp.sum(pr, axis=-1, keepdims=True)
        pr = pr.astype(MXU_DT)
        acc = acc * alpha
        for p in range(pages):
            acc = acc + _dot_nt(pr[:, p * LANES:(p + 1) * LANES],
                                buf[slot, p, 0:KV_LORA, :].astype(MXU_DT))
        return m_new, l, acc

    m, l, acc = lax.fori_loop(
        0, n_chunks, chunk_step,
        (jnp.full((HEADS, 1), NEG_INF, F32), jnp.zeros((HEADS, 1), F32),
         jnp.zeros((HEADS, KV_LORA), F32)))

    new = new_ref[0].astype(MXU_DT).astype(F32)
    s_new = (jnp.sum(ql.astype(F32) * new[:, :KV_LORA], axis=-1, keepdims=True)
             + jnp.sum(qr.astype(F32) * new[:, KV_LORA:], axis=-1, keepdims=True))
    m_f = jnp.maximum(m, s_new)
    alpha = jnp.exp(m - m_f)
    pn = jnp.exp(s_new - m_f)
    pn_r = pn.astype(MXU_DT).astype(F32)
    o_ref[0] = (acc * alpha + pn_r * new[:, :KV_LORA]) / (l * alpha + pn)


def mla_decode(q_lat, q_rope, rows_new, page_table, cache, *, pages=16):
    db, n_pages = page_table.shape
    assert n_pages % pages == 0
    n_chunks = n_pages // pages
    assert n_chunks >= RING - 1
    grid_spec = pltpu.PrefetchScalarGridSpec(
        num_scalar_prefetch=1,
        grid=(db,),
        in_specs=[pl.BlockSpec((1, HEADS, KV_LORA), lambda b, pt: (b, 0, 0)),
                  pl.BlockSpec((1, HEADS, QK_ROPE), lambda b, pt: (b, 0, 0)),
                  pl.BlockSpec((1, 1, MLA_ROW), lambda b, pt: (b, 0, 0)),
                  pl.BlockSpec(memory_space=pl.ANY)],
        out_specs=pl.BlockSpec((1, HEADS, KV_LORA), lambda b, pt: (b, 0, 0)),
        scratch_shapes=[pltpu.VMEM((RING, pages, MLA_ROW, LANES), F32),
                        pltpu.SemaphoreType.DMA((RING,))])
    return pl.pallas_call(
        functools.partial(_mla_decode_kernel, n_chunks=n_chunks, pages=pages),
        grid_spec=grid_spec,
        out_shape=jax.ShapeDtypeStruct((db, HEADS, KV_LORA), F32),
        compiler_params=_cparams(1),
        name="mla_decode",
    )(page_table, q_lat, q_rope, rows_new, cache)


def _rope_tables(pos):
    inv = ROPE_THETA ** (-jnp.arange(0, QK_ROPE, 2, dtype=F32) / QK_ROPE)
    ang = pos.astype(F32)[:, None] * inv[None, :]
    cos, sin = jnp.cos(ang), jnp.sin(ang)
    return jnp.tile(jnp.concatenate([cos, cos], axis=-1), (1, 2)), \
        jnp.tile(jnp.concatenate([-sin, sin], axis=-1), (1, 2))


def _bias_ids(seq, past_len, w_buf):
    nt = seq // QB
    r = np.arange(QB)[:, None]
    c = np.arange(QB)[None, :]
    prev = _t5_bucket_np(r - c + QB)
    diag = _t5_bucket_np(r - c)
    qi = np.arange(nt)[:, None, None]
    cmp_p = _t5_bucket_np(qi * QB + r[None] - (c[None] * CMP_STRIDE + CMP_BLOCK - 1)).reshape(nt * QB, QB)
    n_tok = past_len // CMP_STRIDE
    cmp_s = _t5_bucket_np(past_len - (np.arange(n_tok) * CMP_STRIDE + CMP_BLOCK - 1)).reshape(-1, LANES)
    off = np.arange(LANES) % SLC_BLOCK
    slc_s = np.stack([_t5_bucket_np(SLC_BLOCK - off), _t5_bucket_np(2 * SLC_BLOCK - off),
                      _t5_bucket_np(np.full(LANES, 3 * SLC_BLOCK)), _t5_bucket_np(np.zeros(LANES))])
    win_s = _t5_bucket_np(w_buf - np.arange(w_buf)).reshape(-1, LANES)
    parts = [prev, diag, cmp_p, cmp_s, slc_s, win_s]
    rows = sum(p.shape[0] for p in parts)
    pad = -rows % 256
    ids = np.concatenate(parts + [np.zeros((pad, LANES), np.int32)], axis=0).astype(np.int32)
    offs = np.cumsum([0] + [p.shape[0] for p in parts])
    return ids, offs


def kernel(x_prompt, x_sample, p_prompt, p_sample, cache_nsa_kv, state_nsa_win, cache_mla_kv, page_table, t5_table, ln_g, ln_b, nsa_w_in, nsa_cmp_a, nsa_cmp_phi, nsa_w_o, mla_w_in, mla_q_norm, mla_kv_norm, mla_w_q_b, mla_w_kv_b, mla_w_o, ffn_w_in, ffn_w_out, ple_w_gate, ple_w_proj):
    batch, seq, d = x_prompt.shape
    db = x_sample.shape[0]
    depth = ln_g.shape[0]
    assert depth == 2 and x_sample.shape[1] == 1
    n_pages = page_table.shape[1]
    page_size = cache_nsa_kv.shape[2]
    assert page_size == LANES
    past_len = n_pages * page_size
    npb = past_len // SLC_BLOCK
    n_pool = cache_nsa_kv.shape[1]
    w_buf = state_nsa_win.shape[2]
    alpha = (2 * depth) ** 0.25
    q_cols = HEADS * HD
    kv_cols = NSA_KV_SLOTS * HD
    mp = batch * seq

    ids, offs = _bias_ids(seq, past_len, w_buf)
    assert offs[2] == 256
    bias = t5_bias_lookup(t5_table, jnp.asarray(ids), n_rel=1)
    sect = lambda k: bias[:, offs[k]:offs[k + 1]]
    bias_near = jnp.concatenate([sect(0), sect(1)], axis=-1)
    bias_c = sect(2).reshape(HEADS, seq // QB, QB, QB).transpose(1, 0, 2, 3)
    bias_cs = sect(3).reshape(HEADS, -1)
    bias_ss = sect(4)
    bias_slc = bias_ss[:, :3].transpose(1, 0, 2)
    bias_0 = bias_ss[:, 3]
    bias_w = sect(5).reshape(HEADS, w_buf)

    def dense_tail(x, mixed_in, w_o, p, i):
        x = matmul_residual_ln(mixed_in, w_o, 0, x, ln_g[i, 0], ln_b[i, 0], alpha=alpha)
        hmid = swiglu_in(x, ffn_w_in, i)
        x = matmul_residual_ln(hmid, ffn_w_out, i, x, ln_g[i, 1], ln_b[i, 1], alpha=alpha)
        return ple(x, ple_w_gate, p, ple_w_proj, i)

    pp = p_prompt.reshape(depth, mp, -1)
    ps = p_sample.reshape(depth, db, -1)

    xp = x_prompt.reshape(mp, d)
    xs = x_sample.reshape(db, d)
    w_in0 = nsa_w_in[0]
    hp = matmul(xp, w_in0, tn=512)
    hs = matmul(xs, w_in0, tn=512)

    att_p = nsa_prompt_attention(hp, nsa_cmp_a[0], nsa_cmp_phi[0], bias_c, bias_near,
                                 batch=batch, seq=seq)

    hp3 = hp.reshape(batch, seq, -1)
    nsa_kv_prompt = hp3[:, :, q_cols:q_cols + 4 * HD].reshape(1, batch, seq, 4, HD)
    win_rows_p = hp3[:, :, q_cols + 4 * HD:q_cols + kv_cols].reshape(batch, seq, 2, HD)
    if seq >= w_buf:
        nsa_win_prompt = win_rows_p[:, seq - w_buf:][None]
    else:
        nsa_win_prompt = jnp.pad(win_rows_p, ((0, 0), (w_buf - seq, 0), (0, 0), (0, 0)))[None]

    qs = hs[:, :q_cols].reshape(db, HEADS, HD)
    kvn = hs[:, q_cols:q_cols + kv_cols].reshape(db, 1, kv_cols)
    cache_rows = cache_nsa_kv.reshape(cache_nsa_kv.shape[0] * n_pool, page_size * NSA_CACHE_SLOTS, HD)
    o_cmp_s, imp_s = nsa_sample_compressed(qs, page_table, cache_rows, nsa_cmp_a[0], nsa_cmp_phi[0],
                                           bias_cs, past_len=past_len)
    idx = nsa_sample_select(imp_s.reshape(db, -1), npb=npb)
    win_state = state_nsa_win[0].reshape(db, w_buf * 2, HD)
    o_slc_s, o_win_s = nsa_sample_slc_win(qs, kvn, win_state, idx, page_table, cache_rows,
                                          bias_slc, bias_0, bias_w, npb=npb)
    gate_logits = hs[:, q_cols + kv_cols:].reshape(db, 3, HEADS).transpose(1, 0, 2).reshape(3, db * HEADS, 1)
    att_s = nsa_gate_combine(gate_logits, o_cmp_s.reshape(db * HEADS, HD),
                             o_slc_s.reshape(db * HEADS, HD), o_win_s.reshape(db * HEADS, HD))
    att_s = att_s.reshape(db, q_cols)

    nsa_kv_sample = hs[:, q_cols:q_cols + 4 * HD].reshape(1, db, 1, 4, HD)
    new_win = hs[:, q_cols + 4 * HD:q_cols + kv_cols].reshape(db, 1, 2, HD)
    nsa_win_sample = jnp.concatenate([state_nsa_win[0], new_win], axis=1)[:, -w_buf:][None]

    xp = dense_tail(xp, att_p, nsa_w_o, pp, 0)
    xs = dense_tail(xs, att_s, nsa_w_o, ps, 0)

    wqb = mla_w_q_b[0].reshape(Q_LORA, HEADS, HD + QK_ROPE)
    wqb = jnp.concatenate([wqb[:, :, :HD].reshape(Q_LORA, HEADS * HD),
                           wqb[:, :, HD:].reshape(Q_LORA, HEADS * QK_ROPE)], axis=1)
    w_kv_b = mla_w_kv_b[0]
    cos_p, sin_p = _rope_tables(jnp.arange(seq))
    cos_s, sin_s = _rope_tables(jnp.full((db,), past_len))

    tmp = 512
    hp = matmul(xp, mla_w_in[0], tn=512)
    cq_p, ckv_p, rows_p, kr2_p = mla_prep(hp, mla_q_norm[0], mla_kv_norm[0], cos_p, sin_p,
                                          pos_blocks=seq // tmp, tm=tmp)
    q_p = matmul(cq_p, wqb, tn=512)
    qr_p = rope_q(q_p, cos_p, sin_p, pos_blocks=seq // tmp, tm=tmp)
    kvx_p = matmul(ckv_p, w_kv_b, tn=512, out_dtype=MXU_DT)
    att_p = mla_prompt_attention(q_p, qr_p, kvx_p, kr2_p, batch=batch, seq=seq)

    hs = matmul(xs, mla_w_in[0], tn=512)
    cq_s, _, rows_s, _ = mla_prep(hs, mla_q_norm[0], mla_kv_norm[0], cos_s, sin_s, pos_blocks=1, tm=db)
    q_s = matmul(cq_s, wqb, tn=512)
    qr_s = rope_q(q_s, cos_s, sin_s, pos_blocks=1, tm=db)
    q_lat = mla_absorb_q(q_s, w_kv_b).transpose(1, 0, 2)
    cache_mla = jnp.swapaxes(cache_mla_kv, 2, 3).reshape(cache_mla_kv.shape[0] * n_pool, MLA_ROW, page_size)
    o_lat = mla_decode(q_lat, qr_s.reshape(db, HEADS, QK_ROPE), rows_s.reshape(db, 1, MLA_ROW),
                       page_table, cache_mla)
    att_s = mla_absorb_o(o_lat.transpose(1, 0, 2), w_kv_b)

    xp = dense_tail(xp, att_p, mla_w_o, pp, 1)
    xs = dense_tail(xs, att_s, mla_w_o, ps, 1)

    return (xp.reshape(batch, seq, d), xs.reshape(db, 1, d),
            nsa_kv_prompt, nsa_win_prompt, rows_p.reshape(1, batch, seq, MLA_ROW),
            nsa_kv_sample, nsa_win_sample, rows_s.reshape(1, db, 1, MLA_ROW))
```

```python
import functools
import math

import numpy as np
import jax
import jax.numpy as jnp
from jax import lax
from jax.experimental import pallas as pl
from jax.experimental.pallas import tpu as pltpu

F32 = jnp.float32
MXU_DT = jnp.bfloat16
HI = lax.Precision.HIGHEST

HEADS = 16
HD = 128
CMP_STRIDE = 16
CMP_BLOCK = 32
SLC_BLOCK = 64
N_SEL = 16
WINDOW = 512
NSA_KV_SLOTS = 6
NSA_CACHE_SLOTS = 4
NSA_SCALE = HD ** -0.5
Q_LORA = 768
KV_LORA = 512
QK_ROPE = 64
MLA_ROW = KV_LORA + QK_ROPE
MLA_SCALE = (HD + QK_ROPE) ** -0.5
ROPE_THETA = 10000.0
T5_BUCKETS = 32
T5_MAX_DIST = 128
LN_EPS = 1e-5
RMS_EPS = 1e-6
NEG_INF = -1e30
FORCE_SCORE = 1e4
QB = 128
LANES = 128
HEAD_GROUP = 8
FAR_CHUNK = 512
MLA_QBLK = 512
RING = 3

VMEM_LIMIT = 56 * 1024 * 1024


def _cparams(n_axes):
    return pltpu.CompilerParams(dimension_semantics=("arbitrary",) * n_axes,
                                vmem_limit_bytes=VMEM_LIMIT)


def _dot(a, b):
    return jnp.dot(a, b, preferred_element_type=F32)


def _dot_nt(a, b):
    return lax.dot_general(a, b, (((1,), (1,)), ((), ())), preferred_element_type=F32)


def _dot_hi(a, b):
    return jnp.dot(a, b, precision=HI, preferred_element_type=F32)


def _sigmoid(x):
    return 1.0 / (1.0 + jnp.exp(-x))


def _mm_kernel(x_ref, w_ref, o_ref, xb_ref):
    @pl.when(pl.program_id(1) == 0)
    def _():
        xb_ref[...] = x_ref[...].astype(MXU_DT)
    o_ref[...] = _dot(xb_ref[...], w_ref[...].astype(MXU_DT)).astype(o_ref.dtype)


def matmul(x, w, *, tn, out_dtype=F32, tm=1024):
    m, k = x.shape
    n = w.shape[1]
    tm = min(tm, m)
    return pl.pallas_call(
        _mm_kernel,
        grid=(pl.cdiv(m, tm), pl.cdiv(n, tn)),
        in_specs=[pl.BlockSpec((tm, k), lambda i, j: (i, 0)),
                  pl.BlockSpec((k, tn), lambda i, j: (0, j))],
        out_specs=pl.BlockSpec((tm, tn), lambda i, j: (i, j)),
        out_shape=jax.ShapeDtypeStruct((m, n), out_dtype),
        scratch_shapes=[pltpu.VMEM((tm, k), MXU_DT)],
        compiler_params=_cparams(2),
        name="matmul",
    )(x, w)


def _layer_norm(y, g, b):
    mu = jnp.mean(y, axis=-1, keepdims=True)
    yc = y - mu
    var = jnp.mean(yc * yc, axis=-1, keepdims=True)
    return yc * lax.rsqrt(var + LN_EPS) * g + b


def _mm_res_kernel(x_ref, w_ref, r_ref, o_ref, *, alpha):
    o_ref[...] = alpha * r_ref[...] + _dot(x_ref[...].astype(MXU_DT), w_ref[...].astype(MXU_DT))


def matmul_residual(x, w, layer, resid, *, alpha, tn=512, tm=1024):
    m, k = x.shape
    n = w.shape[2]
    tm = min(tm, m)
    return pl.pallas_call(
        functools.partial(_mm_res_kernel, alpha=alpha),
        grid=(pl.cdiv(m, tm), pl.cdiv(n, tn)),
        in_specs=[pl.BlockSpec((tm, k), lambda i, j: (i, 0)),
                  pl.BlockSpec((None, k, tn), lambda i, j: (layer, 0, j)),
                  pl.BlockSpec((tm, tn), lambda i, j: (i, j))],
        out_specs=pl.BlockSpec((tm, tn), lambda i, j: (i, j)),
        out_shape=jax.ShapeDtypeStruct((m, n), F32),
        compiler_params=_cparams(2),
        name="matmul_residual",
    )(x, w, resid)


def _swiglu_ln_kernel(y_ref, g_ref, b_ref, w1_ref, w2_ref, o_ref, xn_ref, xb_ref):
    @pl.when(pl.program_id(1) == 0)
    def _():
        xn = _layer_norm(y_ref[...], g_ref[...], b_ref[...])
        xn_ref[...] = xn
        xb_ref[...] = xn.astype(MXU_DT)
    xb = xb_ref[...]
    h1 = _dot(xb, w1_ref[...].astype(MXU_DT))
    h2 = _dot(xb, w2_ref[...].astype(MXU_DT))
    o_ref[...] = (h1 * _sigmoid(h1) * h2).astype(o_ref.dtype)


def swiglu_ln(y, g, b, w_in, layer, *, tn=256, tm=1024):
    m, k = y.shape
    d_ff = w_in.shape[2] // 2
    assert d_ff % tn == 0
    nff = d_ff // tn
    tm = min(tm, m)
    return pl.pallas_call(
        _swiglu_ln_kernel,
        grid=(pl.cdiv(m, tm), nff),
        in_specs=[pl.BlockSpec((tm, k), lambda i, j: (i, 0), pipeline_mode=pl.Buffered(1)),
                  pl.BlockSpec((1, k), lambda i, j: (0, 0)),
                  pl.BlockSpec((1, k), lambda i, j: (0, 0)),
                  pl.BlockSpec((None, k, tn), lambda i, j: (layer, 0, j)),
                  pl.BlockSpec((None, k, tn), lambda i, j: (layer, 0, j + nff))],
        out_specs=[pl.BlockSpec((tm, tn), lambda i, j: (i, j)),
                   pl.BlockSpec((tm, k), lambda i, j: (i, 0))],
        out_shape=[jax.ShapeDtypeStruct((m, d_ff), MXU_DT), jax.ShapeDtypeStruct((m, k), F32)],
        scratch_shapes=[pltpu.VMEM((tm, k), MXU_DT)],
        compiler_params=_cparams(2),
        name="swiglu_ln",
    )(y, g.reshape(1, k), b.reshape(1, k), w_in, w_in)


def _ple_ln_kernel(y_ref, g_ref, b_ref, wg_ref, p_ref, wp_ref, o_ref, xn_ref, xb_ref, *, tn):
    j = pl.program_id(1)

    @pl.when(j == 0)
    def _():
        xn = _layer_norm(y_ref[...], g_ref[...], b_ref[...])
        xb_ref[...] = xn.astype(MXU_DT)
        for t in range(xn_ref.shape[0]):
            xn_ref[t] = xn[:, t * tn:(t + 1) * tn]

    gate = _sigmoid(_dot(xb_ref[...], wg_ref[...].astype(MXU_DT)))
    proj = _dot(p_ref[...].astype(MXU_DT), wp_ref[...].astype(MXU_DT))
    o_ref[...] = xn_ref[j] + gate * proj


def ple_ln(y, g, b, w_gate, p, w_proj, layer, *, tn=512, tm=1024):
    m, k = y.shape
    n = w_gate.shape[2]
    pe = p.shape[2]
    tm = min(tm, m)
    assert n == k and n % tn == 0
    return pl.pallas_call(
        functools.partial(_ple_ln_kernel, tn=tn),
        grid=(pl.cdiv(m, tm), n // tn),
        in_specs=[pl.BlockSpec((tm, k), lambda i, j: (i, 0), pipeline_mode=pl.Buffered(1)),
                  pl.BlockSpec((1, k), lambda i, j: (0, 0)),
                  pl.BlockSpec((1, k), lambda i, j: (0, 0)),
                  pl.BlockSpec((None, k, tn), lambda i, j: (layer, 0, j)),
                  pl.BlockSpec((None, tm, pe), lambda i, j: (layer, i, 0)),
                  pl.BlockSpec((None, pe, tn), lambda i, j: (layer, 0, j))],
        out_specs=pl.BlockSpec((tm, tn), lambda i, j: (i, j)),
        out_shape=jax.ShapeDtypeStruct((m, n), F32),
        scratch_shapes=[pltpu.VMEM((n // tn, tm, tn), F32), pltpu.VMEM((tm, k), MXU_DT)],
        compiler_params=_cparams(2),
        name="ple_ln",
    )(y, g.reshape(1, k), b.reshape(1, k), w_gate, p, w_proj)


def _t5_bucket_np(dist):
    n = np.maximum(np.asarray(dist, np.int64), 0)
    max_exact = T5_BUCKETS // 2
    nf = np.maximum(n, 1).astype(np.float32)
    scaled = (np.log(nf / np.float32(max_exact)) / np.float32(math.log(T5_MAX_DIST / max_exact))
              * np.float32(T5_BUCKETS - max_exact))
    large = np.minimum(max_exact + scaled.astype(np.int32), T5_BUCKETS - 1)
    return np.where(n < max_exact, n, large).astype(np.int32)


def _bias_kernel(tab_ref, ids_ref, o_ref, *, n_rel):
    ids = ids_ref[...]
    relative = pl.program_id(0) < n_rel
    for h in range(HEADS):
        acc = jnp.zeros(ids.shape, F32)
        for bkt in range(T5_BUCKETS):
            acc = jnp.where(ids == bkt, tab_ref[bkt, h], acc)
        o_ref[h] = acc - jnp.where(relative, tab_ref[T5_BUCKETS - 1, h], 0.0)


def t5_bias_lookup(table, ids, *, n_rel, tr=256):
    r = ids.shape[0]
    assert r % tr == 0
    return pl.pallas_call(
        functools.partial(_bias_kernel, n_rel=n_rel),
        grid=(r // tr,),
        in_specs=[pl.BlockSpec(memory_space=pltpu.SMEM),
                  pl.BlockSpec((tr, LANES), lambda i: (i, 0))],
        out_specs=pl.BlockSpec((HEADS, tr, LANES), lambda i: (0, i, 0)),
        out_shape=jax.ShapeDtypeStruct((HEADS, r, LANES), F32),
        compiler_params=_cparams(1),
        name="t5_bias_lookup",
    )(table, ids)


def _softmax_rows(s):
    m = jnp.max(s, axis=-1, keepdims=True)
    e = jnp.exp(s - m)
    l = jnp.sum(e, axis=-1, keepdims=True)
    return jnp.where(m > 0.5 * NEG_INF, e / l, 0.0)


def _lane_tiles(x):
    return [x[..., t * LANES:(t + 1) * LANES] for t in range(x.shape[-1] // LANES)]


def _fold(op, tiles):
    out = tiles[0]
    for t in tiles[1:]:
        out = op(out, t)
    return out


def _rope_lanes(x, cos_t, sin_t):
    lane = lax.broadcasted_iota(jnp.int32, x.shape, 1)
    first_half = (lane % QK_ROPE) < (QK_ROPE // 2)
    rot = jnp.where(first_half, pltpu.roll(x, LANES - QK_ROPE // 2, 1), pltpu.roll(x, QK_ROPE // 2, 1))
    return x * cos_t + rot * sin_t


def _nsa_prompt_kernel(q_ref, g_ref, kcr_ref, vcr_ref, ksr_ref, vsr_ref, kwr_ref, vwr_ref,
                       a_ref, phi_ref, bc_ref, bn_ref, amat_ref, emat_ref, o_ref,
                       kc_scr, vc_scr, ks_scr, vs_scr, kw_scr, vw_scr, q_scr, selk_scr, a_scr,
                       mx_scr, mb_scr, l_scr, acc_scr, sn_scr, sf_scr, oc_scr, os_scr, ow_scr,
                       *, nt, n_blk):
    i = pl.program_id(1)
    seq = nt * QB
    n_cmp = nt * (QB // CMP_STRIDE) - 1
    wt = WINDOW // QB
    hg = HEAD_GROUP
    gm = hg * QB
    fc = FAR_CHUNK
    tpc = fc // QB

    @pl.when(i == 0)
    def _():
        ks_scr[0:QB, :] = jnp.zeros((QB, HD), MXU_DT)
        vs_scr[0:QB, :] = jnp.zeros((QB, HD), MXU_DT)
        kw_scr[0:wt * QB, :] = jnp.zeros((wt * QB, HD), MXU_DT)
        vw_scr[0:wt * QB, :] = jnp.zeros((wt * QB, HD), MXU_DT)
        ks_scr[QB:QB + seq, :] = ksr_ref[...].astype(MXU_DT)
        vs_scr[QB:QB + seq, :] = vsr_ref[...].astype(MXU_DT)
        kw_scr[wt * QB:wt * QB + seq, :] = kwr_ref[...].astype(MXU_DT)
        vw_scr[wt * QB:wt * QB + seq, :] = vwr_ref[...].astype(MXU_DT)
        for which, (rows_ref, dst) in enumerate(((kcr_ref, kc_scr), (vcr_ref, vc_scr))):
            ng = nt * QB // CMP_STRIDE
            first = jnp.zeros((ng, HD), F32)
            second = jnp.zeros((ng, HD), F32)
            for j in range(CMP_STRIDE):
                xj = rows_ref[pl.ds(j, ng, stride=CMP_STRIDE), :]
                first = first + xj * a_ref[which, j:j + 1, :]
                second = second + xj * a_ref[which, CMP_STRIDE + j:CMP_STRIDE + j + 1, :]
            pre = first + pltpu.roll(second, ng - 1, 0)
            dst[...] = _dot_hi(pre, phi_ref[which]).astype(MXU_DT)

    for h in range(HEADS):
        q_scr[h * QB:(h + 1) * QB, :] = (q_ref[:, h * HD:(h + 1) * HD] * NSA_SCALE).astype(MXU_DT)

    row = lax.broadcasted_iota(jnp.int32, (QB, QB), 0)
    lane = lax.broadcasted_iota(jnp.int32, (QB, QB), 1)
    qpos = i * QB + row
    tri = jnp.where(lane <= row, 0.0, NEG_INF)
    cmp_valid = (qpos >= lane * CMP_STRIDE + (CMP_BLOCK - 1)) & (lane < n_cmp)
    cmp_mask = jnp.where(cmp_valid, 0.0, NEG_INF)

    s = _dot_nt(q_scr[...], kc_scr[...]).reshape(HEADS, QB, QB) + bc_ref[0] + cmp_mask[None]
    p = _softmax_rows(s)
    imp = jnp.sum(p, axis=0)
    oc_scr[...] = _dot(p.reshape(HEADS * QB, QB).astype(MXU_DT), vc_scr[...])

    slc_imp = _dot_hi(imp, amat_ref[...])
    cur = qpos // SLC_BLOCK
    forced = (lane == 0) | (lane == cur) | (lane == cur - 1)
    score = jnp.where(lane > cur, -1.0, jnp.where(forced, FORCE_SCORE, slc_imp))
    score = jnp.where(lane < n_blk, score, -2.0)
    rank = jnp.zeros((QB, QB), F32)
    for b2 in range(n_blk):
        col = score[:, b2:b2 + 1]
        beats = (col > score) | ((col == score) & (lane > b2))
        rank = rank + jnp.where(beats, 1.0, 0.0)
    sel = jnp.where((rank < min(N_SEL, n_blk)) & (lane < n_blk), 1.0, 0.0)
    sel_keys = _dot(sel.astype(MXU_DT), emat_ref[...])
    for kt in range(nt):
        tile = jnp.where(sel_keys[:, kt * QB:(kt + 1) * QB] > 0.5, 0.0, NEG_INF)
        selk_scr[kt] = tile
        a_scr[kt // tpc, :, (kt % tpc) * QB:(kt % tpc + 1) * QB] = jnp.where(kt < i - 1, tile, NEG_INF)
    prev_ok = jnp.where(i > 0, 0.0, NEG_INF)
    near_mask = jnp.concatenate([selk_scr[jnp.maximum(i - 1, 0)] + prev_ok, selk_scr[i] + tri], axis=1)

    zeros = jnp.zeros((QB, QB), F32)
    band = jnp.where(lane >= row, 0.0, NEG_INF)

    def tile_ok(t):
        return jnp.where(i - wt + t >= 0, 0.0, NEG_INF)

    win_far_mask = jnp.concatenate(
        [band + tile_ok(0)] + [zeros + tile_ok(t) for t in range(1, wt - 1)], axis=1)
    win_near_mask = jnp.concatenate([zeros + tile_ok(wt - 1), tri], axis=1)
    n_far = (i + tpc - 2) // tpc
    n0 = pl.multiple_of(i * QB, QB)

    def group(g, carry):
        r0 = pl.multiple_of(g * gm, gm)
        qg = q_scr[pl.ds(r0, gm), :]
        bn = bn_ref[pl.ds(g * hg, hg)]

        mx_scr[...] = jnp.full((hg, QB, QB), NEG_INF, F32)

        def far_max(c, carry2):
            k0 = pl.multiple_of(QB + c * fc, QB)
            s = _dot_nt(qg, ks_scr[pl.ds(k0, fc), :]).reshape(hg, QB, fc) + a_scr[c][None]
            sf_scr[c] = s
            mx_scr[...] = jnp.maximum(mx_scr[...], _fold(jnp.maximum, _lane_tiles(s)))
            return carry2

        lax.fori_loop(0, n_far, far_max, 0)
        sn = _dot_nt(qg, ks_scr[pl.ds(n0, 2 * QB), :]).reshape(hg, QB, 2 * QB) + bn + near_mask[None]
        sn_scr[...] = sn
        mfold = jnp.maximum(mx_scr[...], _fold(jnp.maximum, _lane_tiles(sn)))
        mb_scr[...] = jnp.broadcast_to(jnp.max(mfold, axis=-1, keepdims=True), (hg, QB, QB))
        l_scr[...] = jnp.zeros((hg, QB, QB), F32)
        acc_scr[...] = jnp.zeros((gm, HD), F32)

        def far_pv(c, carry2):
            k0 = pl.multiple_of(QB + c * fc, QB)
            mbv = mb_scr[...]
            ps = [jnp.exp(t - mbv) for t in _lane_tiles(sf_scr[c])]
            l_scr[...] += _fold(jnp.add, ps)
            pm = jnp.concatenate(ps, axis=-1).reshape(gm, fc).astype(MXU_DT)
            acc_scr[...] += _dot(pm, vs_scr[pl.ds(k0, fc), :])
            return carry2

        lax.fori_loop(0, n_far, far_pv, 0)
        mbv = mb_scr[...]
        pn = [jnp.exp(t - mbv) for t in _lane_tiles(sn_scr[...])]
        l = l_scr[...] + _fold(jnp.add, pn)
        pm = jnp.concatenate(pn, axis=-1).reshape(gm, 2 * QB).astype(MXU_DT)
        acc = acc_scr[...] + _dot(pm, vs_scr[pl.ds(n0, 2 * QB), :])
        o_slc = acc.reshape(hg, QB, HD) / jnp.sum(l, axis=-1, keepdims=True)
        os_scr[pl.ds(r0, gm), :] = o_slc.reshape(gm, HD)

        nw = (wt + 1) * QB
        nf = (wt - 1) * QB
        sw = _dot_nt(qg, kw_scr[pl.ds(n0, nw), :]).reshape(hg, QB, nw)
        s_far = sw[:, :, :nf] + win_far_mask[None]
        s_near = sw[:, :, nf:] + bn + win_near_mask[None]
        m = jnp.maximum(jnp.max(s_far, axis=-1, keepdims=True), jnp.max(s_near, axis=-1, keepdims=True))
        e_far = jnp.exp(s_far - m)
        e_near = jnp.exp(s_near - m)
        l = jnp.sum(e_far, axis=-1, keepdims=True) + jnp.sum(e_near, axis=-1, keepdims=True)
        pm = jnp.concatenate([e_far, e_near], axis=-1).reshape(gm, nw).astype(MXU_DT)
        o_win = _dot(pm, vw_scr[pl.ds(n0, nw), :]).reshape(hg, QB, HD) / l
        ow_scr[pl.ds(r0, gm), :] = o_win.reshape(gm, HD)
        return carry

    lax.fori_loop(0, HEADS // hg, group, 0)

    gates = _sigmoid(g_ref[...])
    for h in range(HEADS):
        rows = slice(h * QB, (h + 1) * QB)
        o = (gates[:, h:h + 1] * oc_scr[rows, :] + gates[:, HEADS + h:HEADS + h + 1] * os_scr[rows, :]
             + gates[:, 2 * HEADS + h:2 * HEADS + h + 1] * ow_scr[rows, :])
        o_ref[:, h * HD:(h + 1) * HD] = o.astype(o_ref.dtype)


def nsa_prompt_attention(h, cmp_a, cmp_phi, bias_c, bias_near, *, batch, seq):
    nt = seq // QB
    n_blk = seq // SLC_BLOCK
    wt = WINDOW // QB
    assert seq // CMP_STRIDE == QB and n_blk <= QB and WINDOW % QB == 0 and wt >= 2
    assert FAR_CHUNK % QB == 0 and seq % FAR_CHUNK == 0 and HEADS % HEAD_GROUP == 0
    q_cols = HEADS * HD
    kv_blk0 = q_cols // HD
    gate_blk = kv_blk0 + NSA_KV_SLOTS
    c = np.arange(QB)[:, None]
    b = np.arange(QB)[None, :]
    ratio = SLC_BLOCK // CMP_STRIDE
    amat = (((c >= ratio * b) & (c <= ratio * b + ratio - 1)).astype(np.float32)
            + ((c + 1 >= ratio * b) & (c + 1 <= ratio * b + ratio - 1)).astype(np.float32))
    amat = amat * (b < n_blk)
    emat = (np.arange(seq)[None, :] // SLC_BLOCK == np.arange(QB)[:, None]).astype(np.float32)

    kv_spec = lambda s: pl.BlockSpec((seq, HD), lambda bb, i, s=s: (bb, kv_blk0 + s))
    full = lambda shape: pl.BlockSpec(shape, lambda bb, i: (0,) * len(shape))
    keys_scr = lambda pad_tiles: pltpu.VMEM(((nt + pad_tiles) * QB, HD), MXU_DT)
    heads_scr = lambda dt: pltpu.VMEM((HEADS * QB, HD), dt)
    group_scr = lambda: pltpu.VMEM((HEAD_GROUP, QB, QB), F32)
    return pl.pallas_call(
        functools.partial(_nsa_prompt_kernel, nt=nt, n_blk=n_blk),
        grid=(batch, nt),
        in_specs=[pl.BlockSpec((QB, q_cols), lambda bb, i: (bb * nt + i, 0)),
                  pl.BlockSpec((QB, LANES), lambda bb, i: (bb * nt + i, gate_blk))]
                 + [kv_spec(s) for s in range(NSA_KV_SLOTS)]
                 + [full((2, CMP_BLOCK, HD)), full((2, HD, HD)),
                    pl.BlockSpec((1, HEADS, QB, QB), lambda bb, i: (i, 0, 0, 0)),
                    full((HEADS, QB, 2 * QB)), full((QB, QB)), full((QB, seq))],
        out_specs=pl.BlockSpec((QB, q_cols), lambda bb, i: (bb * nt + i, 0)),
        out_shape=jax.ShapeDtypeStruct((batch * seq, q_cols), MXU_DT),
        scratch_shapes=[pltpu.VMEM((QB, HD), MXU_DT), pltpu.VMEM((QB, HD), MXU_DT),
                        keys_scr(1), keys_scr(1), keys_scr(wt), keys_scr(wt),
                        heads_scr(MXU_DT), pltpu.VMEM((nt, QB, QB), F32),
                        pltpu.VMEM((seq // FAR_CHUNK, QB, FAR_CHUNK), F32),
                        group_scr(), group_scr(), group_scr(),
                        pltpu.VMEM((HEAD_GROUP * QB, HD), F32),
                        pltpu.VMEM((HEAD_GROUP, QB, 2 * QB), F32),
                        pltpu.VMEM((seq // FAR_CHUNK, HEAD_GROUP, QB, FAR_CHUNK), F32),
                        heads_scr(F32), heads_scr(F32), heads_scr(F32)],
        compiler_params=_cparams(2),
        name="nsa_prompt_attention",
    )(h, h, h, h, h, h, h, h, cmp_a, cmp_phi, bias_c, bias_near,
      jnp.asarray(amat), jnp.asarray(emat, MXU_DT))


def _nsa_sample_cmp_kernel(pt_ref, q_ref, w_ref, phi_ref, bias_ref, cache_ref, oc_ref, imp_ref,
                           buf, sem, part_scr, kc_scr, vc_scr, *, n_chunks, pages, past_len):
    b = pl.program_id(0)
    nb = pl.num_programs(0)
    prow = LANES * NSA_CACHE_SLOTS
    rows = pages * prow
    grow = CMP_STRIDE * NSA_CACHE_SLOTS
    groups = pages * LANES // CMP_STRIDE
    sub = 8
    nv = grow // sub

    def chunk_copies(bb, ch, slot):
        cps = []
        for p in range(pages):
            page = pt_ref[bb, ch * pages + p]
            cps.append(pltpu.make_async_copy(cache_ref.at[page],
                                             buf.at[slot, pl.ds(p * prow, prow), :], sem.at[slot]))
        if ch + 1 < n_chunks:
            page = pt_ref[bb, (ch + 1) * pages]
            cps.append(pltpu.make_async_copy(cache_ref.at[page, pl.ds(0, grow), :],
                                             buf.at[slot, pl.ds(rows, grow), :], sem.at[slot]))
        return cps

    ahead = RING - 1

    @pl.when(b == 0)
    def _():
        for g0 in range(ahead):
            for cp in chunk_copies(0, g0, g0):
                cp.start()

    for ch in range(n_chunks):
        g = b * n_chunks + ch
        slot = g % RING
        nxt = (g + ahead) % RING
        if ch + ahead < n_chunks:
            for cp in chunk_copies(b, ch + ahead, nxt):
                cp.start()
        else:
            @pl.when(b + 1 < nb)
            def _():
                for cp in chunk_copies(b + 1, ch + ahead - n_chunks, nxt):
                    cp.start()
        for cp in chunk_copies(b, ch, slot):
            cp.wait()
        if ch == n_chunks - 1:
            buf[slot, rows:rows + grow, :] = jnp.zeros((grow, HD), F32)
        x0 = buf[slot, 0:rows, :].reshape(groups, nv, sub, HD)
        x1 = buf[slot, grow:rows + grow, :].reshape(groups, nv, sub, HD)
        part = x0[:, 0] * w_ref[0, 0] + x1[:, 0] * w_ref[1, 0]
        for t in range(1, nv):
            part = part + x0[:, t] * w_ref[0, t] + x1[:, t] * w_ref[1, t]
        part_scr[...] = part.reshape(groups * sub, HD)
        for kv, dst in enumerate((kc_scr, vc_scr)):
            pre = (part_scr[pl.ds(kv, groups, stride=sub), :]
                   + part_scr[pl.ds(NSA_CACHE_SLOTS + kv, groups, stride=sub), :])
            dst[ch * groups:(ch + 1) * groups, :] = _dot_hi(pre, phi_ref[kv]).astype(MXU_DT)

    n_tok = n_chunks * groups
    q = (q_ref[0] * NSA_SCALE).astype(MXU_DT)
    tok = lax.broadcasted_iota(jnp.int32, (HEADS, n_tok), 1)
    valid = tok * CMP_STRIDE + (CMP_BLOCK - 1) <= past_len
    s = _dot_nt(q, kc_scr[...]) + bias_ref[...] + jnp.where(valid, 0.0, NEG_INF)
    p = _softmax_rows(s)
    oc_ref[0] = _dot(p.astype(MXU_DT), vc_scr[...])
    imp_ref[0] = jnp.sum(p, axis=0, keepdims=True)


def nsa_sample_compressed(q, page_table, cache_rows, cmp_a, cmp_phi, bias_c, *, past_len, pages=16):
    db, n_pages = page_table.shape
    assert n_pages % pages == 0
    n_chunks = n_pages // pages
    assert n_chunks >= RING - 1
    rows = (pages * LANES + CMP_STRIDE) * NSA_CACHE_SLOTS
    n_tok = n_pages * LANES // CMP_STRIDE
    nv = CMP_STRIDE // 2
    w = cmp_a.reshape(2, 2, nv, 2, HD).transpose(1, 2, 3, 0, 4)
    w = jnp.pad(w, ((0, 0), (0, 0), (0, 0), (0, NSA_CACHE_SLOTS - 2), (0, 0)))
    w = w.reshape(2, nv, 2 * NSA_CACHE_SLOTS, HD)
    grid_spec = pltpu.PrefetchScalarGridSpec(
        num_scalar_prefetch=1,
        grid=(db,),
        in_specs=[pl.BlockSpec((1, HEADS, HD), lambda b, pt: (b, 0, 0)),
                  pl.BlockSpec((2, nv, 2 * NSA_CACHE_SLOTS, HD), lambda b, pt: (0, 0, 0, 0)),
                  pl.BlockSpec((2, HD, HD), lambda b, pt: (0, 0, 0)),
                  pl.BlockSpec((HEADS, n_tok), lambda b, pt: (0, 0)),
                  pl.BlockSpec(memory_space=pl.ANY)],
        out_specs=[pl.BlockSpec((1, HEADS, HD), lambda b, pt: (b, 0, 0)),
                   pl.BlockSpec((1, 1, n_tok), lambda b, pt: (b, 0, 0))],
        scratch_shapes=[pltpu.VMEM((RING, rows, HD), F32),
                        pltpu.SemaphoreType.DMA((RING,)),
                        pltpu.VMEM((pages * LANES // CMP_STRIDE * 8, HD), F32),
                        pltpu.VMEM((n_tok, HD), MXU_DT), pltpu.VMEM((n_tok, HD), MXU_DT)])
    return pl.pallas_call(
        functools.partial(_nsa_sample_cmp_kernel, n_chunks=n_chunks, pages=pages, past_len=past_len),
        grid_spec=grid_spec,
        out_shape=[jax.ShapeDtypeStruct((db, HEADS, HD), F32),
                   jax.ShapeDtypeStruct((db, 1, n_tok), F32)],
        compiler_params=_cparams(1),
        name="nsa_sample_compressed",
    )(page_table, q, w, cmp_phi, bias_c, cache_rows)


def _select_kernel(imp_ref, amat_ref, idx_ref, *, npb, n_pick):
    slc_imp = _dot_hi(imp_ref[...], amat_ref[...])
    lane = lax.broadcasted_iota(jnp.int32, slc_imp.shape, 1).astype(F32)
    forced = (lane == 0.0) | (lane == float(npb - 1))
    score = jnp.where(forced, FORCE_SCORE, slc_imp)
    score = jnp.where(lane < float(npb), score, -2.0)
    out_lane = lax.broadcasted_iota(jnp.int32, idx_ref.shape, 1)
    out = jnp.zeros(idx_ref.shape, F32)
    for t in range(n_pick):
        best = jnp.max(score, axis=-1, keepdims=True)
        first = jnp.min(jnp.where(score == best, lane, 1e9), axis=-1, keepdims=True)
        out = jnp.where(out_lane == t, first, out)
        score = jnp.where(lane == first, -3.0, score)
    idx_ref[...] = out.astype(jnp.int32)


def nsa_sample_select(imp, *, npb):
    db, n_tok = imp.shape
    assert npb >= N_SEL
    nbp = -(-npb // LANES) * LANES
    c = np.arange(n_tok)[:, None]
    b = np.arange(nbp)[None, :]
    ratio = SLC_BLOCK // CMP_STRIDE
    amat = (((c >= ratio * b) & (c <= ratio * b + ratio - 1)).astype(np.float32)
            + ((c + 1 >= ratio * b) & (c + 1 <= ratio * b + ratio - 1)).astype(np.float32))
    amat = amat * (b < npb)
    return pl.pallas_call(
        functools.partial(_select_kernel, npb=npb, n_pick=N_SEL - 1),
        out_shape=jax.ShapeDtypeStruct((db, LANES), jnp.int32),
        compiler_params=pltpu.CompilerParams(vmem_limit_bytes=VMEM_LIMIT),
        name="nsa_sample_select",
    )(imp, jnp.asarray(amat))


def _nsa_sample_slc_win_kernel(idx_ref, pt_ref, q_ref, kvn_ref, win_ref, bsl_ref, b0_ref, bw_ref,
                               cache_ref, osl_ref, owin_ref, kbuf, sem, *, n_pick, npb):
    b = pl.program_id(0)
    nb = pl.num_programs(0)
    bpp = LANES // SLC_BLOCK
    brow = SLC_BLOCK * NSA_CACHE_SLOTS

    def block_copies(bb, slot):
        cps = []
        for t in range(n_pick):
            blk = idx_ref[bb, t]
            page = pt_ref[bb, blk // bpp]
            off = pl.multiple_of((blk % bpp) * brow, brow)
            cps.append(pltpu.make_async_copy(cache_ref.at[page, pl.ds(off, brow), :],
                                             kbuf.at[slot, pl.ds(t * brow, brow), :], sem.at[slot]))
        return cps

    slot = b % 2

    @pl.when(b == 0)
    def _():
        for cp in block_copies(0, 0):
            cp.start()

    @pl.when(b + 1 < nb)
    def _():
        for cp in block_copies(b + 1, 1 - slot):
            cp.start()

    for cp in block_copies(b, slot):
        cp.wait()

    n_keys = (n_pick + 1) * SLC_BLOCK
    kvn = kvn_ref[0]
    rowi = lax.broadcasted_iota(jnp.int32, (SLC_BLOCK, HD), 0)

    def gathered(cache_slot):
        parts = [kbuf[slot, pl.ds(t * brow + cache_slot, SLC_BLOCK, stride=NSA_CACHE_SLOTS), :]
                 for t in range(n_pick)]
        new_row = kvn[:, cache_slot * HD:(cache_slot + 1) * HD]
        parts.append(jnp.where(rowi == 0, new_row, 0.0))
        return jnp.concatenate(parts, axis=0).astype(MXU_DT)

    q = (q_ref[0] * NSA_SCALE).astype(MXU_DT)
    keys = gathered(2)
    vals = gathered(3)
    lane = lax.broadcasted_iota(jnp.int32, (HEADS, LANES), 1)

    def slot_bias(t):
        if t == n_pick:
            return b0_ref[...]
        blk = idx_ref[b, t]
        return jnp.where(blk == npb - 1, bsl_ref[0], jnp.where(blk == npb - 2, bsl_ref[1], bsl_ref[2]))

    tiles = []
    for u in range(n_keys // LANES):
        tiles.append(jnp.where(lane < SLC_BLOCK, slot_bias(2 * u), slot_bias(2 * u + 1)))
    bias = jnp.concatenate(tiles, axis=1)
    col = lax.broadcasted_iota(jnp.int32, (HEADS, n_keys), 1)
    s = _dot_nt(q, keys) + bias + jnp.where(col <= n_pick * SLC_BLOCK, 0.0, NEG_INF)
    p = _softmax_rows(s)
    osl_ref[0] = _dot(p.astype(MXU_DT), vals)

    w_buf = win_ref.shape[1] // 2
    wk = win_ref[0, pl.ds(0, w_buf, stride=2), :].astype(MXU_DT)
    wv = win_ref[0, pl.ds(1, w_buf, stride=2), :].astype(MXU_DT)
    sw = _dot_nt(q, wk) + bw_ref[...]
    new_k = kvn[:, 4 * HD:5 * HD].astype(MXU_DT).astype(F32)
    new_v = kvn[:, 5 * HD:6 * HD].astype(MXU_DT).astype(F32)
    s_new = jnp.sum(q.astype(F32) * new_k, axis=-1, keepdims=True) + b0_ref[:, 0:1]
    m = jnp.maximum(jnp.max(sw, axis=-1, keepdims=True), s_new)
    pw = jnp.exp(sw - m)
    pn = jnp.exp(s_new - m)
    l = jnp.sum(pw, axis=-1, keepdims=True) + pn
    pn_r = pn.astype(MXU_DT).astype(F32)
    owin_ref[0] = (_dot(pw.astype(MXU_DT), wv) + pn_r * new_v) / l


def nsa_sample_slc_win(q, kv_new, win_state, idx, page_table, cache_rows, bias_slc, bias_0, bias_w,
                       *, npb):
    db = q.shape[0]
    n_pick = N_SEL - 1
    w_buf = win_state.shape[1] // 2
    assert w_buf <= WINDOW and ((n_pick + 1) * SLC_BLOCK) % LANES == 0
    n_keys = (n_pick + 1) * SLC_BLOCK
    grid_spec = pltpu.PrefetchScalarGridSpec(
        num_scalar_prefetch=2,
        grid=(db,),
        in_specs=[pl.BlockSpec((1, HEADS, HD), lambda b, ix, pt: (b, 0, 0)),
                  pl.BlockSpec((1, 1, NSA_KV_SLOTS * HD), lambda b, ix, pt: (b, 0, 0)),
                  pl.BlockSpec((1, 2 * w_buf, HD), lambda b, ix, pt: (b, 0, 0)),
                  pl.BlockSpec((3, HEADS, LANES), lambda b, ix, pt: (0, 0, 0)),
                  pl.BlockSpec((HEADS, LANES), lambda b, ix, pt: (0, 0)),
                  pl.BlockSpec((HEADS, w_buf), lambda b, ix, pt: (0, 0)),
                  pl.BlockSpec(memory_space=pl.ANY)],
        out_specs=[pl.BlockSpec((1, HEADS, HD), lambda b, ix, pt: (b, 0, 0)),
                   pl.BlockSpec((1, HEADS, HD), lambda b, ix, pt: (b, 0, 0))],
        scratch_shapes=[pltpu.VMEM((2, n_pick * SLC_BLOCK * NSA_CACHE_SLOTS, HD), F32),
                        pltpu.SemaphoreType.DMA((2,))])
    return pl.pallas_call(
        functools.partial(_nsa_sample_slc_win_kernel, n_pick=n_pick, npb=npb),
        grid_spec=grid_spec,
        out_shape=[jax.ShapeDtypeStruct((db, HEADS, HD), F32),
                   jax.ShapeDtypeStruct((db, HEADS, HD), F32)],
        compiler_params=_cparams(1),
        name="nsa_sample_slc_win",
    )(idx, page_table, q, kv_new, win_state, bias_slc, bias_0, bias_w, cache_rows)


def _gate_combine_kernel(g_ref, oc_ref, os_ref, ow_ref, o_ref):
    g = _sigmoid(g_ref[...])
    o_ref[...] = (g[0] * oc_ref[...] + g[1] * os_ref[...] + g[2] * ow_ref[...]).astype(o_ref.dtype)


def nsa_gate_combine(gate_logits, o_cmp, o_slc, o_win):
    r = o_cmp.shape[0]
    return pl.pallas_call(
        _gate_combine_kernel,
        out_shape=jax.ShapeDtypeStruct((r, HD), MXU_DT),
        compiler_params=pltpu.CompilerParams(vmem_limit_bytes=VMEM_LIMIT),
        name="nsa_gate_combine",
    )(gate_logits, o_cmp, o_slc, o_win)


def _mla_prep_kernel(h_ref, gq_ref, gkv_ref, cos_ref, sin_ref, dup_ref,
                     cq_ref, ckv_ref, rows_ref, kr_ref):
    h = h_ref[...]

    def rms(x, g):
        return x * lax.rsqrt(jnp.mean(x * x, axis=-1, keepdims=True) + RMS_EPS) * g

    cq_ref[...] = rms(h[:, :Q_LORA], gq_ref[...]).astype(cq_ref.dtype)
    ckv = rms(h[:, Q_LORA:Q_LORA + KV_LORA], gkv_ref[...])
    ckv_ref[...] = ckv.astype(ckv_ref.dtype)
    kr2 = _dot_hi(h[:, Q_LORA + KV_LORA:Q_LORA + KV_LORA + QK_ROPE], dup_ref[...])
    kr2 = _rope_lanes(kr2, cos_ref[...], sin_ref[...])
    kr_ref[...] = kr2
    rows_ref[:, :KV_LORA] = ckv
    rows_ref[:, KV_LORA:] = kr2[:, :QK_ROPE]


def mla_prep(h, g_q, g_kv, cos_t, sin_t, *, pos_blocks, tm=512):
    m = h.shape[0]
    tm = min(tm, m)
    dup = np.concatenate([np.eye(QK_ROPE, dtype=np.float32)] * 2, axis=1)
    return pl.pallas_call(
        _mla_prep_kernel,
        grid=(m // tm,),
        in_specs=[pl.BlockSpec((tm, h.shape[1]), lambda i: (i, 0)),
                  pl.BlockSpec((1, Q_LORA), lambda i: (0, 0)),
                  pl.BlockSpec((1, KV_LORA), lambda i: (0, 0)),
                  pl.BlockSpec((tm, LANES), lambda i: (i % pos_blocks, 0)),
                  pl.BlockSpec((tm, LANES), lambda i: (i % pos_blocks, 0)),
                  pl.BlockSpec((QK_ROPE, LANES), lambda i: (0, 0))],
        out_specs=[pl.BlockSpec((tm, Q_LORA), lambda i: (i, 0)),
                   pl.BlockSpec((tm, KV_LORA), lambda i: (i, 0)),
                   pl.BlockSpec((tm, MLA_ROW), lambda i: (i, 0)),
                   pl.BlockSpec((tm, LANES), lambda i: (i, 0))],
        out_shape=[jax.ShapeDtypeStruct((m, Q_LORA), MXU_DT),
                   jax.ShapeDtypeStruct((m, KV_LORA), MXU_DT),
                   jax.ShapeDtypeStruct((m, MLA_ROW), F32),
                   jax.ShapeDtypeStruct((m, LANES), F32)],
        compiler_params=_cparams(1),
        name="mla_prep",
    )(h, g_q.reshape(1, -1), g_kv.reshape(1, -1), cos_t, sin_t, jnp.asarray(dup))


def _rope_q_kernel(q_ref, cos_ref, sin_ref, o_ref):
    cos_t = cos_ref[...]
    sin_t = sin_ref[...]
    for g in range(q_ref.shape[1] // LANES):
        cols = slice(g * LANES, (g + 1) * LANES)
        o_ref[:, cols] = _rope_lanes(q_ref[:, cols], cos_t, sin_t)


def rope_q(q, cos_t, sin_t, *, pos_blocks, tm=512):
    m = q.shape[0]
    tm = min(tm, m)
    wr = HEADS * QK_ROPE
    assert (HEADS * HD) % wr == 0
    col_blk = HEADS * HD // wr
    return pl.pallas_call(
        _rope_q_kernel,
        grid=(m // tm,),
        in_specs=[pl.BlockSpec((tm, wr), lambda i: (i, col_blk)),
                  pl.BlockSpec((tm, LANES), lambda i: (i % pos_blocks, 0)),
                  pl.BlockSpec((tm, LANES), lambda i: (i % pos_blocks, 0))],
        out_specs=pl.BlockSpec((tm, wr), lambda i: (i, 0)),
        out_shape=jax.ShapeDtypeStruct((m, wr), F32),
        compiler_params=_cparams(1),
        name="rope_q",
    )(q, cos_t, sin_t)


def _mla_prompt_kernel(qn_ref, qr_ref, kv_ref, kr_ref, o_ref, qc_scr, kc_scr, v_scr, *, seq):
    h = pl.program_id(1)
    blk = MLA_QBLK
    nq = seq // blk
    lane = lax.broadcasted_iota(jnp.int32, (blk, LANES), 1)
    for qb in range(nq):
        rows = slice(qb * blk, (qb + 1) * blk)
        qr = jnp.where((lane // QK_ROPE) == (h % 2), qr_ref[rows, :], 0.0)
        qc_scr[rows, 0:HD] = (qn_ref[rows, :] * MLA_SCALE).astype(MXU_DT)
        qc_scr[rows, HD:2 * HD] = (qr * MLA_SCALE).astype(MXU_DT)
        kc_scr[rows, 0:HD] = kv_ref[rows, 0:HD].astype(MXU_DT)
        kc_scr[rows, HD:2 * HD] = kr_ref[rows, :].astype(MXU_DT)
        v_scr[rows, :] = kv_ref[rows, HD:2 * HD].astype(MXU_DT)

    r = lax.broadcasted_iota(jnp.int32, (blk, blk), 0)
    c = lax.broadcasted_iota(jnp.int32, (blk, blk), 1)
    tri = jnp.where(c <= r, 0.0, NEG_INF)
    for qb in range(nq):
        d0 = qb * blk
        q = qc_scr[d0:d0 + blk, :]
        s_d = _dot_nt(q, kc_scr[d0:d0 + blk, :]) + tri
        m = jnp.max(s_d, axis=-1, keepdims=True)
        if qb > 0:
            s_o = _dot_nt(q, kc_scr[0:d0, :])
            m = jnp.maximum(m, jnp.max(s_o, axis=-1, keepdims=True))
        p_d = jnp.exp(s_d - m)
        l = jnp.sum(p_d, axis=-1, keepdims=True)
        o = _dot(p_d.astype(MXU_DT), v_scr[d0:d0 + blk, :])
        if qb > 0:
            p_o = jnp.exp(s_o - m)
            l = l + jnp.sum(p_o, axis=-1, keepdims=True)
            o = o + _dot(p_o.astype(MXU_DT), v_scr[0:d0, :])
        o_ref[d0:d0 + blk, :] = (o / l).astype(o_ref.dtype)


def mla_prompt_attention(q, qr, kvx, kr2, *, batch, seq):
    assert seq % MLA_QBLK == 0
    return pl.pallas_call(
        functools.partial(_mla_prompt_kernel, seq=seq),
        grid=(batch, HEADS),
        in_specs=[pl.BlockSpec((seq, HD), lambda b, h: (b, h)),
                  pl.BlockSpec((seq, LANES), lambda b, h: (b, h // 2)),
                  pl.BlockSpec((seq, 2 * HD), lambda b, h: (b, h)),
                  pl.BlockSpec((seq, LANES), lambda b, h: (b, 0))],
        out_specs=pl.BlockSpec((seq, HD), lambda b, h: (b, h)),
        out_shape=jax.ShapeDtypeStruct((batch * seq, HEADS * HD), MXU_DT),
        scratch_shapes=[pltpu.VMEM((seq, 2 * HD), MXU_DT), pltpu.VMEM((seq, 2 * HD), MXU_DT),
                        pltpu.VMEM((seq, HD), MXU_DT)],
        compiler_params=_cparams(2),
        name="mla_prompt_attention",
    )(q, qr, kvx, kr2)


def _absorb_q_kernel(q_ref, w_ref, o_ref):
    o_ref[0] = _dot_nt(q_ref[...].astype(MXU_DT), w_ref[...].astype(MXU_DT))


def mla_absorb_q(q, w_kv_b):
    m = q.shape[0]
    return pl.pallas_call(
        _absorb_q_kernel,
        grid=(HEADS,),
        in_specs=[pl.BlockSpec((m, HD), lambda h: (0, h)),
                  pl.BlockSpec((KV_LORA, HD), lambda h: (0, 2 * h))],
        out_specs=pl.BlockSpec((1, m, KV_LORA), lambda h: (h, 0, 0)),
        out_shape=jax.ShapeDtypeStruct((HEADS, m, KV_LORA), F32),
        compiler_params=_cparams(1),
        name="mla_absorb_q",
    )(q, w_kv_b)


def _absorb_o_kernel(o_ref, w_ref, out_ref):
    out_ref[...] = _dot(o_ref[0].astype(MXU_DT), w_ref[...].astype(MXU_DT)).astype(out_ref.dtype)


def mla_absorb_o(o_lat, w_kv_b):
    m = o_lat.shape[1]
    return pl.pallas_call(
        _absorb_o_kernel,
        grid=(HEADS,),
        in_specs=[pl.BlockSpec((1, m, KV_LORA), lambda h: (h, 0, 0)),
                  pl.BlockSpec((KV_LORA, HD), lambda h: (0, 2 * h + 1))],
        out_specs=pl.BlockSpec((m, HD), lambda h: (0, h)),
        out_shape=jax.ShapeDtypeStruct((m, HEADS * HD), MXU_DT),
        compiler_params=_cparams(1),
        name="mla_absorb_o",
    )(o_lat, w_kv_b)


def _mla_decode_kernel(pt_ref, ql_ref, qr_ref, new_ref, cache_ref, o_ref, buf, sem,
                       *, n_chunks, pages):
    b = pl.program_id(0)
    nb = pl.num_programs(0)

    def chunk_copies(bb, ch, slot):
        cps = []
        for p in range(pages):
            page = pt_ref[bb, ch * pages + p]
            cps.append(pltpu.make_async_copy(cache_ref.at[page], buf.at[slot, p], sem.at[slot]))
        return cps

    ahead = RING - 1

    @pl.when(b == 0)
    def _():
        for g0 in range(ahead):
            for cp in chunk_copies(0, g0, g0):
                cp.start()

    ql = (ql_ref[0] * MLA_SCALE).astype(MXU_DT)
    qr = (qr_ref[0] * MLA_SCALE).astype(MXU_DT)

    def chunk_step(ch, carry):
        m, l, acc = carry
        g = b * n_chunks + ch
        slot = g % RING
        nxt = (g + ahead) % RING

        @pl.when(ch + ahead < n_chunks)
        def _():
            for cp in chunk_copies(b, ch + ahead, nxt):
                cp.start()

        @pl.when((ch + ahead >= n_chunks) & (b + 1 < nb))
        def _():
            for cp in chunk_copies(b + 1, ch + ahead - n_chunks, nxt):
                cp.start()

        for cp in chunk_copies(b, ch, slot):
            cp.wait()
        s = jnp.concatenate(
            [_dot(ql, buf[slot, p, 0:KV_LORA, :].astype(MXU_DT))
             + _dot(qr, buf[slot, p, KV_LORA:MLA_ROW, :].astype(MXU_DT))
             for p in range(pages)], axis=1)
        m_new = jnp.maximum(m, jnp.max(s, axis=-1, keepdims=True))
        alpha = jnp.exp(m - m_new)
        pr = jnp.exp(s - m_new)
        l = l * alpha + jnp.sum(pr, axis=-1, keepdims=True)
        pr = pr.astype(MXU_DT)
        acc = acc * alpha
        for p in range(pages):
            acc = acc + _dot_nt(pr[:, p * LANES:(p + 1) * LANES],
                                buf[slot, p, 0:KV_LORA, :].astype(MXU_DT))
        return m_new, l, acc

    m, l, acc = lax.fori_loop(
        0, n_chunks, chunk_step,
        (jnp.full((HEADS, 1), NEG_INF, F32), jnp.zeros((HEADS, 1), F32),
         jnp.zeros((HEADS, KV_LORA), F32)))

    new = new_ref[0].astype(MXU_DT).astype(F32)
    s_new = (jnp.sum(ql.astype(F32) * new[:, :KV_LORA], axis=-1, keepdims=True)
             + jnp.sum(qr.astype(F32) * new[:, KV_LORA:], axis=-1, keepdims=True))
    m_f = jnp.maximum(m, s_new)
    alpha = jnp.exp(m - m_f)
    pn = jnp.exp(s_new - m_f)
    pn_r = pn.astype(MXU_DT).astype(F32)
    o_ref[0] = (acc * alpha + pn_r * new[:, :KV_LORA]) / (l * alpha + pn)


def mla_decode(q_lat, q_rope, rows_new, page_table, cache, *, pages=16):
    db, n_pages = page_table.shape
    assert n_pages % pages == 0
    n_chunks = n_pages // pages
    assert n_chunks >= RING - 1
    grid_spec = pltpu.PrefetchScalarGridSpec(
        num_scalar_prefetch=1,
        grid=(db,),
        in_specs=[pl.BlockSpec((1, HEADS, KV_LORA), lambda b, pt: (b, 0, 0)),
                  pl.BlockSpec((1, HEADS, QK_ROPE), lambda b, pt: (b, 0, 0)),
                  pl.BlockSpec((1, 1, MLA_ROW), lambda b, pt: (b, 0, 0)),
                  pl.BlockSpec(memory_space=pl.ANY)],
        out_specs=pl.BlockSpec((1, HEADS, KV_LORA), lambda b, pt: (b, 0, 0)),
        scratch_shapes=[pltpu.VMEM((RING, pages, MLA_ROW, LANES), F32),
                        pltpu.SemaphoreType.DMA((RING,))])
    return pl.pallas_call(
        functools.partial(_mla_decode_kernel, n_chunks=n_chunks, pages=pages),
        grid_spec=grid_spec,
        out_shape=jax.ShapeDtypeStruct((db, HEADS, KV_LORA), F32),
        compiler_params=_cparams(1),
        name="mla_decode",
    )(page_table, q_lat, q_rope, rows_new, cache)


def _rope_tables(pos):
    inv = ROPE_THETA ** (-jnp.arange(0, QK_ROPE, 2, dtype=F32) / QK_ROPE)
    ang = pos.astype(F32)[:, None] * inv[None, :]
    cos, sin = jnp.cos(ang), jnp.sin(ang)
    return jnp.tile(jnp.concatenate([cos, cos], axis=-1), (1, 2)), \
        jnp.tile(jnp.concatenate([-sin, sin], axis=-1), (1, 2))


def _bias_ids(seq, past_len, w_buf):
    nt = seq // QB
    r = np.arange(QB)[:, None]
    c = np.arange(QB)[None, :]
    prev = _t5_bucket_np(r - c + QB)
    diag = _t5_bucket_np(r - c)
    qi = np.arange(nt)[:, None, None]
    cmp_p = _t5_bucket_np(qi * QB + r[None] - (c[None] * CMP_STRIDE + CMP_BLOCK - 1)).reshape(nt * QB, QB)
    n_tok = past_len // CMP_STRIDE
    cmp_s = _t5_bucket_np(past_len - (np.arange(n_tok) * CMP_STRIDE + CMP_BLOCK - 1)).reshape(-1, LANES)
    off = np.arange(LANES) % SLC_BLOCK
    slc_s = np.stack([_t5_bucket_np(SLC_BLOCK - off), _t5_bucket_np(2 * SLC_BLOCK - off),
                      _t5_bucket_np(np.full(LANES, 3 * SLC_BLOCK)), _t5_bucket_np(np.zeros(LANES))])
    win_s = _t5_bucket_np(w_buf - np.arange(w_buf)).reshape(-1, LANES)
    parts = [prev, diag, cmp_p, cmp_s, slc_s, win_s]
    rows = sum(p.shape[0] for p in parts)
    pad = -rows % 256
    ids = np.concatenate(parts + [np.zeros((pad, LANES), np.int32)], axis=0).astype(np.int32)
    offs = np.cumsum([0] + [p.shape[0] for p in parts])
    return ids, offs


def kernel(x_prompt, x_sample, p_prompt, p_sample, cache_nsa_kv, state_nsa_win, cache_mla_kv, page_table, t5_table, ln_g, ln_b, nsa_w_in, nsa_cmp_a, nsa_cmp_phi, nsa_w_o, mla_w_in, mla_q_norm, mla_kv_norm, mla_w_q_b, mla_w_kv_b, mla_w_o, ffn_w_in, ffn_w_out, ple_w_gate, ple_w_proj):
    batch, seq, d = x_prompt.shape
    db = x_sample.shape[0]
    depth = ln_g.shape[0]
    assert depth == 2 and x_sample.shape[1] == 1
    n_pages = page_table.shape[1]
    page_size = cache_nsa_kv.shape[2]
    assert page_size == LANES
    past_len = n_pages * page_size
    npb = past_len // SLC_BLOCK
    n_pool = cache_nsa_kv.shape[1]
    w_buf = state_nsa_win.shape[2]
    alpha = (2 * depth) ** 0.25
    q_cols = HEADS * HD
    kv_cols = NSA_KV_SLOTS * HD
    mp = batch * seq

    ids, offs = _bias_ids(seq, past_len, w_buf)
    assert offs[2] == 256
    bias = t5_bias_lookup(t5_table, jnp.asarray(ids), n_rel=1)
    sect = lambda k: bias[:, offs[k]:offs[k + 1]]
    bias_near = jnp.concatenate([sect(0), sect(1)], axis=-1)
    bias_c = sect(2).reshape(HEADS, seq // QB, QB, QB).transpose(1, 0, 2, 3)
    bias_cs = sect(3).reshape(HEADS, -1)
    bias_ss = sect(4)
    bias_slc = bias_ss[:, :3].transpose(1, 0, 2)
    bias_0 = bias_ss[:, 3]
    bias_w = sect(5).reshape(HEADS, w_buf)

    def dense_tail(x, mixed_in, w_o, p, i):
        y = matmul_residual(mixed_in, w_o, 0, x, alpha=alpha)
        hmid, x = swiglu_ln(y, ln_g[i, 0], ln_b[i, 0], ffn_w_in, i)
        y = matmul_residual(hmid, ffn_w_out, i, x, alpha=alpha, tn=256)
        return ple_ln(y, ln_g[i, 1], ln_b[i, 1], ple_w_gate, p, ple_w_proj, i)

    pp = p_prompt.reshape(depth, mp, -1)
    ps = p_sample.reshape(depth, db, -1)

    xp = x_prompt.reshape(mp, d)
    xs = x_sample.reshape(db, d)
    w_in0 = nsa_w_in[0]
    hp = matmul(xp, w_in0, tn=512)
    hs = matmul(xs, w_in0, tn=512)

    att_p = nsa_prompt_attention(hp, nsa_cmp_a[0], nsa_cmp_phi[0], bias_c, bias_near,
                                 batch=batch, seq=seq)

    hp3 = hp.reshape(batch, seq, -1)
    nsa_kv_prompt = hp3[:, :, q_cols:q_cols + 4 * HD].reshape(1, batch, seq, 4, HD)
    win_rows_p = hp3[:, :, q_cols + 4 * HD:q_cols + kv_cols].reshape(batch, seq, 2, HD)
    if seq >= w_buf:
        nsa_win_prompt = win_rows_p[:, seq - w_buf:][None]
    else:
        nsa_win_prompt = jnp.pad(win_rows_p, ((0, 0), (w_buf - seq, 0), (0, 0), (0, 0)))[None]

    qs = hs[:, :q_cols].reshape(db, HEADS, HD)
    kvn = hs[:, q_cols:q_cols + kv_cols].reshape(db, 1, kv_cols)
    cache_rows = cache_nsa_kv.reshape(cache_nsa_kv.shape[0] * n_pool, page_size * NSA_CACHE_SLOTS, HD)
    o_cmp_s, imp_s = nsa_sample_compressed(qs, page_table, cache_rows, nsa_cmp_a[0], nsa_cmp_phi[0],
                                           bias_cs, past_len=past_len)
    idx = nsa_sample_select(imp_s.reshape(db, -1), npb=npb)
    win_state = state_nsa_win[0].reshape(db, w_buf * 2, HD)
    o_slc_s, o_win_s = nsa_sample_slc_win(qs, kvn, win_state, idx, page_table, cache_rows,
                                          bias_slc, bias_0, bias_w, npb=npb)
    gate_logits = hs[:, q_cols + kv_cols:].reshape(db, 3, HEADS).transpose(1, 0, 2).reshape(3, db * HEADS, 1)
    att_s = nsa_gate_combine(gate_logits, o_cmp_s.reshape(db * HEADS, HD),
                             o_slc_s.reshape(db * HEADS, HD), o_win_s.reshape(db * HEADS, HD))
    att_s = att_s.reshape(db, q_cols)

    nsa_kv_sample = hs[:, q_cols:q_cols + 4 * HD].reshape(1, db, 1, 4, HD)
    new_win = hs[:, q_cols + 4 * HD:q_cols + kv_cols].reshape(db, 1, 2, HD)
    nsa_win_sample = jnp.concatenate([state_nsa_win[0], new_win], axis=1)[:, -w_buf:][None]

    xp = dense_tail(xp, att_p, nsa_w_o, pp, 0)
    xs = dense_tail(xs, att_s, nsa_w_o, ps, 0)

    wqb = mla_w_q_b[0].reshape(Q_LORA, HEADS, HD + QK_ROPE)
    wqb = jnp.concatenate([wqb[:, :, :HD].reshape(Q_LORA, HEADS * HD),
                           wqb[:, :, HD:].reshape(Q_LORA, HEADS * QK_ROPE)], axis=1)
    w_kv_b = mla_w_kv_b[0]
    cos_p, sin_p = _rope_tables(jnp.arange(seq))
    cos_s, sin_s = _rope_tables(jnp.full((db,), past_len))

    tmp = 512
    hp = matmul(xp, mla_w_in[0], tn=512)
    cq_p, ckv_p, rows_p, kr2_p = mla_prep(hp, mla_q_norm[0], mla_kv_norm[0], cos_p, sin_p,
                                          pos_blocks=seq // tmp, tm=tmp)
    q_p = matmul(cq_p, wqb, tn=512)
    qr_p = rope_q(q_p, cos_p, sin_p, pos_blocks=seq // tmp, tm=tmp)
    kvx_p = matmul(ckv_p, w_kv_b, tn=512, out_dtype=MXU_DT)
    att_p = mla_prompt_attention(q_p, qr_p, kvx_p, kr2_p, batch=batch, seq=seq)

    hs = matmul(xs, mla_w_in[0], tn=512)
    cq_s, _, rows_s, _ = mla_prep(hs, mla_q_norm[0], mla_kv_norm[0], cos_s, sin_s, pos_blocks=1, tm=db)
    q_s = matmul(cq_s, wqb, tn=512)
    qr_s = rope_q(q_s, cos_s, sin_s, pos_blocks=1, tm=db)
    q_lat = mla_absorb_q(q_s, w_kv_b).transpose(1, 0, 2)
    cache_mla = jnp.swapaxes(cache_mla_kv, 2, 3).reshape(cache_mla_kv.shape[0] * n_pool, MLA_ROW, page_size)
    o_lat = mla_decode(q_lat, qr_s.reshape(db, HEADS, QK_ROPE), rows_s.reshape(db, 1, MLA_ROW),
                       page_table, cache_mla)
    att_s = mla_absorb_o(o_lat.transpose(1, 0, 2), w_kv_b)

    xp = dense_tail(xp, att_p, mla_w_o, pp, 1)
    xs = dense_tail(xs, att_s, mla_w_o, ps, 1)

    return (xp.reshape(batch, seq, d), xs.reshape(db, 1, d),
            nsa_kv_prompt, nsa_win_prompt, rows_p.reshape(1, batch, seq, MLA_ROW),
            nsa_kv_sample, nsa_win_sample, rows_s.reshape(1, db, 1, MLA_ROW))
```

```python
import functools
import math

import numpy as np
import jax
import jax.numpy as jnp
from jax import lax
from jax.experimental import pallas as pl
from jax.experimental.pallas import tpu as pltpu

F32 = jnp.float32
MXU_DT = jnp.bfloat16
HI = lax.Precision.HIGHEST

HEADS = 16
HD = 128
CMP_STRIDE = 16
CMP_BLOCK = 32
SLC_BLOCK = 64
N_SEL = 16
WINDOW = 512
NSA_KV_SLOTS = 6
NSA_CACHE_SLOTS = 4
NSA_SCALE = HD ** -0.5
Q_LORA = 768
KV_LORA = 512
QK_ROPE = 64
MLA_ROW = KV_LORA + QK_ROPE
MLA_SCALE = (HD + QK_ROPE) ** -0.5
ROPE_THETA = 10000.0
T5_BUCKETS = 32
T5_MAX_DIST = 128
LN_EPS = 1e-5
RMS_EPS = 1e-6
NEG_INF = -1e30
FORCE_SCORE = 1e4
QB = 128
LANES = 128
HEAD_GROUP = 8
FAR_CHUNK = 512
MLA_QBLK = 512
RING = 3

VMEM_LIMIT = 56 * 1024 * 1024


def _cparams(n_axes):
    return pltpu.CompilerParams(dimension_semantics=("arbitrary",) * n_axes,
                                vmem_limit_bytes=VMEM_LIMIT)


def _dot(a, b):
    return jnp.dot(a, b, preferred_element_type=F32)


def _dot_nt(a, b):
    return lax.dot_general(a, b, (((1,), (1,)), ((), ())), preferred_element_type=F32)


def _dot_hi(a, b):
    return jnp.dot(a, b, precision=HI, preferred_element_type=F32)


def _sigmoid(x):
    return 1.0 / (1.0 + jnp.exp(-x))


def _mm_kernel(x_ref, w_ref, o_ref, xb_ref):
    @pl.when(pl.program_id(1) == 0)
    def _():
        xb_ref[...] = x_ref[...].astype(MXU_DT)
    o_ref[...] = _dot(xb_ref[...], w_ref[...].astype(MXU_DT)).astype(o_ref.dtype)


def matmul(x, w, *, tn, out_dtype=F32, tm=1024):
    m, k = x.shape
    n = w.shape[1]
    tm = min(tm, m)
    return pl.pallas_call(
        _mm_kernel,
        grid=(pl.cdiv(m, tm), pl.cdiv(n, tn)),
        in_specs=[pl.BlockSpec((tm, k), lambda i, j: (i, 0)),
                  pl.BlockSpec((k, tn), lambda i, j: (0, j))],
        out_specs=pl.BlockSpec((tm, tn), lambda i, j: (i, j)),
        out_shape=jax.ShapeDtypeStruct((m, n), out_dtype),
        scratch_shapes=[pltpu.VMEM((tm, k), MXU_DT)],
        compiler_params=_cparams(2),
        name="matmul",
    )(x, w)


def _row_stats(y):
    mu = jnp.mean(y, axis=-1, keepdims=True)
    yc = y - mu
    var = jnp.mean(yc * yc, axis=-1, keepdims=True)
    return mu, lax.rsqrt(var + LN_EPS)


def _layer_norm(y, g, b):
    mu, rstd = _row_stats(y)
    return (y - mu) * rstd * g + b


def _mm_res_kernel(x_ref, w_ref, r_ref, o_ref, *, alpha):
    o_ref[...] = alpha * r_ref[...] + _dot(x_ref[...].astype(MXU_DT), w_ref[...].astype(MXU_DT))


def _mm_lnres_kernel(x_ref, w_ref, y_ref, mu_ref, rs_ref, g_ref, b_ref, o_ref, *, alpha):
    resid = (y_ref[...] - mu_ref[...]) * rs_ref[...] * g_ref[...] + b_ref[...]
    o_ref[...] = alpha * resid + _dot(x_ref[...].astype(MXU_DT), w_ref[...].astype(MXU_DT))


def matmul_ln_residual(x, w, layer, y, mu, rstd, g, b, *, alpha, tn=256, tm=1024):
    m, k = x.shape
    n = w.shape[2]
    tm = min(tm, m)
    return pl.pallas_call(
        functools.partial(_mm_lnres_kernel, alpha=alpha),
        grid=(pl.cdiv(m, tm), pl.cdiv(n, tn)),
        in_specs=[pl.BlockSpec((tm, k), lambda i, j: (i, 0)),
                  pl.BlockSpec((None, k, tn), lambda i, j: (layer, 0, j)),
                  pl.BlockSpec((tm, tn), lambda i, j: (i, j)),
                  pl.BlockSpec((tm, 1), lambda i, j: (i, 0)),
                  pl.BlockSpec((tm, 1), lambda i, j: (i, 0)),
                  pl.BlockSpec((1, tn), lambda i, j: (0, j)),
                  pl.BlockSpec((1, tn), lambda i, j: (0, j))],
        out_specs=pl.BlockSpec((tm, tn), lambda i, j: (i, j)),
        out_shape=jax.ShapeDtypeStruct((m, n), F32),
        compiler_params=_cparams(2),
        name="matmul_ln_residual",
    )(x, w, y, mu, rstd, g.reshape(1, n), b.reshape(1, n))


def matmul_residual(x, w, layer, resid, *, alpha, tn=512, tm=1024):
    m, k = x.shape
    n = w.shape[2]
    tm = min(tm, m)
    return pl.pallas_call(
        functools.partial(_mm_res_kernel, alpha=alpha),
        grid=(pl.cdiv(m, tm), pl.cdiv(n, tn)),
        in_specs=[pl.BlockSpec((tm, k), lambda i, j: (i, 0)),
                  pl.BlockSpec((None, k, tn), lambda i, j: (layer, 0, j)),
                  pl.BlockSpec((tm, tn), lambda i, j: (i, j))],
        out_specs=pl.BlockSpec((tm, tn), lambda i, j: (i, j)),
        out_shape=jax.ShapeDtypeStruct((m, n), F32),
        compiler_params=_cparams(2),
        name="matmul_residual",
    )(x, w, resid)


def _swiglu_ln_kernel(y_ref, g_ref, b_ref, w1_ref, w2_ref, o_ref, mu_ref, rs_ref, xb_ref):
    @pl.when(pl.program_id(1) == 0)
    def _():
        y = y_ref[...]
        mu, rstd = _row_stats(y)
        mu_ref[...] = mu
        rs_ref[...] = rstd
        xb_ref[...] = ((y - mu) * rstd * g_ref[...] + b_ref[...]).astype(MXU_DT)
    xb = xb_ref[...]
    h1 = _dot(xb, w1_ref[...].astype(MXU_DT))
    h2 = _dot(xb, w2_ref[...].astype(MXU_DT))
    o_ref[...] = (h1 * _sigmoid(h1) * h2).astype(o_ref.dtype)


def swiglu_ln(y, g, b, w_in, layer, *, tn=512, tm=1024):
    m, k = y.shape
    d_ff = w_in.shape[2] // 2
    assert d_ff % tn == 0
    nff = d_ff // tn
    tm = min(tm, m)
    return pl.pallas_call(
        _swiglu_ln_kernel,
        grid=(pl.cdiv(m, tm), nff),
        in_specs=[pl.BlockSpec((tm, k), lambda i, j: (i, 0), pipeline_mode=pl.Buffered(1)),
                  pl.BlockSpec((1, k), lambda i, j: (0, 0)),
                  pl.BlockSpec((1, k), lambda i, j: (0, 0)),
                  pl.BlockSpec((None, k, tn), lambda i, j: (layer, 0, j)),
                  pl.BlockSpec((None, k, tn), lambda i, j: (layer, 0, j + nff))],
        out_specs=[pl.BlockSpec((tm, tn), lambda i, j: (i, j)),
                   pl.BlockSpec((tm, 1), lambda i, j: (i, 0)),
                   pl.BlockSpec((tm, 1), lambda i, j: (i, 0))],
        out_shape=[jax.ShapeDtypeStruct((m, d_ff), MXU_DT), jax.ShapeDtypeStruct((m, 1), F32),
                   jax.ShapeDtypeStruct((m, 1), F32)],
        scratch_shapes=[pltpu.VMEM((tm, k), MXU_DT)],
        compiler_params=_cparams(2),
        name="swiglu_ln",
    )(y, g.reshape(1, k), b.reshape(1, k), w_in, w_in)


def _ple_ln_kernel(y_ref, g_ref, b_ref, wg_ref, p_ref, wp_ref, o_ref, xn_ref, xb_ref, *, tn):
    j = pl.program_id(1)

    @pl.when(j == 0)
    def _():
        xn = _layer_norm(y_ref[...], g_ref[...], b_ref[...])
        xb_ref[...] = xn.astype(MXU_DT)
        for t in range(xn_ref.shape[0]):
            xn_ref[t] = xn[:, t * tn:(t + 1) * tn]

    gate = _sigmoid(_dot(xb_ref[...], wg_ref[...].astype(MXU_DT)))
    proj = _dot(p_ref[...].astype(MXU_DT), wp_ref[...].astype(MXU_DT))
    o_ref[...] = xn_ref[j] + gate * proj


def ple_ln(y, g, b, w_gate, p, w_proj, layer, *, tn=512, tm=1024):
    m, k = y.shape
    n = w_gate.shape[2]
    pe = p.shape[2]
    tm = min(tm, m)
    assert n == k and n % tn == 0
    return pl.pallas_call(
        functools.partial(_ple_ln_kernel, tn=tn),
        grid=(pl.cdiv(m, tm), n // tn),
        in_specs=[pl.BlockSpec((tm, k), lambda i, j: (i, 0), pipeline_mode=pl.Buffered(1)),
                  pl.BlockSpec((1, k), lambda i, j: (0, 0)),
                  pl.BlockSpec((1, k), lambda i, j: (0, 0)),
                  pl.BlockSpec((None, k, tn), lambda i, j: (layer, 0, j)),
                  pl.BlockSpec((None, tm, pe), lambda i, j: (layer, i, 0)),
                  pl.BlockSpec((None, pe, tn), lambda i, j: (layer, 0, j))],
        out_specs=pl.BlockSpec((tm, tn), lambda i, j: (i, j)),
        out_shape=jax.ShapeDtypeStruct((m, n), F32),
        scratch_shapes=[pltpu.VMEM((n // tn, tm, tn), F32), pltpu.VMEM((tm, k), MXU_DT)],
        compiler_params=_cparams(2),
        name="ple_ln",
    )(y, g.reshape(1, k), b.reshape(1, k), w_gate, p, w_proj)


def _t5_bucket_np(dist):
    n = np.maximum(np.asarray(dist, np.int64), 0)
    max_exact = T5_BUCKETS // 2
    nf = np.maximum(n, 1).astype(np.float32)
    scaled = (np.log(nf / np.float32(max_exact)) / np.float32(math.log(T5_MAX_DIST / max_exact))
              * np.float32(T5_BUCKETS - max_exact))
    large = np.minimum(max_exact + scaled.astype(np.int32), T5_BUCKETS - 1)
    return np.where(n < max_exact, n, large).astype(np.int32)


def _bias_kernel(tab_ref, ids_ref, o_ref, *, n_rel):
    ids = ids_ref[...]
    relative = pl.program_id(0) < n_rel
    for h in range(HEADS):
        acc = jnp.zeros(ids.shape, F32)
        for bkt in range(T5_BUCKETS):
            acc = jnp.where(ids == bkt, tab_ref[bkt, h], acc)
        o_ref[h] = acc - jnp.where(relative, tab_ref[T5_BUCKETS - 1, h], 0.0)


def t5_bias_lookup(table, ids, *, n_rel, tr=256):
    r = ids.shape[0]
    assert r % tr == 0
    return pl.pallas_call(
        functools.partial(_bias_kernel, n_rel=n_rel),
        grid=(r // tr,),
        in_specs=[pl.BlockSpec(memory_space=pltpu.SMEM),
                  pl.BlockSpec((tr, LANES), lambda i: (i, 0))],
        out_specs=pl.BlockSpec((HEADS, tr, LANES), lambda i: (0, i, 0)),
        out_shape=jax.ShapeDtypeStruct((HEADS, r, LANES), F32),
        compiler_params=_cparams(1),
        name="t5_bias_lookup",
    )(table, ids)


def _softmax_rows(s):
    m = jnp.max(s, axis=-1, keepdims=True)
    e = jnp.exp(s - m)
    l = jnp.sum(e, axis=-1, keepdims=True)
    return jnp.where(m > 0.5 * NEG_INF, e / l, 0.0)


def _lane_tiles(x):
    return [x[..., t * LANES:(t + 1) * LANES] for t in range(x.shape[-1] // LANES)]


def _fold(op, tiles):
    out = tiles[0]
    for t in tiles[1:]:
        out = op(out, t)
    return out


def _rope_lanes(x, cos_t, sin_t):
    lane = lax.broadcasted_iota(jnp.int32, x.shape, 1)
    first_half = (lane % QK_ROPE) < (QK_ROPE // 2)
    rot = jnp.where(first_half, pltpu.roll(x, LANES - QK_ROPE // 2, 1), pltpu.roll(x, QK_ROPE // 2, 1))
    return x * cos_t + rot * sin_t


def _nsa_prompt_kernel(q_ref, g_ref, kcr_ref, vcr_ref, ksr_ref, vsr_ref, kwr_ref, vwr_ref,
                       a_ref, phi_ref, bc_ref, bn_ref, amat_ref, emat_ref, o_ref,
                       kc_scr, vc_scr, ks_scr, vs_scr, kw_scr, vw_scr, q_scr, selk_scr, a_scr,
                       mx_scr, mb_scr, l_scr, acc_scr, sn_scr, sf_scr, oc_scr, os_scr, ow_scr,
                       *, nt, n_blk):
    i = pl.program_id(1)
    seq = nt * QB
    n_cmp = nt * (QB // CMP_STRIDE) - 1
    wt = WINDOW // QB
    hg = HEAD_GROUP
    gm = hg * QB
    fc = FAR_CHUNK
    tpc = fc // QB

    @pl.when(i == 0)
    def _():
        ks_scr[0:QB, :] = jnp.zeros((QB, HD), MXU_DT)
        vs_scr[0:QB, :] = jnp.zeros((QB, HD), MXU_DT)
        kw_scr[0:wt * QB, :] = jnp.zeros((wt * QB, HD), MXU_DT)
        vw_scr[0:wt * QB, :] = jnp.zeros((wt * QB, HD), MXU_DT)
        ks_scr[QB:QB + seq, :] = ksr_ref[...].astype(MXU_DT)
        vs_scr[QB:QB + seq, :] = vsr_ref[...].astype(MXU_DT)
        kw_scr[wt * QB:wt * QB + seq, :] = kwr_ref[...].astype(MXU_DT)
        vw_scr[wt * QB:wt * QB + seq, :] = vwr_ref[...].astype(MXU_DT)
        for which, (rows_ref, dst) in enumerate(((kcr_ref, kc_scr), (vcr_ref, vc_scr))):
            ng = nt * QB // CMP_STRIDE
            first = jnp.zeros((ng, HD), F32)
            second = jnp.zeros((ng, HD), F32)
            for j in range(CMP_STRIDE):
                xj = rows_ref[pl.ds(j, ng, stride=CMP_STRIDE), :]
                first = first + xj * a_ref[which, j:j + 1, :]
                second = second + xj * a_ref[which, CMP_STRIDE + j:CMP_STRIDE + j + 1, :]
            pre = first + pltpu.roll(second, ng - 1, 0)
            dst[...] = _dot_hi(pre, phi_ref[which]).astype(MXU_DT)

    for h in range(HEADS):
        q_scr[h * QB:(h + 1) * QB, :] = (q_ref[:, h * HD:(h + 1) * HD] * NSA_SCALE).astype(MXU_DT)

    row = lax.broadcasted_iota(jnp.int32, (QB, QB), 0)
    lane = lax.broadcasted_iota(jnp.int32, (QB, QB), 1)
    qpos = i * QB + row
    tri = jnp.where(lane <= row, 0.0, NEG_INF)
    cmp_valid = (qpos >= lane * CMP_STRIDE + (CMP_BLOCK - 1)) & (lane < n_cmp)
    cmp_mask = jnp.where(cmp_valid, 0.0, NEG_INF)

    s = _dot_nt(q_scr[...], kc_scr[...]).reshape(HEADS, QB, QB) + bc_ref[0] + cmp_mask[None]
    p = _softmax_rows(s)
    imp = jnp.sum(p, axis=0)
    oc_scr[...] = _dot(p.reshape(HEADS * QB, QB).astype(MXU_DT), vc_scr[...])

    slc_imp = _dot_hi(imp, amat_ref[...])
    cur = qpos // SLC_BLOCK
    forced = (lane == 0) | (lane == cur) | (lane == cur - 1)
    score = jnp.where(lane > cur, -1.0, jnp.where(forced, FORCE_SCORE, slc_imp))
    score = jnp.where(lane < n_blk, score, -2.0)
    rank = jnp.zeros((QB, QB), F32)
    for b2 in range(n_blk):
        col = score[:, b2:b2 + 1]
        beats = (col > score) | ((col == score) & (lane > b2))
        rank = rank + jnp.where(beats, 1.0, 0.0)
    sel = jnp.where((rank < min(N_SEL, n_blk)) & (lane < n_blk), 1.0, 0.0)
    sel_keys = _dot(sel.astype(MXU_DT), emat_ref[...])
    for kt in range(nt):
        tile = jnp.where(sel_keys[:, kt * QB:(kt + 1) * QB] > 0.5, 0.0, NEG_INF)
        selk_scr[kt] = tile
        a_scr[kt // tpc, :, (kt % tpc) * QB:(kt % tpc + 1) * QB] = jnp.where(kt < i - 1, tile, NEG_INF)
    prev_ok = jnp.where(i > 0, 0.0, NEG_INF)
    near_mask = jnp.concatenate([selk_scr[jnp.maximum(i - 1, 0)] + prev_ok, selk_scr[i] + tri], axis=1)

    zeros = jnp.zeros((QB, QB), F32)
    band = jnp.where(lane >= row, 0.0, NEG_INF)

    def tile_ok(t):
        return jnp.where(i - wt + t >= 0, 0.0, NEG_INF)

    win_far_mask = jnp.concatenate(
        [band + tile_ok(0)] + [zeros + tile_ok(t) for t in range(1, wt - 1)], axis=1)
    win_near_mask = jnp.concatenate([zeros + tile_ok(wt - 1), tri], axis=1)
    n_far = (i + tpc - 2) // tpc
    n0 = pl.multiple_of(i * QB, QB)

    def group(g, carry):
        r0 = pl.multiple_of(g * gm, gm)
        qg = q_scr[pl.ds(r0, gm), :]
        bn = bn_ref[pl.ds(g * hg, hg)]

        mx_scr[...] = jnp.full((hg, QB, QB), NEG_INF, F32)

        def far_max(c, carry2):
            k0 = pl.multiple_of(QB + c * fc, QB)
            s = _dot_nt(qg, ks_scr[pl.ds(k0, fc), :]).reshape(hg, QB, fc) + a_scr[c][None]
            sf_scr[c] = s
            mx_scr[...] = jnp.maximum(mx_scr[...], _fold(jnp.maximum, _lane_tiles(s)))
            return carry2

        lax.fori_loop(0, n_far, far_max, 0)
        sn = _dot_nt(qg, ks_scr[pl.ds(n0, 2 * QB), :]).reshape(hg, QB, 2 * QB) + bn + near_mask[None]
        sn_scr[...] = sn
        mfold = jnp.maximum(mx_scr[...], _fold(jnp.maximum, _lane_tiles(sn)))
        mb_scr[...] = jnp.broadcast_to(jnp.max(mfold, axis=-1, keepdims=True), (hg, QB, QB))
        l_scr[...] = jnp.zeros((hg, QB, QB), F32)
        acc_scr[...] = jnp.zeros((gm, HD), F32)

        def far_pv(c, carry2):
            k0 = pl.multiple_of(QB + c * fc, QB)
            mbv = mb_scr[...]
            ps = [jnp.exp(t - mbv) for t in _lane_tiles(sf_scr[c])]
            l_scr[...] += _fold(jnp.add, ps)
            pm = jnp.concatenate(ps, axis=-1).reshape(gm, fc).astype(MXU_DT)
            acc_scr[...] += _dot(pm, vs_scr[pl.ds(k0, fc), :])
            return carry2

        lax.fori_loop(0, n_far, far_pv, 0)
        mbv = mb_scr[...]
        pn = [jnp.exp(t - mbv) for t in _lane_tiles(sn_scr[...])]
        l = l_scr[...] + _fold(jnp.add, pn)
        pm = jnp.concatenate(pn, axis=-1).reshape(gm, 2 * QB).astype(MXU_DT)
        acc = acc_scr[...] + _dot(pm, vs_scr[pl.ds(n0, 2 * QB), :])
        o_slc = acc.reshape(hg, QB, HD) / jnp.sum(l, axis=-1, keepdims=True)
        os_scr[pl.ds(r0, gm), :] = o_slc.reshape(gm, HD)

        nw = (wt + 1) * QB
        nf = (wt - 1) * QB
        sw = _dot_nt(qg, kw_scr[pl.ds(n0, nw), :]).reshape(hg, QB, nw)
        s_far = sw[:, :, :nf] + win_far_mask[None]
        s_near = sw[:, :, nf:] + bn + win_near_mask[None]
        m = jnp.maximum(jnp.max(s_far, axis=-1, keepdims=True), jnp.max(s_near, axis=-1, keepdims=True))
        e_far = jnp.exp(s_far - m)
        e_near = jnp.exp(s_near - m)
        l = jnp.sum(e_far, axis=-1, keepdims=True) + jnp.sum(e_near, axis=-1, keepdims=True)
        pm = jnp.concatenate([e_far, e_near], axis=-1).reshape(gm, nw).astype(MXU_DT)
        o_win = _dot(pm, vw_scr[pl.ds(n0, nw), :]).reshape(hg, QB, HD) / l
        ow_scr[pl.ds(r0, gm), :] = o_win.reshape(gm, HD)
        return carry

    lax.fori_loop(0, HEADS // hg, group, 0)

    gates = _sigmoid(g_ref[...])
    for h in range(HEADS):
        rows = slice(h * QB, (h + 1) * QB)
        o = (gates[:, h:h + 1] * oc_scr[rows, :] + gates[:, HEADS + h:HEADS + h + 1] * os_scr[rows, :]
             + gates[:, 2 * HEADS + h:2 * HEADS + h + 1] * ow_scr[rows, :])
        o_ref[:, h * HD:(h + 1) * HD] = o.astype(o_ref.dtype)


def nsa_prompt_attention(h, cmp_a, cmp_phi, bias_c, bias_near, *, batch, seq):
    nt = seq // QB
    n_blk = seq // SLC_BLOCK
    wt = WINDOW // QB
    assert seq // CMP_STRIDE == QB and n_blk <= QB and WINDOW % QB == 0 and wt >= 2
    assert FAR_CHUNK % QB == 0 and seq % FAR_CHUNK == 0 and HEADS % HEAD_GROUP == 0
    q_cols = HEADS * HD
    kv_blk0 = q_cols // HD
    gate_blk = kv_blk0 + NSA_KV_SLOTS
    c = np.arange(QB)[:, None]
    b = np.arange(QB)[None, :]
    ratio = SLC_BLOCK // CMP_STRIDE
    amat = (((c >= ratio * b) & (c <= ratio * b + ratio - 1)).astype(np.float32)
            + ((c + 1 >= ratio * b) & (c + 1 <= ratio * b + ratio - 1)).astype(np.float32))
    amat = amat * (b < n_blk)
    emat = (np.arange(seq)[None, :] // SLC_BLOCK == np.arange(QB)[:, None]).astype(np.float32)

    kv_spec = lambda s: pl.BlockSpec((seq, HD), lambda bb, i, s=s: (bb, kv_blk0 + s))
    full = lambda shape: pl.BlockSpec(shape, lambda bb, i: (0,) * len(shape))
    keys_scr = lambda pad_tiles: pltpu.VMEM(((nt + pad_tiles) * QB, HD), MXU_DT)
    heads_scr = lambda dt: pltpu.VMEM((HEADS * QB, HD), dt)
    group_scr = lambda: pltpu.VMEM((HEAD_GROUP, QB, QB), F32)
    return pl.pallas_call(
        functools.partial(_nsa_prompt_kernel, nt=nt, n_blk=n_blk),
        grid=(batch, nt),
        in_specs=[pl.BlockSpec((QB, q_cols), lambda bb, i: (bb * nt + i, 0)),
                  pl.BlockSpec((QB, LANES), lambda bb, i: (bb * nt + i, gate_blk))]
                 + [kv_spec(s) for s in range(NSA_KV_SLOTS)]
                 + [full((2, CMP_BLOCK, HD)), full((2, HD, HD)),
                    pl.BlockSpec((1, HEADS, QB, QB), lambda bb, i: (i, 0, 0, 0)),
                    full((HEADS, QB, 2 * QB)), full((QB, QB)), full((QB, seq))],
        out_specs=pl.BlockSpec((QB, q_cols), lambda bb, i: (bb * nt + i, 0)),
        out_shape=jax.ShapeDtypeStruct((batch * seq, q_cols), MXU_DT),
        scratch_shapes=[pltpu.VMEM((QB, HD), MXU_DT), pltpu.VMEM((QB, HD), MXU_DT),
                        keys_scr(1), keys_scr(1), keys_scr(wt), keys_scr(wt),
                        heads_scr(MXU_DT), pltpu.VMEM((nt, QB, QB), F32),
                        pltpu.VMEM((seq // FAR_CHUNK, QB, FAR_CHUNK), F32),
                        group_scr(), group_scr(), group_scr(),
                        pltpu.VMEM((HEAD_GROUP * QB, HD), F32),
                        pltpu.VMEM((HEAD_GROUP, QB, 2 * QB), F32),
                        pltpu.VMEM((seq // FAR_CHUNK, HEAD_GROUP, QB, FAR_CHUNK), F32),
                        heads_scr(F32), heads_scr(F32), heads_scr(F32)],
        compiler_params=_cparams(2),
        name="nsa_prompt_attention",
    )(h, h, h, h, h, h, h, h, cmp_a, cmp_phi, bias_c, bias_near,
      jnp.asarray(amat), jnp.asarray(emat, MXU_DT))


def _nsa_sample_cmp_kernel(pt_ref, q_ref, w_ref, phi_ref, bias_ref, cache_ref, oc_ref, imp_ref,
                           buf, sem, part_scr, kc_scr, vc_scr, *, n_chunks, pages, past_len):
    b = pl.program_id(0)
    nb = pl.num_programs(0)
    prow = LANES * NSA_CACHE_SLOTS
    rows = pages * prow
    grow = CMP_STRIDE * NSA_CACHE_SLOTS
    groups = pages * LANES // CMP_STRIDE
    sub = 8
    nv = grow // sub

    def chunk_copies(bb, ch, slot):
        cps = []
        for p in range(pages):
            page = pt_ref[bb, ch * pages + p]
            cps.append(pltpu.make_async_copy(cache_ref.at[page],
                                             buf.at[slot, pl.ds(p * prow, prow), :], sem.at[slot]))
        if ch + 1 < n_chunks:
            page = pt_ref[bb, (ch + 1) * pages]
            cps.append(pltpu.make_async_copy(cache_ref.at[page, pl.ds(0, grow), :],
                                             buf.at[slot, pl.ds(rows, grow), :], sem.at[slot]))
        return cps

    ahead = RING - 1

    @pl.when(b == 0)
    def _():
        for g0 in range(ahead):
            for cp in chunk_copies(0, g0, g0):
                cp.start()

    for ch in range(n_chunks):
        g = b * n_chunks + ch
        slot = g % RING
        nxt = (g + ahead) % RING
        if ch + ahead < n_chunks:
            for cp in chunk_copies(b, ch + ahead, nxt):
                cp.start()
        else:
            @pl.when(b + 1 < nb)
            def _():
                for cp in chunk_copies(b + 1, ch + ahead - n_chunks, nxt):
                    cp.start()
        for cp in chunk_copies(b, ch, slot):
            cp.wait()
        if ch == n_chunks - 1:
            buf[slot, rows:rows + grow, :] = jnp.zeros((grow, HD), F32)
        x0 = buf[slot, 0:rows, :].reshape(groups, nv, sub, HD)
        x1 = buf[slot, grow:rows + grow, :].reshape(groups, nv, sub, HD)
        part = x0[:, 0] * w_ref[0, 0] + x1[:, 0] * w_ref[1, 0]
        for t in range(1, nv):
            part = part + x0[:, t] * w_ref[0, t] + x1[:, t] * w_ref[1, t]
        part_scr[...] = part.reshape(groups * sub, HD)
        for kv, dst in enumerate((kc_scr, vc_scr)):
            pre = (part_scr[pl.ds(kv, groups, stride=sub), :]
                   + part_scr[pl.ds(NSA_CACHE_SLOTS + kv, groups, stride=sub), :])
            dst[ch * groups:(ch + 1) * groups, :] = _dot_hi(pre, phi_ref[kv]).astype(MXU_DT)

    n_tok = n_chunks * groups
    q = (q_ref[0] * NSA_SCALE).astype(MXU_DT)
    tok = lax.broadcasted_iota(jnp.int32, (HEADS, n_tok), 1)
    valid = tok * CMP_STRIDE + (CMP_BLOCK - 1) <= past_len
    s = _dot_nt(q, kc_scr[...]) + bias_ref[...] + jnp.where(valid, 0.0, NEG_INF)
    p = _softmax_rows(s)
    oc_ref[0] = _dot(p.astype(MXU_DT), vc_scr[...])
    imp_ref[0] = jnp.sum(p, axis=0, keepdims=True)


def nsa_sample_compressed(q, page_table, cache_rows, cmp_a, cmp_phi, bias_c, *, past_len, pages=16):
    db, n_pages = page_table.shape
    assert n_pages % pages == 0
    n_chunks = n_pages // pages
    assert n_chunks >= RING - 1
    rows = (pages * LANES + CMP_STRIDE) * NSA_CACHE_SLOTS
    n_tok = n_pages * LANES // CMP_STRIDE
    nv = CMP_STRIDE // 2
    w = cmp_a.reshape(2, 2, nv, 2, HD).transpose(1, 2, 3, 0, 4)
    w = jnp.pad(w, ((0, 0), (0, 0), (0, 0), (0, NSA_CACHE_SLOTS - 2), (0, 0)))
    w = w.reshape(2, nv, 2 * NSA_CACHE_SLOTS, HD)
    grid_spec = pltpu.PrefetchScalarGridSpec(
        num_scalar_prefetch=1,
        grid=(db,),
        in_specs=[pl.BlockSpec((1, HEADS, HD), lambda b, pt: (b, 0, 0)),
                  pl.BlockSpec((2, nv, 2 * NSA_CACHE_SLOTS, HD), lambda b, pt: (0, 0, 0, 0)),
                  pl.BlockSpec((2, HD, HD), lambda b, pt: (0, 0, 0)),
                  pl.BlockSpec((HEADS, n_tok), lambda b, pt: (0, 0)),
                  pl.BlockSpec(memory_space=pl.ANY)],
        out_specs=[pl.BlockSpec((1, HEADS, HD), lambda b, pt: (b, 0, 0)),
                   pl.BlockSpec((1, 1, n_tok), lambda b, pt: (b, 0, 0))],
        scratch_shapes=[pltpu.VMEM((RING, rows, HD), F32),
                        pltpu.SemaphoreType.DMA((RING,)),
                        pltpu.VMEM((pages * LANES // CMP_STRIDE * 8, HD), F32),
                        pltpu.VMEM((n_tok, HD), MXU_DT), pltpu.VMEM((n_tok, HD), MXU_DT)])
    return pl.pallas_call(
        functools.partial(_nsa_sample_cmp_kernel, n_chunks=n_chunks, pages=pages, past_len=past_len),
        grid_spec=grid_spec,
        out_shape=[jax.ShapeDtypeStruct((db, HEADS, HD), F32),
                   jax.ShapeDtypeStruct((db, 1, n_tok), F32)],
        compiler_params=_cparams(1),
        name="nsa_sample_compressed",
    )(page_table, q, w, cmp_phi, bias_c, cache_rows)


def _select_kernel(imp_ref, amat_ref, idx_ref, *, npb, n_pick):
    slc_imp = _dot_hi(imp_ref[...], amat_ref[...])
    lane = lax.broadcasted_iota(jnp.int32, slc_imp.shape, 1).astype(F32)
    forced = (lane == 0.0) | (lane == float(npb - 1))
    score = jnp.where(forced, FORCE_SCORE, slc_imp)
    score = jnp.where(lane < float(npb), score, -2.0)
    out_lane = lax.broadcasted_iota(jnp.int32, idx_ref.shape, 1)
    out = jnp.zeros(idx_ref.shape, F32)
    for t in range(n_pick):
        best = jnp.max(score, axis=-1, keepdims=True)
        first = jnp.min(jnp.where(score == best, lane, 1e9), axis=-1, keepdims=True)
        out = jnp.where(out_lane == t, first, out)
        score = jnp.where(lane == first, -3.0, score)
    idx_ref[...] = out.astype(jnp.int32)


def nsa_sample_select(imp, *, npb):
    db, n_tok = imp.shape
    assert npb >= N_SEL
    nbp = -(-npb // LANES) * LANES
    c = np.arange(n_tok)[:, None]
    b = np.arange(nbp)[None, :]
    ratio = SLC_BLOCK // CMP_STRIDE
    amat = (((c >= ratio * b) & (c <= ratio * b + ratio - 1)).astype(np.float32)
            + ((c + 1 >= ratio * b) & (c + 1 <= ratio * b + ratio - 1)).astype(np.float32))
    amat = amat * (b < npb)
    return pl.pallas_call(
        functools.partial(_select_kernel, npb=npb, n_pick=N_SEL - 1),
        out_shape=jax.ShapeDtypeStruct((db, LANES), jnp.int32),
        compiler_params=pltpu.CompilerParams(vmem_limit_bytes=VMEM_LIMIT),
        name="nsa_sample_select",
    )(imp, jnp.asarray(amat))


def _nsa_sample_slc_win_kernel(idx_ref, pt_ref, q_ref, kvn_ref, win_ref, bsl_ref, b0_ref, bw_ref,
                               cache_ref, osl_ref, owin_ref, kbuf, sem, *, n_pick, npb):
    b = pl.program_id(0)
    nb = pl.num_programs(0)
    bpp = LANES // SLC_BLOCK
    brow = SLC_BLOCK * NSA_CACHE_SLOTS

    def block_copies(bb, slot):
        cps = []
        for t in range(n_pick):
            blk = idx_ref[bb, t]
            page = pt_ref[bb, blk // bpp]
            off = pl.multiple_of((blk % bpp) * brow, brow)
            cps.append(pltpu.make_async_copy(cache_ref.at[page, pl.ds(off, brow), :],
                                             kbuf.at[slot, pl.ds(t * brow, brow), :], sem.at[slot]))
        return cps

    slot = b % 2

    @pl.when(b == 0)
    def _():
        for cp in block_copies(0, 0):
            cp.start()

    @pl.when(b + 1 < nb)
    def _():
        for cp in block_copies(b + 1, 1 - slot):
            cp.start()

    for cp in block_copies(b, slot):
        cp.wait()

    n_keys = (n_pick + 1) * SLC_BLOCK
    kvn = kvn_ref[0]
    rowi = lax.broadcasted_iota(jnp.int32, (SLC_BLOCK, HD), 0)

    def gathered(cache_slot):
        parts = [kbuf[slot, pl.ds(t * brow + cache_slot, SLC_BLOCK, stride=NSA_CACHE_SLOTS), :]
                 for t in range(n_pick)]
        new_row = kvn[:, cache_slot * HD:(cache_slot + 1) * HD]
        parts.append(jnp.where(rowi == 0, new_row, 0.0))
        return jnp.concatenate(parts, axis=0).astype(MXU_DT)

    q = (q_ref[0] * NSA_SCALE).astype(MXU_DT)
    keys = gathered(2)
    vals = gathered(3)
    lane = lax.broadcasted_iota(jnp.int32, (HEADS, LANES), 1)

    def slot_bias(t):
        if t == n_pick:
            return b0_ref[...]
        blk = idx_ref[b, t]
        return jnp.where(blk == npb - 1, bsl_ref[0], jnp.where(blk == npb - 2, bsl_ref[1], bsl_ref[2]))

    tiles = []
    for u in range(n_keys // LANES):
        tiles.append(jnp.where(lane < SLC_BLOCK, slot_bias(2 * u), slot_bias(2 * u + 1)))
    bias = jnp.concatenate(tiles, axis=1)
    col = lax.broadcasted_iota(jnp.int32, (HEADS, n_keys), 1)
    s = _dot_nt(q, keys) + bias + jnp.where(col <= n_pick * SLC_BLOCK, 0.0, NEG_INF)
    p = _softmax_rows(s)
    osl_ref[0] = _dot(p.astype(MXU_DT), vals)

    w_buf = win_ref.shape[1] // 2
    wk = win_ref[0, pl.ds(0, w_buf, stride=2), :].astype(MXU_DT)
    wv = win_ref[0, pl.ds(1, w_buf, stride=2), :].astype(MXU_DT)
    sw = _dot_nt(q, wk) + bw_ref[...]
    new_k = kvn[:, 4 * HD:5 * HD].astype(MXU_DT).astype(F32)
    new_v = kvn[:, 5 * HD:6 * HD].astype(MXU_DT).astype(F32)
    s_new = jnp.sum(q.astype(F32) * new_k, axis=-1, keepdims=True) + b0_ref[:, 0:1]
    m = jnp.maximum(jnp.max(sw, axis=-1, keepdims=True), s_new)
    pw = jnp.exp(sw - m)
    pn = jnp.exp(s_new - m)
    l = jnp.sum(pw, axis=-1, keepdims=True) + pn
    pn_r = pn.astype(MXU_DT).astype(F32)
    owin_ref[0] = (_dot(pw.astype(MXU_DT), wv) + pn_r * new_v) / l


def nsa_sample_slc_win(q, kv_new, win_state, idx, page_table, cache_rows, bias_slc, bias_0, bias_w,
                       *, npb):
    db = q.shape[0]
    n_pick = N_SEL - 1
    w_buf = win_state.shape[1] // 2
    assert w_buf <= WINDOW and ((n_pick + 1) * SLC_BLOCK) % LANES == 0
    n_keys = (n_pick + 1) * SLC_BLOCK
    grid_spec = pltpu.PrefetchScalarGridSpec(
        num_scalar_prefetch=2,
        grid=(db,),
        in_specs=[pl.BlockSpec((1, HEADS, HD), lambda b, ix, pt: (b, 0, 0)),
                  pl.BlockSpec((1, 1, NSA_KV_SLOTS * HD), lambda b, ix, pt: (b, 0, 0)),
                  pl.BlockSpec((1, 2 * w_buf, HD), lambda b, ix, pt: (b, 0, 0)),
                  pl.BlockSpec((3, HEADS, LANES), lambda b, ix, pt: (0, 0, 0)),
                  pl.BlockSpec((HEADS, LANES), lambda b, ix, pt: (0, 0)),
                  pl.BlockSpec((HEADS, w_buf), lambda b, ix, pt: (0, 0)),
                  pl.BlockSpec(memory_space=pl.ANY)],
        out_specs=[pl.BlockSpec((1, HEADS, HD), lambda b, ix, pt: (b, 0, 0)),
                   pl.BlockSpec((1, HEADS, HD), lambda b, ix, pt: (b, 0, 0))],
        scratch_shapes=[pltpu.VMEM((2, n_pick * SLC_BLOCK * NSA_CACHE_SLOTS, HD), F32),
                        pltpu.SemaphoreType.DMA((2,))])
    return pl.pallas_call(
        functools.partial(_nsa_sample_slc_win_kernel, n_pick=n_pick, npb=npb),
        grid_spec=grid_spec,
        out_shape=[jax.ShapeDtypeStruct((db, HEADS, HD), F32),
                   jax.ShapeDtypeStruct((db, HEADS, HD), F32)],
        compiler_params=_cparams(1),
        name="nsa_sample_slc_win",
    )(idx, page_table, q, kv_new, win_state, bias_slc, bias_0, bias_w, cache_rows)


def _gate_combine_kernel(g_ref, oc_ref, os_ref, ow_ref, o_ref):
    g = _sigmoid(g_ref[...])
    o_ref[...] = (g[0] * oc_ref[...] + g[1] * os_ref[...] + g[2] * ow_ref[...]).astype(o_ref.dtype)


def nsa_gate_combine(gate_logits, o_cmp, o_slc, o_win):
    r = o_cmp.shape[0]
    return pl.pallas_call(
        _gate_combine_kernel,
        out_shape=jax.ShapeDtypeStruct((r, HD), MXU_DT),
        compiler_params=pltpu.CompilerParams(vmem_limit_bytes=VMEM_LIMIT),
        name="nsa_gate_combine",
    )(gate_logits, o_cmp, o_slc, o_win)


def _mla_prep_kernel(h_ref, gq_ref, gkv_ref, cos_ref, sin_ref, dup_ref,
                     cq_ref, ckv_ref, rows_ref, kr_ref):
    h = h_ref[...]

    def rms(x, g):
        return x * lax.rsqrt(jnp.mean(x * x, axis=-1, keepdims=True) + RMS_EPS) * g

    cq_ref[...] = rms(h[:, :Q_LORA], gq_ref[...]).astype(cq_ref.dtype)
    ckv = rms(h[:, Q_LORA:Q_LORA + KV_LORA], gkv_ref[...])
    ckv_ref[...] = ckv.astype(ckv_ref.dtype)
    kr2 = _dot_hi(h[:, Q_LORA + KV_LORA:Q_LORA + KV_LORA + QK_ROPE], dup_ref[...])
    kr2 = _rope_lanes(kr2, cos_ref[...], sin_ref[...])
    kr_ref[...] = kr2
    rows_ref[:, :KV_LORA] = ckv
    rows_ref[:, KV_LORA:] = kr2[:, :QK_ROPE]


def mla_prep(h, g_q, g_kv, cos_t, sin_t, *, pos_blocks, tm=512):
    m = h.shape[0]
    tm = min(tm, m)
    dup = np.concatenate([np.eye(QK_ROPE, dtype=np.float32)] * 2, axis=1)
    return pl.pallas_call(
        _mla_prep_kernel,
        grid=(m // tm,),
        in_specs=[pl.BlockSpec((tm, h.shape[1]), lambda i: (i, 0)),
                  pl.BlockSpec((1, Q_LORA), lambda i: (0, 0)),
                  pl.BlockSpec((1, KV_LORA), lambda i: (0, 0)),
                  pl.BlockSpec((tm, LANES), lambda i: (i % pos_blocks, 0)),
                  pl.BlockSpec((tm, LANES), lambda i: (i % pos_blocks, 0)),
                  pl.BlockSpec((QK_ROPE, LANES), lambda i: (0, 0))],
        out_specs=[pl.BlockSpec((tm, Q_LORA), lambda i: (i, 0)),
                   pl.BlockSpec((tm, KV_LORA), lambda i: (i, 0)),
                   pl.BlockSpec((tm, MLA_ROW), lambda i: (i, 0)),
                   pl.BlockSpec((tm, LANES), lambda i: (i, 0))],
        out_shape=[jax.ShapeDtypeStruct((m, Q_LORA), MXU_DT),
                   jax.ShapeDtypeStruct((m, KV_LORA), MXU_DT),
                   jax.ShapeDtypeStruct((m, MLA_ROW), F32),
                   jax.ShapeDtypeStruct((m, LANES), F32)],
        compiler_params=_cparams(1),
        name="mla_prep",
    )(h, g_q.reshape(1, -1), g_kv.reshape(1, -1), cos_t, sin_t, jnp.asarray(dup))


def _rope_q_kernel(q_ref, cos_ref, sin_ref, o_ref):
    cos_t = cos_ref[...]
    sin_t = sin_ref[...]
    for g in range(q_ref.shape[1] // LANES):
        cols = slice(g * LANES, (g + 1) * LANES)
        o_ref[:, cols] = _rope_lanes(q_ref[:, cols], cos_t, sin_t)


def rope_q(q, cos_t, sin_t, *, pos_blocks, tm=512):
    m = q.shape[0]
    tm = min(tm, m)
    wr = HEADS * QK_ROPE
    assert (HEADS * HD) % wr == 0
    col_blk = HEADS * HD // wr
    return pl.pallas_call(
        _rope_q_kernel,
        grid=(m // tm,),
        in_specs=[pl.BlockSpec((tm, wr), lambda i: (i, col_blk)),
                  pl.BlockSpec((tm, LANES), lambda i: (i % pos_blocks, 0)),
                  pl.BlockSpec((tm, LANES), lambda i: (i % pos_blocks, 0))],
        out_specs=pl.BlockSpec((tm, wr), lambda i: (i, 0)),
        out_shape=jax.ShapeDtypeStruct((m, wr), F32),
        compiler_params=_cparams(1),
        name="rope_q",
    )(q, cos_t, sin_t)


def _mla_prompt_kernel(qn_ref, qr_ref, kv_ref, kr_ref, o_ref, qc_scr, kc_scr, v_scr, *, seq):
    h = pl.program_id(1)
    blk = MLA_QBLK
    nq = seq // blk
    lane = lax.broadcasted_iota(jnp.int32, (blk, LANES), 1)
    for qb in range(nq):
        rows = slice(qb * blk, (qb + 1) * blk)
        qr = jnp.where((lane // QK_ROPE) == (h % 2), qr_ref[rows, :], 0.0)
        qc_scr[rows, 0:HD] = (qn_ref[rows, :] * MLA_SCALE).astype(MXU_DT)
        qc_scr[rows, HD:2 * HD] = (qr * MLA_SCALE).astype(MXU_DT)
        kc_scr[rows, 0:HD] = kv_ref[rows, 0:HD].astype(MXU_DT)
        kc_scr[rows, HD:2 * HD] = kr_ref[rows, :].astype(MXU_DT)
        v_scr[rows, :] = kv_ref[rows, HD:2 * HD].astype(MXU_DT)

    r = lax.broadcasted_iota(jnp.int32, (blk, blk), 0)
    c = lax.broadcasted_iota(jnp.int32, (blk, blk), 1)
    tri = jnp.where(c <= r, 0.0, NEG_INF)
    for qb in range(nq):
        d0 = qb * blk
        q = qc_scr[d0:d0 + blk, :]
        s_d = _dot_nt(q, kc_scr[d0:d0 + blk, :]) + tri
        m = jnp.max(s_d, axis=-1, keepdims=True)
        if qb > 0:
            s_o = _dot_nt(q, kc_scr[0:d0, :])
            m = jnp.maximum(m, jnp.max(s_o, axis=-1, keepdims=True))
        p_d = jnp.exp(s_d - m)
        l = jnp.sum(p_d, axis=-1, keepdims=True)
        o = _dot(p_d.astype(MXU_DT), v_scr[d0:d0 + blk, :])
        if qb > 0:
            p_o = jnp.exp(s_o - m)
            l = l + jnp.sum(p_o, axis=-1, keepdims=True)
            o = o + _dot(p_o.astype(MXU_DT), v_scr[0:d0, :])
        o_ref[d0:d0 + blk, :] = (o / l).astype(o_ref.dtype)


def mla_prompt_attention(q, qr, kvx, kr2, *, batch, seq):
    assert seq % MLA_QBLK == 0
    return pl.pallas_call(
        functools.partial(_mla_prompt_kernel, seq=seq),
        grid=(batch, HEADS),
        in_specs=[pl.BlockSpec((seq, HD), lambda b, h: (b, h)),
                  pl.BlockSpec((seq, LANES), lambda b, h: (b, h // 2)),
                  pl.BlockSpec((seq, 2 * HD), lambda b, h: (b, h)),
                  pl.BlockSpec((seq, LANES), lambda b, h: (b, 0))],
        out_specs=pl.BlockSpec((seq, HD), lambda b, h: (b, h)),
        out_shape=jax.ShapeDtypeStruct((batch * seq, HEADS * HD), MXU_DT),
        scratch_shapes=[pltpu.VMEM((seq, 2 * HD), MXU_DT), pltpu.VMEM((seq, 2 * HD), MXU_DT),
                        pltpu.VMEM((seq, HD), MXU_DT)],
        compiler_params=_cparams(2),
        name="mla_prompt_attention",
    )(q, qr, kvx, kr2)


def _absorb_q_kernel(q_ref, w_ref, o_ref):
    o_ref[0] = _dot_nt(q_ref[...].astype(MXU_DT), w_ref[...].astype(MXU_DT))


def mla_absorb_q(q, w_kv_b):
    m = q.shape[0]
    return pl.pallas_call(
        _absorb_q_kernel,
        grid=(HEADS,),
        in_specs=[pl.BlockSpec((m, HD), lambda h: (0, h)),
                  pl.BlockSpec((KV_LORA, HD), lambda h: (0, 2 * h))],
        out_specs=pl.BlockSpec((1, m, KV_LORA), lambda h: (h, 0, 0)),
        out_shape=jax.ShapeDtypeStruct((HEADS, m, KV_LORA), F32),
        compiler_params=_cparams(1),
        name="mla_absorb_q",
    )(q, w_kv_b)


def _absorb_o_kernel(o_ref, w_ref, out_ref):
    out_ref[...] = _dot(o_ref[0].astype(MXU_DT), w_ref[...].astype(MXU_DT)).astype(out_ref.dtype)


def mla_absorb_o(o_lat, w_kv_b):
    m = o_lat.shape[1]
    return pl.pallas_call(
        _absorb_o_kernel,
        grid=(HEADS,),
        in_specs=[pl.BlockSpec((1, m, KV_LORA), lambda h: (h, 0, 0)),
                  pl.BlockSpec((KV_LORA, HD), lambda h: (0, 2 * h + 1))],
        out_specs=pl.BlockSpec((m, HD), lambda h: (0, h)),
        out_shape=jax.ShapeDtypeStruct((m, HEADS * HD), MXU_DT),
        compiler_params=_cparams(1),
        name="mla_absorb_o",
    )(o_lat, w_kv_b)


def _mla_decode_kernel(pt_ref, ql_ref, qr_ref, new_ref, cache_ref, o_ref, buf, sem,
                       *, n_chunks, pages):
    b = pl.program_id(0)
    nb = pl.num_programs(0)

    def chunk_copies(bb, ch, slot):
        cps = []
        for p in range(pages):
            page = pt_ref[bb, ch * pages + p]
            cps.append(pltpu.make_async_copy(cache_ref.at[page], buf.at[slot, p], sem.at[slot]))
        return cps

    ahead = RING - 1

    @pl.when(b == 0)
    def _():
        for g0 in range(ahead):
            for cp in chunk_copies(0, g0, g0):
                cp.start()

    ql = (ql_ref[0] * MLA_SCALE).astype(MXU_DT)
    qr = (qr_ref[0] * MLA_SCALE).astype(MXU_DT)

    def chunk_step(ch, carry):
        m, l, acc = carry
        g = b * n_chunks + ch
        slot = g % RING
        nxt = (g + ahead) % RING

        @pl.when(ch + ahead < n_chunks)
        def _():
            for cp in chunk_copies(b, ch + ahead, nxt):
                cp.start()

        @pl.when((ch + ahead >= n_chunks) & (b + 1 < nb))
        def _():
            for cp in chunk_copies(b + 1, ch + ahead - n_chunks, nxt):
                cp.start()

        for cp in chunk_copies(b, ch, slot):
            cp.wait()
        s = jnp.concatenate(
            [_dot(ql, buf[slot, p, 0:KV_LORA, :].astype(MXU_DT))
             + _dot(qr, buf[slot, p, KV_LORA:MLA_ROW, :].astype(MXU_DT))
             for p in range(pages)], axis=1)
        m_new = jnp.maximum(m, jnp.max(s, axis=-1, keepdims=True))
        alpha = jnp.exp(m - m_new)
        pr = jnp.exp(s - m_new)
        l = l * alpha + jnp.sum(pr, axis=-1, keepdims=True)
        pr = pr.astype(MXU_DT)
        acc = acc * alpha
        for p in range(pages):
            acc = acc + _dot_nt(pr[:, p * LANES:(p + 1) * LANES],
                                buf[slot, p, 0:KV_LORA, :].astype(MXU_DT))
        return m_new, l, acc

    m, l, acc = lax.fori_loop(
        0, n_chunks, chunk_step,
        (jnp.full((HEADS, 1), NEG_INF, F32), jnp.zeros((HEADS, 1), F32),
         jnp.zeros((HEADS, KV_LORA), F32)))

    new = new_ref[0].astype(MXU_DT).astype(F32)
    s_new = (jnp.sum(ql.astype(F32) * new[:, :KV_LORA], axis=-1, keepdims=True)
             + jnp.sum(qr.astype(F32) * new[:, KV_LORA:], axis=-1, keepdims=True))
    m_f = jnp.maximum(m, s_new)
    alpha = jnp.exp(m - m_f)
    pn = jnp.exp(s_new - m_f)
    pn_r = pn.astype(MXU_DT).astype(F32)
    o_ref[0] = (acc * alpha + pn_r * new[:, :KV_LORA]) / (l * alpha + pn)


def mla_decode(q_lat, q_rope, rows_new, page_table, cache, *, pages=16):
    db, n_pages = page_table.shape
    assert n_pages % pages == 0
    n_chunks = n_pages // pages
    assert n_chunks >= RING - 1
    grid_spec = pltpu.PrefetchScalarGridSpec(
        num_scalar_prefetch=1,
        grid=(db,),
        in_specs=[pl.BlockSpec((1, HEADS, KV_LORA), lambda b, pt: (b, 0, 0)),
                  pl.BlockSpec((1, HEADS, QK_ROPE), lambda b, pt: (b, 0, 0)),
                  pl.BlockSpec((1, 1, MLA_ROW), lambda b, pt: (b, 0, 0)),
                  pl.BlockSpec(memory_space=pl.ANY)],
        out_specs=pl.BlockSpec((1, HEADS, KV_LORA), lambda b, pt: (b, 0, 0)),
        scratch_shapes=[pltpu.VMEM((RING, pages, MLA_ROW, LANES), F32),
                        pltpu.SemaphoreType.DMA((RING,))])
    return pl.pallas_call(
        functools.partial(_mla_decode_kernel, n_chunks=n_chunks, pages=pages),
        grid_spec=grid_spec,
        out_shape=jax.ShapeDtypeStruct((db, HEADS, KV_LORA), F32),
        compiler_params=_cparams(1),
        name="mla_decode",
    )(page_table, q_lat, q_rope, rows_new, cache)


def _rope_tables(pos):
    inv = ROPE_THETA ** (-jnp.arange(0, QK_ROPE, 2, dtype=F32) / QK_ROPE)
    ang = pos.astype(F32)[:, None] * inv[None, :]
    cos, sin = jnp.cos(ang), jnp.sin(ang)
    return jnp.tile(jnp.concatenate([cos, cos], axis=-1), (1, 2)), \
        jnp.tile(jnp.concatenate([-sin, sin], axis=-1), (1, 2))


def _bias_ids(seq, past_len, w_buf):
    nt = seq // QB
    r = np.arange(QB)[:, None]
    c = np.arange(QB)[None, :]
    prev = _t5_bucket_np(r - c + QB)
    diag = _t5_bucket_np(r - c)
    qi = np.arange(nt)[:, None, None]
    cmp_p = _t5_bucket_np(qi * QB + r[None] - (c[None] * CMP_STRIDE + CMP_BLOCK - 1)).reshape(nt * QB, QB)
    n_tok = past_len // CMP_STRIDE
    cmp_s = _t5_bucket_np(past_len - (np.arange(n_tok) * CMP_STRIDE + CMP_BLOCK - 1)).reshape(-1, LANES)
    off = np.arange(LANES) % SLC_BLOCK
    slc_s = np.stack([_t5_bucket_np(SLC_BLOCK - off), _t5_bucket_np(2 * SLC_BLOCK - off),
                      _t5_bucket_np(np.full(LANES, 3 * SLC_BLOCK)), _t5_bucket_np(np.zeros(LANES))])
    win_s = _t5_bucket_np(w_buf - np.arange(w_buf)).reshape(-1, LANES)
    parts = [prev, diag, cmp_p, cmp_s, slc_s, win_s]
    rows = sum(p.shape[0] for p in parts)
    pad = -rows % 256
    ids = np.concatenate(parts + [np.zeros((pad, LANES), np.int32)], axis=0).astype(np.int32)
    offs = np.cumsum([0] + [p.shape[0] for p in parts])
    return ids, offs


def kernel(x_prompt, x_sample, p_prompt, p_sample, cache_nsa_kv, state_nsa_win, cache_mla_kv, page_table, t5_table, ln_g, ln_b, nsa_w_in, nsa_cmp_a, nsa_cmp_phi, nsa_w_o, mla_w_in, mla_q_norm, mla_kv_norm, mla_w_q_b, mla_w_kv_b, mla_w_o, ffn_w_in, ffn_w_out, ple_w_gate, ple_w_proj):
    batch, seq, d = x_prompt.shape
    db = x_sample.shape[0]
    depth = ln_g.shape[0]
    assert depth == 2 and x_sample.shape[1] == 1
    n_pages = page_table.shape[1]
    page_size = cache_nsa_kv.shape[2]
    assert page_size == LANES
    past_len = n_pages * page_size
    npb = past_len // SLC_BLOCK
    n_pool = cache_nsa_kv.shape[1]
    w_buf = state_nsa_win.shape[2]
    alpha = (2 * depth) ** 0.25
    q_cols = HEADS * HD
    kv_cols = NSA_KV_SLOTS * HD
    mp = batch * seq

    ids, offs = _bias_ids(seq, past_len, w_buf)
    assert offs[2] == 256
    bias = t5_bias_lookup(t5_table, jnp.asarray(ids), n_rel=1)
    sect = lambda k: bias[:, offs[k]:offs[k + 1]]
    bias_near = jnp.concatenate([sect(0), sect(1)], axis=-1)
    bias_c = sect(2).reshape(HEADS, seq // QB, QB, QB).transpose(1, 0, 2, 3)
    bias_cs = sect(3).reshape(HEADS, -1)
    bias_ss = sect(4)
    bias_slc = bias_ss[:, :3].transpose(1, 0, 2)
    bias_0 = bias_ss[:, 3]
    bias_w = sect(5).reshape(HEADS, w_buf)

    def dense_tail(x, mixed_in, w_o, p, i):
        y = matmul_residual(mixed_in, w_o, 0, x, alpha=alpha)
        hmid, mu, rstd = swiglu_ln(y, ln_g[i, 0], ln_b[i, 0], ffn_w_in, i)
        y = matmul_ln_residual(hmid, ffn_w_out, i, y, mu, rstd, ln_g[i, 0], ln_b[i, 0], alpha=alpha)
        return ple_ln(y, ln_g[i, 1], ln_b[i, 1], ple_w_gate, p, ple_w_proj, i)

    pp = p_prompt.reshape(depth, mp, -1)
    ps = p_sample.reshape(depth, db, -1)

    xp = x_prompt.reshape(mp, d)
    xs = x_sample.reshape(db, d)
    w_in0 = nsa_w_in[0]
    hp = matmul(xp, w_in0, tn=512)
    hs = matmul(xs, w_in0, tn=512)

    att_p = nsa_prompt_attention(hp, nsa_cmp_a[0], nsa_cmp_phi[0], bias_c, bias_near,
                                 batch=batch, seq=seq)

    hp3 = hp.reshape(batch, seq, -1)
    nsa_kv_prompt = hp3[:, :, q_cols:q_cols + 4 * HD].reshape(1, batch, seq, 4, HD)
    win_rows_p = hp3[:, :, q_cols + 4 * HD:q_cols + kv_cols].reshape(batch, seq, 2, HD)
    if seq >= w_buf:
        nsa_win_prompt = win_rows_p[:, seq - w_buf:][None]
    else:
        nsa_win_prompt = jnp.pad(win_rows_p, ((0, 0), (w_buf - seq, 0), (0, 0), (0, 0)))[None]

    qs = hs[:, :q_cols].reshape(db, HEADS, HD)
    kvn = hs[:, q_cols:q_cols + kv_cols].reshape(db, 1, kv_cols)
    cache_rows = cache_nsa_kv.reshape(cache_nsa_kv.shape[0] * n_pool, page_size * NSA_CACHE_SLOTS, HD)
    o_cmp_s, imp_s = nsa_sample_compressed(qs, page_table, cache_rows, nsa_cmp_a[0], nsa_cmp_phi[0],
                                           bias_cs, past_len=past_len)
    idx = nsa_sample_select(imp_s.reshape(db, -1), npb=npb)
    win_state = state_nsa_win[0].reshape(db, w_buf * 2, HD)
    o_slc_s, o_win_s = nsa_sample_slc_win(qs, kvn, win_state, idx, page_table, cache_rows,
                                          bias_slc, bias_0, bias_w, npb=npb)
    gate_logits = hs[:, q_cols + kv_cols:].reshape(db, 3, HEADS).transpose(1, 0, 2).reshape(3, db * HEADS, 1)
    att_s = nsa_gate_combine(gate_logits, o_cmp_s.reshape(db * HEADS, HD),
                             o_slc_s.reshape(db * HEADS, HD), o_win_s.reshape(db * HEADS, HD))
    att_s = att_s.reshape(db, q_cols)

    nsa_kv_sample = hs[:, q_cols:q_cols + 4 * HD].reshape(1, db, 1, 4, HD)
    new_win = hs[:, q_cols + 4 * HD:q_cols + kv_cols].reshape(db, 1, 2, HD)
    nsa_win_sample = jnp.concatenate([state_nsa_win[0], new_win], axis=1)[:, -w_buf:][None]

    xp = dense_tail(xp, att_p, nsa_w_o, pp, 0)
    xs = dense_tail(xs, att_s, nsa_w_o, ps, 0)

    wqb = mla_w_q_b[0].reshape(Q_LORA, HEADS, HD + QK_ROPE)
    wqb = jnp.concatenate([wqb[:, :, :HD].reshape(Q_LORA, HEADS * HD),
                           wqb[:, :, HD:].reshape(Q_LORA, HEADS * QK_ROPE)], axis=1)
    w_kv_b = mla_w_kv_b[0]
    cos_p, sin_p = _rope_tables(jnp.arange(seq))
    cos_s, sin_s = _rope_tables(jnp.full((db,), past_len))

    tmp = 512
    hp = matmul(xp, mla_w_in[0], tn=512)
    cq_p, ckv_p, rows_p, kr2_p = mla_prep(hp, mla_q_norm[0], mla_kv_norm[0], cos_p, sin_p,
                                          pos_blocks=seq // tmp, tm=tmp)
    q_p = matmul(cq_p, wqb, tn=512)
    qr_p = rope_q(q_p, cos_p, sin_p, pos_blocks=seq // tmp, tm=tmp)
    kvx_p = matmul(ckv_p, w_kv_b, tn=512, out_dtype=MXU_DT)
    att_p = mla_prompt_attention(q_p, qr_p, kvx_p, kr2_p, batch=batch, seq=seq)

    hs = matmul(xs, mla_w_in[0], tn=512)
    cq_s, _, rows_s, _ = mla_prep(hs, mla_q_norm[0], mla_kv_norm[0], cos_s, sin_s, pos_blocks=1, tm=db)
    q_s = matmul(cq_s, wqb, tn=512)
    qr_s = rope_q(q_s, cos_s, sin_s, pos_blocks=1, tm=db)
    q_lat = mla_absorb_q(q_s, w_kv_b).transpose(1, 0, 2)
    cache_mla = jnp.swapaxes(cache_mla_kv, 2, 3).reshape(cache_mla_kv.shape[0] * n_pool, MLA_ROW, page_size)
    o_lat = mla_decode(q_lat, qr_s.reshape(db, HEADS, QK_ROPE), rows_s.reshape(db, 1, MLA_ROW),
                       page_table, cache_mla)
    att_s = mla_absorb_o(o_lat.transpose(1, 0, 2), w_kv_b)

    xp = dense_tail(xp, att_p, mla_w_o, pp, 1)
    xs = dense_tail(xs, att_s, mla_w_o, ps, 1)

    return (xp.reshape(batch, seq, d), xs.reshape(db, 1, d),
            nsa_kv_prompt, nsa_win_prompt, rows_p.reshape(1, batch, seq, MLA_ROW),
            nsa_kv_sample, nsa_win_sample, rows_s.reshape(1, db, 1, MLA_ROW))
```

```python
import functools
import math

import numpy as np
import jax
import jax.numpy as jnp
from jax import lax
from jax.experimental import pallas as pl
from jax.experimental.pallas import tpu as pltpu

F32 = jnp.float32
MXU_DT = jnp.bfloat16
HI = lax.Precision.HIGHEST

HEADS = 16
HD = 128
CMP_STRIDE = 16
CMP_BLOCK = 32
SLC_BLOCK = 64
N_SEL = 16
WINDOW = 512
NSA_KV_SLOTS = 6
NSA_CACHE_SLOTS = 4
NSA_SCALE = HD ** -0.5
Q_LORA = 768
KV_LORA = 512
QK_ROPE = 64
MLA_ROW = KV_LORA + QK_ROPE
MLA_SCALE = (HD + QK_ROPE) ** -0.5
ROPE_THETA = 10000.0
T5_BUCKETS = 32
T5_MAX_DIST = 128
LN_EPS = 1e-5
RMS_EPS = 1e-6
NEG_INF = -1e30
FORCE_SCORE = 1e4
QB = 128
LANES = 128
HEAD_GROUP = 8
FAR_CHUNK = 512
MLA_QBLK = 512
RING = 3

VMEM_LIMIT = 56 * 1024 * 1024


def _cparams(n_axes):
    return pltpu.CompilerParams(dimension_semantics=("arbitrary",) * n_axes,
                                vmem_limit_bytes=VMEM_LIMIT)


def _dot(a, b):
    return jnp.dot(a, b, preferred_element_type=F32)


def _dot_nt(a, b):
    return lax.dot_general(a, b, (((1,), (1,)), ((), ())), preferred_element_type=F32)


def _dot_hi(a, b):
    return jnp.dot(a, b, precision=HI, preferred_element_type=F32)


def _sigmoid(x):
    return 1.0 / (1.0 + jnp.exp(-x))


def _mm_kernel(x_ref, w_ref, o_ref, xb_ref):
    @pl.when(pl.program_id(1) == 0)
    def _():
        xb_ref[...] = x_ref[...].astype(MXU_DT)
    o_ref[...] = _dot(xb_ref[...], w_ref[...].astype(MXU_DT)).astype(o_ref.dtype)


def matmul(x, w, *, tn, out_dtype=F32, tm=1024):
    m, k = x.shape
    n = w.shape[1]
    tm = min(tm, m)
    return pl.pallas_call(
        _mm_kernel,
        grid=(pl.cdiv(m, tm), pl.cdiv(n, tn)),
        in_specs=[pl.BlockSpec((tm, k), lambda i, j: (i, 0)),
                  pl.BlockSpec((k, tn), lambda i, j: (0, j))],
        out_specs=pl.BlockSpec((tm, tn), lambda i, j: (i, j)),
        out_shape=jax.ShapeDtypeStruct((m, n), out_dtype),
        scratch_shapes=[pltpu.VMEM((tm, k), MXU_DT)],
        compiler_params=_cparams(2),
        name="matmul",
    )(x, w)


def _row_stats(y):
    mu = jnp.mean(y, axis=-1, keepdims=True)
    yc = y - mu
    var = jnp.mean(yc * yc, axis=-1, keepdims=True)
    return mu, lax.rsqrt(var + LN_EPS)


def _layer_norm(y, g, b):
    mu, rstd = _row_stats(y)
    return (y - mu) * rstd * g + b


def _mm_res_kernel(x_ref, w_ref, r_ref, o_ref, *, alpha):
    o_ref[...] = alpha * r_ref[...] + _dot(x_ref[...].astype(MXU_DT), w_ref[...].astype(MXU_DT))


def _mm_lnres_kernel(x_ref, w_ref, y_ref, mu_ref, rs_ref, g_ref, b_ref, o_ref, *, alpha):
    resid = (y_ref[...] - mu_ref[...]) * rs_ref[...] * g_ref[...] + b_ref[...]
    o_ref[...] = alpha * resid + _dot(x_ref[...].astype(MXU_DT), w_ref[...].astype(MXU_DT))


def matmul_ln_residual(x, w, layer, y, mu, rstd, g, b, *, alpha, tn=256, tm=1024):
    m, k = x.shape
    n = w.shape[2]
    tm = min(tm, m)
    return pl.pallas_call(
        functools.partial(_mm_lnres_kernel, alpha=alpha),
        grid=(pl.cdiv(m, tm), pl.cdiv(n, tn)),
        in_specs=[pl.BlockSpec((tm, k), lambda i, j: (i, 0)),
                  pl.BlockSpec((None, k, tn), lambda i, j: (layer, 0, j)),
                  pl.BlockSpec((tm, tn), lambda i, j: (i, j)),
                  pl.BlockSpec((tm, 1), lambda i, j: (i, 0)),
                  pl.BlockSpec((tm, 1), lambda i, j: (i, 0)),
                  pl.BlockSpec((1, tn), lambda i, j: (0, j)),
                  pl.BlockSpec((1, tn), lambda i, j: (0, j))],
        out_specs=pl.BlockSpec((tm, tn), lambda i, j: (i, j)),
        out_shape=jax.ShapeDtypeStruct((m, n), F32),
        compiler_params=_cparams(2),
        name="matmul_ln_residual",
    )(x, w, y, mu, rstd, g.reshape(1, n), b.reshape(1, n))


def matmul_residual(x, w, layer, resid, *, alpha, tn=512, tm=1024):
    m, k = x.shape
    n = w.shape[2]
    tm = min(tm, m)
    return pl.pallas_call(
        functools.partial(_mm_res_kernel, alpha=alpha),
        grid=(pl.cdiv(m, tm), pl.cdiv(n, tn)),
        in_specs=[pl.BlockSpec((tm, k), lambda i, j: (i, 0)),
                  pl.BlockSpec((None, k, tn), lambda i, j: (layer, 0, j)),
                  pl.BlockSpec((tm, tn), lambda i, j: (i, j))],
        out_specs=pl.BlockSpec((tm, tn), lambda i, j: (i, j)),
        out_shape=jax.ShapeDtypeStruct((m, n), F32),
        compiler_params=_cparams(2),
        name="matmul_residual",
    )(x, w, resid)


def _swiglu_ln_kernel(y_ref, g_ref, b_ref, w1_ref, w2_ref, o_ref, mu_ref, rs_ref, xb_ref):
    @pl.when(pl.program_id(1) == 0)
    def _():
        y = y_ref[...]
        mu, rstd = _row_stats(y)
        mu_ref[...] = mu
        rs_ref[...] = rstd
        xb_ref[...] = ((y - mu) * rstd * g_ref[...] + b_ref[...]).astype(MXU_DT)
    xb = xb_ref[...]
    h1 = _dot(xb, w1_ref[...].astype(MXU_DT))
    h2 = _dot(xb, w2_ref[...].astype(MXU_DT))
    o_ref[...] = (h1 * _sigmoid(h1) * h2).astype(o_ref.dtype)


def swiglu_ln(y, g, b, w_in, layer, *, tn=512, tm=1024):
    m, k = y.shape
    d_ff = w_in.shape[2] // 2
    assert d_ff % tn == 0
    nff = d_ff // tn
    tm = min(tm, m)
    return pl.pallas_call(
        _swiglu_ln_kernel,
        grid=(pl.cdiv(m, tm), nff),
        in_specs=[pl.BlockSpec((tm, k), lambda i, j: (i, 0)),
                  pl.BlockSpec((1, k), lambda i, j: (0, 0)),
                  pl.BlockSpec((1, k), lambda i, j: (0, 0)),
                  pl.BlockSpec((None, k, tn), lambda i, j: (layer, 0, j)),
                  pl.BlockSpec((None, k, tn), lambda i, j: (layer, 0, j + nff))],
        out_specs=[pl.BlockSpec((tm, tn), lambda i, j: (i, j)),
                   pl.BlockSpec((tm, 1), lambda i, j: (i, 0)),
                   pl.BlockSpec((tm, 1), lambda i, j: (i, 0))],
        out_shape=[jax.ShapeDtypeStruct((m, d_ff), MXU_DT), jax.ShapeDtypeStruct((m, 1), F32),
                   jax.ShapeDtypeStruct((m, 1), F32)],
        scratch_shapes=[pltpu.VMEM((tm, k), MXU_DT)],
        compiler_params=_cparams(2),
        name="swiglu_ln",
    )(y, g.reshape(1, k), b.reshape(1, k), w_in, w_in)


def _ple_ln_kernel(y_ref, g_ref, b_ref, wg_ref, p_ref, wp_ref, o_ref, xn_ref, xb_ref, *, tn):
    j = pl.program_id(1)

    @pl.when(j == 0)
    def _():
        xn = _layer_norm(y_ref[...], g_ref[...], b_ref[...])
        xb_ref[...] = xn.astype(MXU_DT)
        for t in range(xn_ref.shape[0]):
            xn_ref[t] = xn[:, t * tn:(t + 1) * tn]

    gate = _sigmoid(_dot(xb_ref[...], wg_ref[...].astype(MXU_DT)))
    proj = _dot(p_ref[...].astype(MXU_DT), wp_ref[...].astype(MXU_DT))
    o_ref[...] = xn_ref[j] + gate * proj


def ple_ln(y, g, b, w_gate, p, w_proj, layer, *, tn=512, tm=1024):
    m, k = y.shape
    n = w_gate.shape[2]
    pe = p.shape[2]
    tm = min(tm, m)
    assert n == k and n % tn == 0
    return pl.pallas_call(
        functools.partial(_ple_ln_kernel, tn=tn),
        grid=(pl.cdiv(m, tm), n // tn),
        in_specs=[pl.BlockSpec((tm, k), lambda i, j: (i, 0)),
                  pl.BlockSpec((1, k), lambda i, j: (0, 0)),
                  pl.BlockSpec((1, k), lambda i, j: (0, 0)),
                  pl.BlockSpec((None, k, tn), lambda i, j: (layer, 0, j)),
                  pl.BlockSpec((None, tm, pe), lambda i, j: (layer, i, 0)),
                  pl.BlockSpec((None, pe, tn), lambda i, j: (layer, 0, j))],
        out_specs=pl.BlockSpec((tm, tn), lambda i, j: (i, j)),
        out_shape=jax.ShapeDtypeStruct((m, n), F32),
        scratch_shapes=[pltpu.VMEM((n // tn, tm, tn), F32), pltpu.VMEM((tm, k), MXU_DT)],
        compiler_params=_cparams(2),
        name="ple_ln",
    )(y, g.reshape(1, k), b.reshape(1, k), w_gate, p, w_proj)


def _t5_bucket_np(dist):
    n = np.maximum(np.asarray(dist, np.int64), 0)
    max_exact = T5_BUCKETS // 2
    nf = np.maximum(n, 1).astype(np.float32)
    scaled = (np.log(nf / np.float32(max_exact)) / np.float32(math.log(T5_MAX_DIST / max_exact))
              * np.float32(T5_BUCKETS - max_exact))
    large = np.minimum(max_exact + scaled.astype(np.int32), T5_BUCKETS - 1)
    return np.where(n < max_exact, n, large).astype(np.int32)


def _bias_kernel(tab_ref, ids_ref, o_ref, *, n_rel):
    ids = ids_ref[...]
    relative = pl.program_id(0) < n_rel
    for h in range(HEADS):
        acc = jnp.zeros(ids.shape, F32)
        for bkt in range(T5_BUCKETS):
            acc = jnp.where(ids == bkt, tab_ref[bkt, h], acc)
        o_ref[h] = acc - jnp.where(relative, tab_ref[T5_BUCKETS - 1, h], 0.0)


def t5_bias_lookup(table, ids, *, n_rel, tr=256):
    r = ids.shape[0]
    assert r % tr == 0
    return pl.pallas_call(
        functools.partial(_bias_kernel, n_rel=n_rel),
        grid=(r // tr,),
        in_specs=[pl.BlockSpec(memory_space=pltpu.SMEM),
                  pl.BlockSpec((tr, LANES), lambda i: (i, 0))],
        out_specs=pl.BlockSpec((HEADS, tr, LANES), lambda i: (0, i, 0)),
        out_shape=jax.ShapeDtypeStruct((HEADS, r, LANES), F32),
        compiler_params=_cparams(1),
        name="t5_bias_lookup",
    )(table, ids)


def _softmax_rows(s):
    m = jnp.max(s, axis=-1, keepdims=True)
    e = jnp.exp(s - m)
    l = jnp.sum(e, axis=-1, keepdims=True)
    return jnp.where(m > 0.5 * NEG_INF, e / l, 0.0)


def _lane_tiles(x):
    return [x[..., t * LANES:(t + 1) * LANES] for t in range(x.shape[-1] // LANES)]


def _fold(op, tiles):
    out = tiles[0]
    for t in tiles[1:]:
        out = op(out, t)
    return out


def _rope_lanes(x, cos_t, sin_t):
    lane = lax.broadcasted_iota(jnp.int32, x.shape, 1)
    first_half = (lane % QK_ROPE) < (QK_ROPE // 2)
    rot = jnp.where(first_half, pltpu.roll(x, LANES - QK_ROPE // 2, 1), pltpu.roll(x, QK_ROPE // 2, 1))
    return x * cos_t + rot * sin_t


def _nsa_prompt_kernel(q_ref, g_ref, kcr_ref, vcr_ref, ksr_ref, vsr_ref, kwr_ref, vwr_ref,
                       a_ref, phi_ref, bc_ref, bn_ref, amat_ref, emat_ref, o_ref,
                       kc_scr, vc_scr, ks_scr, vs_scr, kw_scr, vw_scr, q_scr, selk_scr, a_scr,
                       mx_scr, mb_scr, l_scr, acc_scr, sn_scr, sf_scr, oc_scr, os_scr, ow_scr,
                       *, nt, n_blk):
    i = pl.program_id(1)
    seq = nt * QB
    n_cmp = nt * (QB // CMP_STRIDE) - 1
    wt = WINDOW // QB
    hg = HEAD_GROUP
    gm = hg * QB
    fc = FAR_CHUNK
    tpc = fc // QB

    @pl.when(i == 0)
    def _():
        ks_scr[0:QB, :] = jnp.zeros((QB, HD), MXU_DT)
        vs_scr[0:QB, :] = jnp.zeros((QB, HD), MXU_DT)
        kw_scr[0:wt * QB, :] = jnp.zeros((wt * QB, HD), MXU_DT)
        vw_scr[0:wt * QB, :] = jnp.zeros((wt * QB, HD), MXU_DT)
        ks_scr[QB:QB + seq, :] = ksr_ref[...].astype(MXU_DT)
        vs_scr[QB:QB + seq, :] = vsr_ref[...].astype(MXU_DT)
        kw_scr[wt * QB:wt * QB + seq, :] = kwr_ref[...].astype(MXU_DT)
        vw_scr[wt * QB:wt * QB + seq, :] = vwr_ref[...].astype(MXU_DT)
        for which, (rows_ref, dst) in enumerate(((kcr_ref, kc_scr), (vcr_ref, vc_scr))):
            ng = nt * QB // CMP_STRIDE
            first = jnp.zeros((ng, HD), F32)
            second = jnp.zeros((ng, HD), F32)
            for j in range(CMP_STRIDE):
                xj = rows_ref[pl.ds(j, ng, stride=CMP_STRIDE), :]
                first = first + xj * a_ref[which, j:j + 1, :]
                second = second + xj * a_ref[which, CMP_STRIDE + j:CMP_STRIDE + j + 1, :]
            pre = first + pltpu.roll(second, ng - 1, 0)
            dst[...] = _dot_hi(pre, phi_ref[which]).astype(MXU_DT)

    for h in range(HEADS):
        q_scr[h * QB:(h + 1) * QB, :] = (q_ref[:, h * HD:(h + 1) * HD] * NSA_SCALE).astype(MXU_DT)

    row = lax.broadcasted_iota(jnp.int32, (QB, QB), 0)
    lane = lax.broadcasted_iota(jnp.int32, (QB, QB), 1)
    qpos = i * QB + row
    tri = jnp.where(lane <= row, 0.0, NEG_INF)
    cmp_valid = (qpos >= lane * CMP_STRIDE + (CMP_BLOCK - 1)) & (lane < n_cmp)
    cmp_mask = jnp.where(cmp_valid, 0.0, NEG_INF)

    s = _dot_nt(q_scr[...], kc_scr[...]).reshape(HEADS, QB, QB) + bc_ref[0] + cmp_mask[None]
    p = _softmax_rows(s)
    imp = jnp.sum(p, axis=0)
    oc_scr[...] = _dot(p.reshape(HEADS * QB, QB).astype(MXU_DT), vc_scr[...])

    slc_imp = _dot_hi(imp, amat_ref[...])
    cur = qpos // SLC_BLOCK
    forced = (lane == 0) | (lane == cur) | (lane == cur - 1)
    score = jnp.where(lane > cur, -1.0, jnp.where(forced, FORCE_SCORE, slc_imp))
    score = jnp.where(lane < n_blk, score, -2.0)
    rank = jnp.zeros((QB, QB), F32)
    for b2 in range(n_blk):
        col = score[:, b2:b2 + 1]
        beats = (col > score) | ((col == score) & (lane > b2))
        rank = rank + jnp.where(beats, 1.0, 0.0)
    sel = jnp.where((rank < min(N_SEL, n_blk)) & (lane < n_blk), 1.0, 0.0)
    sel_keys = _dot(sel.astype(MXU_DT), emat_ref[...])
    for kt in range(nt):
        tile = jnp.where(sel_keys[:, kt * QB:(kt + 1) * QB] > 0.5, 0.0, NEG_INF)
        selk_scr[kt] = tile
        a_scr[kt // tpc, :, (kt % tpc) * QB:(kt % tpc + 1) * QB] = jnp.where(kt < i - 1, tile, NEG_INF)
    prev_ok = jnp.where(i > 0, 0.0, NEG_INF)
    near_mask = jnp.concatenate([selk_scr[jnp.maximum(i - 1, 0)] + prev_ok, selk_scr[i] + tri], axis=1)

    zeros = jnp.zeros((QB, QB), F32)
    band = jnp.where(lane >= row, 0.0, NEG_INF)

    def tile_ok(t):
        return jnp.where(i - wt + t >= 0, 0.0, NEG_INF)

    win_far_mask = jnp.concatenate(
        [band + tile_ok(0)] + [zeros + tile_ok(t) for t in range(1, wt - 1)], axis=1)
    win_near_mask = jnp.concatenate([zeros + tile_ok(wt - 1), tri], axis=1)
    n_far = (i + tpc - 2) // tpc
    n0 = pl.multiple_of(i * QB, QB)

    def group(g, carry):
        r0 = pl.multiple_of(g * gm, gm)
        qg = q_scr[pl.ds(r0, gm), :]
        bn = bn_ref[pl.ds(g * hg, hg)]

        mx_scr[...] = jnp.full((hg, QB, QB), NEG_INF, F32)

        def far_max(c, carry2):
            k0 = pl.multiple_of(QB + c * fc, QB)
            s = _dot_nt(qg, ks_scr[pl.ds(k0, fc), :]).reshape(hg, QB, fc) + a_scr[c][None]
            sf_scr[c] = s
            mx_scr[...] = jnp.maximum(mx_scr[...], _fold(jnp.maximum, _lane_tiles(s)))
            return carry2

        lax.fori_loop(0, n_far, far_max, 0)
        sn = _dot_nt(qg, ks_scr[pl.ds(n0, 2 * QB), :]).reshape(hg, QB, 2 * QB) + bn + near_mask[None]
        sn_scr[...] = sn
        mfold = jnp.maximum(mx_scr[...], _fold(jnp.maximum, _lane_tiles(sn)))
        mb_scr[...] = jnp.broadcast_to(jnp.max(mfold, axis=-1, keepdims=True), (hg, QB, QB))
        l_scr[...] = jnp.zeros((hg, QB, QB), F32)
        acc_scr[...] = jnp.zeros((gm, HD), F32)

        def far_pv(c, carry2):
            k0 = pl.multiple_of(QB + c * fc, QB)
            mbv = mb_scr[...]
            ps = [jnp.exp(t - mbv) for t in _lane_tiles(sf_scr[c])]
            l_scr[...] += _fold(jnp.add, ps)
            pm = jnp.concatenate(ps, axis=-1).reshape(gm, fc).astype(MXU_DT)
            acc_scr[...] += _dot(pm, vs_scr[pl.ds(k0, fc), :])
            return carry2

        lax.fori_loop(0, n_far, far_pv, 0)
        mbv = mb_scr[...]
        pn = [jnp.exp(t - mbv) for t in _lane_tiles(sn_scr[...])]
        l = l_scr[...] + _fold(jnp.add, pn)
        pm = jnp.concatenate(pn, axis=-1).reshape(gm, 2 * QB).astype(MXU_DT)
        acc = acc_scr[...] + _dot(pm, vs_scr[pl.ds(n0, 2 * QB), :])
        o_slc = acc.reshape(hg, QB, HD) / jnp.sum(l, axis=-1, keepdims=True)
        os_scr[pl.ds(r0, gm), :] = o_slc.reshape(gm, HD)

        nw = (wt + 1) * QB
        nf = (wt - 1) * QB
        sw = _dot_nt(qg, kw_scr[pl.ds(n0, nw), :]).reshape(hg, QB, nw)
        s_far = sw[:, :, :nf] + win_far_mask[None]
        s_near = sw[:, :, nf:] + bn + win_near_mask[None]
        m = jnp.maximum(jnp.max(s_far, axis=-1, keepdims=True), jnp.max(s_near, axis=-1, keepdims=True))
        e_far = jnp.exp(s_far - m)
        e_near = jnp.exp(s_near - m)
        l = jnp.sum(e_far, axis=-1, keepdims=True) + jnp.sum(e_near, axis=-1, keepdims=True)
        pm = jnp.concatenate([e_far, e_near], axis=-1).reshape(gm, nw).astype(MXU_DT)
        o_win = _dot(pm, vw_scr[pl.ds(n0, nw), :]).reshape(hg, QB, HD) / l
        ow_scr[pl.ds(r0, gm), :] = o_win.reshape(gm, HD)
        return carry

    lax.fori_loop(0, HEADS // hg, group, 0)

    gates = _sigmoid(g_ref[...])
    for h in range(HEADS):
        rows = slice(h * QB, (h + 1) * QB)
        o = (gates[:, h:h + 1] * oc_scr[rows, :] + gates[:, HEADS + h:HEADS + h + 1] * os_scr[rows, :]
             + gates[:, 2 * HEADS + h:2 * HEADS + h + 1] * ow_scr[rows, :])
        o_ref[:, h * HD:(h + 1) * HD] = o.astype(o_ref.dtype)


def nsa_prompt_attention(h, cmp_a, cmp_phi, bias_c, bias_near, *, batch, seq):
    nt = seq // QB
    n_blk = seq // SLC_BLOCK
    wt = WINDOW // QB
    assert seq // CMP_STRIDE == QB and n_blk <= QB and WINDOW % QB == 0 and wt >= 2
    assert FAR_CHUNK % QB == 0 and seq % FAR_CHUNK == 0 and HEADS % HEAD_GROUP == 0
    q_cols = HEADS * HD
    kv_blk0 = q_cols // HD
    gate_blk = kv_blk0 + NSA_KV_SLOTS
    c = np.arange(QB)[:, None]
    b = np.arange(QB)[None, :]
    ratio = SLC_BLOCK // CMP_STRIDE
    amat = (((c >= ratio * b) & (c <= ratio * b + ratio - 1)).astype(np.float32)
            + ((c + 1 >= ratio * b) & (c + 1 <= ratio * b + ratio - 1)).astype(np.float32))
    amat = amat * (b < n_blk)
    emat = (np.arange(seq)[None, :] // SLC_BLOCK == np.arange(QB)[:, None]).astype(np.float32)

    kv_spec = lambda s: pl.BlockSpec((seq, HD), lambda bb, i, s=s: (bb, kv_blk0 + s))
    full = lambda shape: pl.BlockSpec(shape, lambda bb, i: (0,) * len(shape))
    keys_scr = lambda pad_tiles: pltpu.VMEM(((nt + pad_tiles) * QB, HD), MXU_DT)
    heads_scr = lambda dt: pltpu.VMEM((HEADS * QB, HD), dt)
    group_scr = lambda: pltpu.VMEM((HEAD_GROUP, QB, QB), F32)
    return pl.pallas_call(
        functools.partial(_nsa_prompt_kernel, nt=nt, n_blk=n_blk),
        grid=(batch, nt),
        in_specs=[pl.BlockSpec((QB, q_cols), lambda bb, i: (bb * nt + i, 0)),
                  pl.BlockSpec((QB, LANES), lambda bb, i: (bb * nt + i, gate_blk))]
                 + [kv_spec(s) for s in range(NSA_KV_SLOTS)]
                 + [full((2, CMP_BLOCK, HD)), full((2, HD, HD)),
                    pl.BlockSpec((1, HEADS, QB, QB), lambda bb, i: (i, 0, 0, 0)),
                    full((HEADS, QB, 2 * QB)), full((QB, QB)), full((QB, seq))],
        out_specs=pl.BlockSpec((QB, q_cols), lambda bb, i: (bb * nt + i, 0)),
        out_shape=jax.ShapeDtypeStruct((batch * seq, q_cols), MXU_DT),
        scratch_shapes=[pltpu.VMEM((QB, HD), MXU_DT), pltpu.VMEM((QB, HD), MXU_DT),
                        keys_scr(1), keys_scr(1), keys_scr(wt), keys_scr(wt),
                        heads_scr(MXU_DT), pltpu.VMEM((nt, QB, QB), F32),
                        pltpu.VMEM((seq // FAR_CHUNK, QB, FAR_CHUNK), F32),
                        group_scr(), group_scr(), group_scr(),
                        pltpu.VMEM((HEAD_GROUP * QB, HD), F32),
                        pltpu.VMEM((HEAD_GROUP, QB, 2 * QB), F32),
                        pltpu.VMEM((seq // FAR_CHUNK, HEAD_GROUP, QB, FAR_CHUNK), F32),
                        heads_scr(F32), heads_scr(F32), heads_scr(F32)],
        compiler_params=_cparams(2),
        name="nsa_prompt_attention",
    )(h, h, h, h, h, h, h, h, cmp_a, cmp_phi, bias_c, bias_near,
      jnp.asarray(amat), jnp.asarray(emat, MXU_DT))


def _nsa_sample_cmp_kernel(pt_ref, q_ref, w_ref, phi_ref, bias_ref, cache_ref, oc_ref, imp_ref,
                           buf, sem, part_scr, kc_scr, vc_scr, *, n_chunks, pages, past_len):
    b = pl.program_id(0)
    nb = pl.num_programs(0)
    prow = LANES * NSA_CACHE_SLOTS
    rows = pages * prow
    grow = CMP_STRIDE * NSA_CACHE_SLOTS
    groups = pages * LANES // CMP_STRIDE
    sub = 8
    nv = grow // sub

    def chunk_copies(bb, ch, slot):
        cps = []
        for p in range(pages):
            page = pt_ref[bb, ch * pages + p]
            cps.append(pltpu.make_async_copy(cache_ref.at[page],
                                             buf.at[slot, pl.ds(p * prow, prow), :], sem.at[slot]))
        if ch + 1 < n_chunks:
            page = pt_ref[bb, (ch + 1) * pages]
            cps.append(pltpu.make_async_copy(cache_ref.at[page, pl.ds(0, grow), :],
                                             buf.at[slot, pl.ds(rows, grow), :], sem.at[slot]))
        return cps

    ahead = RING - 1

    @pl.when(b == 0)
    def _():
        for g0 in range(ahead):
            for cp in chunk_copies(0, g0, g0):
                cp.start()

    for ch in range(n_chunks):
        g = b * n_chunks + ch
        slot = g % RING
        nxt = (g + ahead) % RING
        if ch + ahead < n_chunks:
            for cp in chunk_copies(b, ch + ahead, nxt):
                cp.start()
        else:
            @pl.when(b + 1 < nb)
            def _():
                for cp in chunk_copies(b + 1, ch + ahead - n_chunks, nxt):
                    cp.start()
        for cp in chunk_copies(b, ch, slot):
            cp.wait()
        if ch == n_chunks - 1:
            buf[slot, rows:rows + grow, :] = jnp.zeros((grow, HD), F32)
        x0 = buf[slot, 0:rows, :].reshape(groups, nv, sub, HD)
        x1 = buf[slot, grow:rows + grow, :].reshape(groups, nv, sub, HD)
        part = x0[:, 0] * w_ref[0, 0] + x1[:, 0] * w_ref[1, 0]
        for t in range(1, nv):
            part = part + x0[:, t] * w_ref[0, t] + x1[:, t] * w_ref[1, t]
        part_scr[...] = part.reshape(groups * sub, HD)
        for kv, dst in enumerate((kc_scr, vc_scr)):
            pre = (part_scr[pl.ds(kv, groups, stride=sub), :]
                   + part_scr[pl.ds(NSA_CACHE_SLOTS + kv, groups, stride=sub), :])
            dst[ch * groups:(ch + 1) * groups, :] = _dot_hi(pre, phi_ref[kv]).astype(MXU_DT)

    n_tok = n_chunks * groups
    q = (q_ref[0] * NSA_SCALE).astype(MXU_DT)
    tok = lax.broadcasted_iota(jnp.int32, (HEADS, n_tok), 1)
    valid = tok * CMP_STRIDE + (CMP_BLOCK - 1) <= past_len
    s = _dot_nt(q, kc_scr[...]) + bias_ref[...] + jnp.where(valid, 0.0, NEG_INF)
    p = _softmax_rows(s)
    oc_ref[0] = _dot(p.astype(MXU_DT), vc_scr[...])
    imp_ref[0] = jnp.sum(p, axis=0, keepdims=True)


def nsa_sample_compressed(q, page_table, cache_rows, cmp_a, cmp_phi, bias_c, *, past_len, pages=16):
    db, n_pages = page_table.shape
    assert n_pages % pages == 0
    n_chunks = n_pages // pages
    assert n_chunks >= RING - 1
    rows = (pages * LANES + CMP_STRIDE) * NSA_CACHE_SLOTS
    n_tok = n_pages * LANES // CMP_STRIDE
    nv = CMP_STRIDE // 2
    w = cmp_a.reshape(2, 2, nv, 2, HD).transpose(1, 2, 3, 0, 4)
    w = jnp.pad(w, ((0, 0), (0, 0), (0, 0), (0, NSA_CACHE_SLOTS - 2), (0, 0)))
    w = w.reshape(2, nv, 2 * NSA_CACHE_SLOTS, HD)
    grid_spec = pltpu.PrefetchScalarGridSpec(
        num_scalar_prefetch=1,
        grid=(db,),
        in_specs=[pl.BlockSpec((1, HEADS, HD), lambda b, pt: (b, 0, 0)),
                  pl.BlockSpec((2, nv, 2 * NSA_CACHE_SLOTS, HD), lambda b, pt: (0, 0, 0, 0)),
                  pl.BlockSpec((2, HD, HD), lambda b, pt: (0, 0, 0)),
                  pl.BlockSpec((HEADS, n_tok), lambda b, pt: (0, 0)),
                  pl.BlockSpec(memory_space=pl.ANY)],
        out_specs=[pl.BlockSpec((1, HEADS, HD), lambda b, pt: (b, 0, 0)),
                   pl.BlockSpec((1, 1, n_tok), lambda b, pt: (b, 0, 0))],
        scratch_shapes=[pltpu.VMEM((RING, rows, HD), F32),
                        pltpu.SemaphoreType.DMA((RING,)),
                        pltpu.VMEM((pages * LANES // CMP_STRIDE * 8, HD), F32),
                        pltpu.VMEM((n_tok, HD), MXU_DT), pltpu.VMEM((n_tok, HD), MXU_DT)])
    return pl.pallas_call(
        functools.partial(_nsa_sample_cmp_kernel, n_chunks=n_chunks, pages=pages, past_len=past_len),
        grid_spec=grid_spec,
        out_shape=[jax.ShapeDtypeStruct((db, HEADS, HD), F32),
                   jax.ShapeDtypeStruct((db, 1, n_tok), F32)],
        compiler_params=_cparams(1),
        name="nsa_sample_compressed",
    )(page_table, q, w, cmp_phi, bias_c, cache_rows)


def _select_kernel(imp_ref, amat_ref, idx_ref, *, npb, n_pick):
    slc_imp = _dot_hi(imp_ref[...], amat_ref[...])
    lane = lax.broadcasted_iota(jnp.int32, slc_imp.shape, 1).astype(F32)
    forced = (lane == 0.0) | (lane == float(npb - 1))
    score = jnp.where(forced, FORCE_SCORE, slc_imp)
    score = jnp.where(lane < float(npb), score, -2.0)
    out_lane = lax.broadcasted_iota(jnp.int32, idx_ref.shape, 1)
    out = jnp.zeros(idx_ref.shape, F32)
    for t in range(n_pick):
        best = jnp.max(score, axis=-1, keepdims=True)
        first = jnp.min(jnp.where(score == best, lane, 1e9), axis=-1, keepdims=True)
        out = jnp.where(out_lane == t, first, out)
        score = jnp.where(lane == first, -3.0, score)
    idx_ref[...] = out.astype(jnp.int32)


def nsa_sample_select(imp, *, npb):
    db, n_tok = imp.shape
    assert npb >= N_SEL
    nbp = -(-npb // LANES) * LANES
    c = np.arange(n_tok)[:, None]
    b = np.arange(nbp)[None, :]
    ratio = SLC_BLOCK // CMP_STRIDE
    amat = (((c >= ratio * b) & (c <= ratio * b + ratio - 1)).astype(np.float32)
            + ((c + 1 >= ratio * b) & (c + 1 <= ratio * b + ratio - 1)).astype(np.float32))
    amat = amat * (b < npb)
    return pl.pallas_call(
        functools.partial(_select_kernel, npb=npb, n_pick=N_SEL - 1),
        out_shape=jax.ShapeDtypeStruct((db, LANES), jnp.int32),
        compiler_params=pltpu.CompilerParams(vmem_limit_bytes=VMEM_LIMIT),
        name="nsa_sample_select",
    )(imp, jnp.asarray(amat))


def _nsa_sample_slc_win_kernel(idx_ref, pt_ref, q_ref, kvn_ref, win_ref, bsl_ref, b0_ref, bw_ref,
                               cache_ref, osl_ref, owin_ref, kbuf, sem, *, n_pick, npb):
    b = pl.program_id(0)
    nb = pl.num_programs(0)
    bpp = LANES // SLC_BLOCK
    brow = SLC_BLOCK * NSA_CACHE_SLOTS

    def block_copies(bb, slot):
        cps = []
        for t in range(n_pick):
            blk = idx_ref[bb, t]
            page = pt_ref[bb, blk // bpp]
            off = pl.multiple_of((blk % bpp) * brow, brow)
            cps.append(pltpu.make_async_copy(cache_ref.at[page, pl.ds(off, brow), :],
                                             kbuf.at[slot, pl.ds(t * brow, brow), :], sem.at[slot]))
        return cps

    slot = b % 2

    @pl.when(b == 0)
    def _():
        for cp in block_copies(0, 0):
            cp.start()

    @pl.when(b + 1 < nb)
    def _():
        for cp in block_copies(b + 1, 1 - slot):
            cp.start()

    for cp in block_copies(b, slot):
        cp.wait()

    n_keys = (n_pick + 1) * SLC_BLOCK
    kvn = kvn_ref[0]
    rowi = lax.broadcasted_iota(jnp.int32, (SLC_BLOCK, HD), 0)

    def gathered(cache_slot):
        parts = [kbuf[slot, pl.ds(t * brow + cache_slot, SLC_BLOCK, stride=NSA_CACHE_SLOTS), :]
                 for t in range(n_pick)]
        new_row = kvn[:, cache_slot * HD:(cache_slot + 1) * HD]
        parts.append(jnp.where(rowi == 0, new_row, 0.0))
        return jnp.concatenate(parts, axis=0).astype(MXU_DT)

    q = (q_ref[0] * NSA_SCALE).astype(MXU_DT)
    keys = gathered(2)
    vals = gathered(3)
    lane = lax.broadcasted_iota(jnp.int32, (HEADS, LANES), 1)

    def slot_bias(t):
        if t == n_pick:
            return b0_ref[...]
        blk = idx_ref[b, t]
        return jnp.where(blk == npb - 1, bsl_ref[0], jnp.where(blk == npb - 2, bsl_ref[1], bsl_ref[2]))

    tiles = []
    for u in range(n_keys // LANES):
        tiles.append(jnp.where(lane < SLC_BLOCK, slot_bias(2 * u), slot_bias(2 * u + 1)))
    bias = jnp.concatenate(tiles, axis=1)
    col = lax.broadcasted_iota(jnp.int32, (HEADS, n_keys), 1)
    s = _dot_nt(q, keys) + bias + jnp.where(col <= n_pick * SLC_BLOCK, 0.0, NEG_INF)
    p = _softmax_rows(s)
    osl_ref[0] = _dot(p.astype(MXU_DT), vals)

    w_buf = win_ref.shape[1] // 2
    wk = win_ref[0, pl.ds(0, w_buf, stride=2), :].astype(MXU_DT)
    wv = win_ref[0, pl.ds(1, w_buf, stride=2), :].astype(MXU_DT)
    sw = _dot_nt(q, wk) + bw_ref[...]
    new_k = kvn[:, 4 * HD:5 * HD].astype(MXU_DT).astype(F32)
    new_v = kvn[:, 5 * HD:6 * HD].astype(MXU_DT).astype(F32)
    s_new = jnp.sum(q.astype(F32) * new_k, axis=-1, keepdims=True) + b0_ref[:, 0:1]
    m = jnp.maximum(jnp.max(sw, axis=-1, keepdims=True), s_new)
    pw = jnp.exp(sw - m)
    pn = jnp.exp(s_new - m)
    l = jnp.sum(pw, axis=-1, keepdims=True) + pn
    pn_r = pn.astype(MXU_DT).astype(F32)
    owin_ref[0] = (_dot(pw.astype(MXU_DT), wv) + pn_r * new_v) / l


def nsa_sample_slc_win(q, kv_new, win_state, idx, page_table, cache_rows, bias_slc, bias_0, bias_w,
                       *, npb):
    db = q.shape[0]
    n_pick = N_SEL - 1
    w_buf = win_state.shape[1] // 2
    assert w_buf <= WINDOW and ((n_pick + 1) * SLC_BLOCK) % LANES == 0
    n_keys = (n_pick + 1) * SLC_BLOCK
    grid_spec = pltpu.PrefetchScalarGridSpec(
        num_scalar_prefetch=2,
        grid=(db,),
        in_specs=[pl.BlockSpec((1, HEADS, HD), lambda b, ix, pt: (b, 0, 0)),
                  pl.BlockSpec((1, 1, NSA_KV_SLOTS * HD), lambda b, ix, pt: (b, 0, 0)),
                  pl.BlockSpec((1, 2 * w_buf, HD), lambda b, ix, pt: (b, 0, 0)),
                  pl.BlockSpec((3, HEADS, LANES), lambda b, ix, pt: (0, 0, 0)),
                  pl.BlockSpec((HEADS, LANES), lambda b, ix, pt: (0, 0)),
                  pl.BlockSpec((HEADS, w_buf), lambda b, ix, pt: (0, 0)),
                  pl.BlockSpec(memory_space=pl.ANY)],
        out_specs=[pl.BlockSpec((1, HEADS, HD), lambda b, ix, pt: (b, 0, 0)),
                   pl.BlockSpec((1, HEADS, HD), lambda b, ix, pt: (b, 0, 0))],
        scratch_shapes=[pltpu.VMEM((2, n_pick * SLC_BLOCK * NSA_CACHE_SLOTS, HD), F32),
                        pltpu.SemaphoreType.DMA((2,))])
    return pl.pallas_call(
        functools.partial(_nsa_sample_slc_win_kernel, n_pick=n_pick, npb=npb),
        grid_spec=grid_spec,
        out_shape=[jax.ShapeDtypeStruct((db, HEADS, HD), F32),
                   jax.ShapeDtypeStruct((db, HEADS, HD), F32)],
        compiler_params=_cparams(1),
        name="nsa_sample_slc_win",
    )(idx, page_table, q, kv_new, win_state, bias_slc, bias_0, bias_w, cache_rows)


def _gate_combine_kernel(g_ref, oc_ref, os_ref, ow_ref, o_ref):
    g = _sigmoid(g_ref[...])
    o_ref[...] = (g[0] * oc_ref[...] + g[1] * os_ref[...] + g[2] * ow_ref[...]).astype(o_ref.dtype)


def nsa_gate_combine(gate_logits, o_cmp, o_slc, o_win):
    r = o_cmp.shape[0]
    return pl.pallas_call(
        _gate_combine_kernel,
        out_shape=jax.ShapeDtypeStruct((r, HD), MXU_DT),
        compiler_params=pltpu.CompilerParams(vmem_limit_bytes=VMEM_LIMIT),
        name="nsa_gate_combine",
    )(gate_logits, o_cmp, o_slc, o_win)


def _mla_prep_kernel(h_ref, gq_ref, gkv_ref, cos_ref, sin_ref, dup_ref,
                     cq_ref, ckv_ref, rows_ref, kr_ref):
    h = h_ref[...]

    def rms(x, g):
        return x * lax.rsqrt(jnp.mean(x * x, axis=-1, keepdims=True) + RMS_EPS) * g

    cq_ref[...] = rms(h[:, :Q_LORA], gq_ref[...]).astype(cq_ref.dtype)
    ckv = rms(h[:, Q_LORA:Q_LORA + KV_LORA], gkv_ref[...])
    ckv_ref[...] = ckv.astype(ckv_ref.dtype)
    kr2 = _dot_hi(h[:, Q_LORA + KV_LORA:Q_LORA + KV_LORA + QK_ROPE], dup_ref[...])
    kr2 = _rope_lanes(kr2, cos_ref[...], sin_ref[...])
    kr_ref[...] = kr2
    rows_ref[:, :KV_LORA] = ckv
    rows_ref[:, KV_LORA:] = kr2[:, :QK_ROPE]


def mla_prep(h, g_q, g_kv, cos_t, sin_t, *, pos_blocks, tm=512):
    m = h.shape[0]
    tm = min(tm, m)
    dup = np.concatenate([np.eye(QK_ROPE, dtype=np.float32)] * 2, axis=1)
    return pl.pallas_call(
        _mla_prep_kernel,
        grid=(m // tm,),
        in_specs=[pl.BlockSpec((tm, h.shape[1]), lambda i: (i, 0)),
                  pl.BlockSpec((1, Q_LORA), lambda i: (0, 0)),
                  pl.BlockSpec((1, KV_LORA), lambda i: (0, 0)),
                  pl.BlockSpec((tm, LANES), lambda i: (i % pos_blocks, 0)),
                  pl.BlockSpec((tm, LANES), lambda i: (i % pos_blocks, 0)),
                  pl.BlockSpec((QK_ROPE, LANES), lambda i: (0, 0))],
        out_specs=[pl.BlockSpec((tm, Q_LORA), lambda i: (i, 0)),
                   pl.BlockSpec((tm, KV_LORA), lambda i: (i, 0)),
                   pl.BlockSpec((tm, MLA_ROW), lambda i: (i, 0)),
                   pl.BlockSpec((tm, LANES), lambda i: (i, 0))],
        out_shape=[jax.ShapeDtypeStruct((m, Q_LORA), MXU_DT),
                   jax.ShapeDtypeStruct((m, KV_LORA), MXU_DT),
                   jax.ShapeDtypeStruct((m, MLA_ROW), F32),
                   jax.ShapeDtypeStruct((m, LANES), F32)],
        compiler_params=_cparams(1),
        name="mla_prep",
    )(h, g_q.reshape(1, -1), g_kv.reshape(1, -1), cos_t, sin_t, jnp.asarray(dup))


def _rope_q_kernel(q_ref, cos_ref, sin_ref, o_ref):
    cos_t = cos_ref[...]
    sin_t = sin_ref[...]
    for g in range(q_ref.shape[1] // LANES):
        cols = slice(g * LANES, (g + 1) * LANES)
        o_ref[:, cols] = _rope_lanes(q_ref[:, cols], cos_t, sin_t)


def rope_q(q, cos_t, sin_t, *, pos_blocks, tm=512):
    m = q.shape[0]
    tm = min(tm, m)
    wr = HEADS * QK_ROPE
    assert (HEADS * HD) % wr == 0
    col_blk = HEADS * HD // wr
    return pl.pallas_call(
        _rope_q_kernel,
        grid=(m // tm,),
        in_specs=[pl.BlockSpec((tm, wr), lambda i: (i, col_blk)),
                  pl.BlockSpec((tm, LANES), lambda i: (i % pos_blocks, 0)),
                  pl.BlockSpec((tm, LANES), lambda i: (i % pos_blocks, 0))],
        out_specs=pl.BlockSpec((tm, wr), lambda i: (i, 0)),
        out_shape=jax.ShapeDtypeStruct((m, wr), F32),
        compiler_params=_cparams(1),
        name="rope_q",
    )(q, cos_t, sin_t)


def _mla_prompt_kernel(qn_ref, qr_ref, kv_ref, kr_ref, o_ref, qc_scr, kc_scr, v_scr, *, seq):
    h = pl.program_id(1)
    blk = MLA_QBLK
    nq = seq // blk
    lane = lax.broadcasted_iota(jnp.int32, (blk, LANES), 1)
    for qb in range(nq):
        rows = slice(qb * blk, (qb + 1) * blk)
        qr = jnp.where((lane // QK_ROPE) == (h % 2), qr_ref[rows, :], 0.0)
        qc_scr[rows, 0:HD] = (qn_ref[rows, :] * MLA_SCALE).astype(MXU_DT)
        qc_scr[rows, HD:2 * HD] = (qr * MLA_SCALE).astype(MXU_DT)
        kc_scr[rows, 0:HD] = kv_ref[rows, 0:HD].astype(MXU_DT)
        kc_scr[rows, HD:2 * HD] = kr_ref[rows, :].astype(MXU_DT)
        v_scr[rows, :] = kv_ref[rows, HD:2 * HD].astype(MXU_DT)

    r = lax.broadcasted_iota(jnp.int32, (blk, blk), 0)
    c = lax.broadcasted_iota(jnp.int32, (blk, blk), 1)
    tri = jnp.where(c <= r, 0.0, NEG_INF)
    for qb in range(nq):
        d0 = qb * blk
        q = qc_scr[d0:d0 + blk, :]
        s_d = _dot_nt(q, kc_scr[d0:d0 + blk, :]) + tri
        m = jnp.max(s_d, axis=-1, keepdims=True)
        if qb > 0:
            s_o = _dot_nt(q, kc_scr[0:d0, :])
            m = jnp.maximum(m, jnp.max(s_o, axis=-1, keepdims=True))
        p_d = jnp.exp(s_d - m)
        l = jnp.sum(p_d, axis=-1, keepdims=True)
        o = _dot(p_d.astype(MXU_DT), v_scr[d0:d0 + blk, :])
        if qb > 0:
            p_o = jnp.exp(s_o - m)
            l = l + jnp.sum(p_o, axis=-1, keepdims=True)
            o = o + _dot(p_o.astype(MXU_DT), v_scr[0:d0, :])
        o_ref[d0:d0 + blk, :] = (o / l).astype(o_ref.dtype)


def mla_prompt_attention(q, qr, kvx, kr2, *, batch, seq):
    assert seq % MLA_QBLK == 0
    return pl.pallas_call(
        functools.partial(_mla_prompt_kernel, seq=seq),
        grid=(batch, HEADS),
        in_specs=[pl.BlockSpec((seq, HD), lambda b, h: (b, h)),
                  pl.BlockSpec((seq, LANES), lambda b, h: (b, h // 2)),
                  pl.BlockSpec((seq, 2 * HD), lambda b, h: (b, h)),
                  pl.BlockSpec((seq, LANES), lambda b, h: (b, 0))],
        out_specs=pl.BlockSpec((seq, HD), lambda b, h: (b, h)),
        out_shape=jax.ShapeDtypeStruct((batch * seq, HEADS * HD), MXU_DT),
        scratch_shapes=[pltpu.VMEM((seq, 2 * HD), MXU_DT), pltpu.VMEM((seq, 2 * HD), MXU_DT),
                        pltpu.VMEM((seq, HD), MXU_DT)],
        compiler_params=_cparams(2),
        name="mla_prompt_attention",
    )(q, qr, kvx, kr2)


def _absorb_q_kernel(q_ref, w_ref, o_ref):
    o_ref[0] = _dot_nt(q_ref[...].astype(MXU_DT), w_ref[...].astype(MXU_DT))


def mla_absorb_q(q, w_kv_b):
    m = q.shape[0]
    return pl.pallas_call(
        _absorb_q_kernel,
        grid=(HEADS,),
        in_specs=[pl.BlockSpec((m, HD), lambda h: (0, h)),
                  pl.BlockSpec((KV_LORA, HD), lambda h: (0, 2 * h))],
        out_specs=pl.BlockSpec((1, m, KV_LORA), lambda h: (h, 0, 0)),
        out_shape=jax.ShapeDtypeStruct((HEADS, m, KV_LORA), F32),
        compiler_params=_cparams(1),
        name="mla_absorb_q",
    )(q, w_kv_b)


def _absorb_o_kernel(o_ref, w_ref, out_ref):
    out_ref[...] = _dot(o_ref[0].astype(MXU_DT), w_ref[...].astype(MXU_DT)).astype(out_ref.dtype)


def mla_absorb_o(o_lat, w_kv_b):
    m = o_lat.shape[1]
    return pl.pallas_call(
        _absorb_o_kernel,
        grid=(HEADS,),
        in_specs=[pl.BlockSpec((1, m, KV_LORA), lambda h: (h, 0, 0)),
                  pl.BlockSpec((KV_LORA, HD), lambda h: (0, 2 * h + 1))],
        out_specs=pl.BlockSpec((m, HD), lambda h: (0, h)),
        out_shape=jax.ShapeDtypeStruct((m, HEADS * HD), MXU_DT),
        compiler_params=_cparams(1),
        name="mla_absorb_o",
    )(o_lat, w_kv_b)


def _mla_decode_kernel(pt_ref, ql_ref, qr_ref, new_ref, cache_ref, o_ref, buf, sem,
                       *, n_chunks, pages):
    b = pl.program_id(0)
    nb = pl.num_programs(0)

    def chunk_copies(bb, ch, slot):
        cps = []
        for p in range(pages):
            page = pt_ref[bb, ch * pages + p]
            cps.append(pltpu.make_async_copy(cache_ref.at[page], buf.at[slot, p], sem.at[slot]))
        return cps

    ahead = RING - 1

    @pl.when(b == 0)
    def _():
        for g0 in range(ahead):
            for cp in chunk_copies(0, g0, g0):
                cp.start()

    ql = (ql_ref[0] * MLA_SCALE).astype(MXU_DT)
    qr = (qr_ref[0] * MLA_SCALE).astype(MXU_DT)

    def chunk_step(ch, carry):
        m, l, acc = carry
        g = b * n_chunks + ch
        slot = g % RING
        nxt = (g + ahead) % RING

        @pl.when(ch + ahead < n_chunks)
        def _():
            for cp in chunk_copies(b, ch + ahead, nxt):
                cp.start()

        @pl.when((ch + ahead >= n_chunks) & (b + 1 < nb))
        def _():
            for cp in chunk_copies(b + 1, ch + ahead - n_chunks, nxt):
                cp.start()

        for cp in chunk_copies(b, ch, slot):
            cp.wait()
        s = jnp.concatenate(
            [_dot(ql, buf[slot, p, 0:KV_LORA, :].astype(MXU_DT))
             + _dot(qr, buf[slot, p, KV_LORA:MLA_ROW, :].astype(MXU_DT))
             for p in range(pages)], axis=1)
        m_new = jnp.maximum(m, jnp.max(s, axis=-1, keepdims=True))
        alpha = jnp.exp(m - m_new)
        pr = jnp.exp(s - m_new)
        l = l * alpha + jnp.sum(pr, axis=-1, keepdims=True)
        pr = pr.astype(MXU_DT)
        acc = acc * alpha
        for p in range(pages):
            acc = acc + _dot_nt(pr[:, p * LANES:(p + 1) * LANES],
                                buf[slot, p, 0:KV_LORA, :].astype(MXU_DT))
        return m_new, l, acc

    m, l, acc = lax.fori_loop(
        0, n_chunks, chunk_step,
        (jnp.full((HEADS, 1), NEG_INF, F32), jnp.zeros((HEADS, 1), F32),
         jnp.zeros((HEADS, KV_LORA), F32)))

    new = new_ref[0].astype(MXU_DT).astype(F32)
    s_new = (jnp.sum(ql.astype(F32) * new[:, :KV_LORA], axis=-1, keepdims=True)
             + jnp.sum(qr.astype(F32) * new[:, KV_LORA:], axis=-1, keepdims=True))
    m_f = jnp.maximum(m, s_new)
    alpha = jnp.exp(m - m_f)
    pn = jnp.exp(s_new - m_f)
    pn_r = pn.astype(MXU_DT).astype(F32)
    o_ref[0] = (acc * alpha + pn_r * new[:, :KV_LORA]) / (l * alpha + pn)


def mla_decode(q_lat, q_rope, rows_new, page_table, cache, *, pages=16):
    db, n_pages = page_table.shape
    assert n_pages % pages == 0
    n_chunks = n_pages // pages
    assert n_chunks >= RING - 1
    grid_spec = pltpu.PrefetchScalarGridSpec(
        num_scalar_prefetch=1,
        grid=(db,),
        in_specs=[pl.BlockSpec((1, HEADS, KV_LORA), lambda b, pt: (b, 0, 0)),
                  pl.BlockSpec((1, HEADS, QK_ROPE), lambda b, pt: (b, 0, 0)),
                  pl.BlockSpec((1, 1, MLA_ROW), lambda b, pt: (b, 0, 0)),
                  pl.BlockSpec(memory_space=pl.ANY)],
        out_specs=pl.BlockSpec((1, HEADS, KV_LORA), lambda b, pt: (b, 0, 0)),
        scratch_shapes=[pltpu.VMEM((RING, pages, MLA_ROW, LANES), F32),
                        pltpu.SemaphoreType.DMA((RING,))])
    return pl.pallas_call(
        functools.partial(_mla_decode_kernel, n_chunks=n_chunks, pages=pages),
        grid_spec=grid_spec,
        out_shape=jax.ShapeDtypeStruct((db, HEADS, KV_LORA), F32),
        compiler_params=_cparams(1),
        name="mla_decode",
    )(page_table, q_lat, q_rope, rows_new, cache)


def _rope_tables(pos):
    inv = ROPE_THETA ** (-jnp.arange(0, QK_ROPE, 2, dtype=F32) / QK_ROPE)
    ang = pos.astype(F32)[:, None] * inv[None, :]
    cos, sin = jnp.cos(ang), jnp.sin(ang)
    return jnp.tile(jnp.concatenate([cos, cos], axis=-1), (1, 2)), \
        jnp.tile(jnp.concatenate([-sin, sin], axis=-1), (1, 2))


def _bias_ids(seq, past_len, w_buf):
    nt = seq // QB
    r = np.arange(QB)[:, None]
    c = np.arange(QB)[None, :]
    prev = _t5_bucket_np(r - c + QB)
    diag = _t5_bucket_np(r - c)
    qi = np.arange(nt)[:, None, None]
    cmp_p = _t5_bucket_np(qi * QB + r[None] - (c[None] * CMP_STRIDE + CMP_BLOCK - 1)).reshape(nt * QB, QB)
    n_tok = past_len // CMP_STRIDE
    cmp_s = _t5_bucket_np(past_len - (np.arange(n_tok) * CMP_STRIDE + CMP_BLOCK - 1)).reshape(-1, LANES)
    off = np.arange(LANES) % SLC_BLOCK
    slc_s = np.stack([_t5_bucket_np(SLC_BLOCK - off), _t5_bucket_np(2 * SLC_BLOCK - off),
                      _t5_bucket_np(np.full(LANES, 3 * SLC_BLOCK)), _t5_bucket_np(np.zeros(LANES))])
    win_s = _t5_bucket_np(w_buf - np.arange(w_buf)).reshape(-1, LANES)
    parts = [prev, diag, cmp_p, cmp_s, slc_s, win_s]
    rows = sum(p.shape[0] for p in parts)
    pad = -rows % 256
    ids = np.concatenate(parts + [np.zeros((pad, LANES), np.int32)], axis=0).astype(np.int32)
    offs = np.cumsum([0] + [p.shape[0] for p in parts])
    return ids, offs


def kernel(x_prompt, x_sample, p_prompt, p_sample, cache_nsa_kv, state_nsa_win, cache_mla_kv, page_table, t5_table, ln_g, ln_b, nsa_w_in, nsa_cmp_a, nsa_cmp_phi, nsa_w_o, mla_w_in, mla_q_norm, mla_kv_norm, mla_w_q_b, mla_w_kv_b, mla_w_o, ffn_w_in, ffn_w_out, ple_w_gate, ple_w_proj):
    batch, seq, d = x_prompt.shape
    db = x_sample.shape[0]
    depth = ln_g.shape[0]
    assert depth == 2 and x_sample.shape[1] == 1
    n_pages = page_table.shape[1]
    page_size = cache_nsa_kv.shape[2]
    assert page_size == LANES
    past_len = n_pages * page_size
    npb = past_len // SLC_BLOCK
    n_pool = cache_nsa_kv.shape[1]
    w_buf = state_nsa_win.shape[2]
    alpha = (2 * depth) ** 0.25
    q_cols = HEADS * HD
    kv_cols = NSA_KV_SLOTS * HD
    mp = batch * seq

    ids, offs = _bias_ids(seq, past_len, w_buf)
    assert offs[2] == 256
    bias = t5_bias_lookup(t5_table, jnp.asarray(ids), n_rel=1)
    sect = lambda k: bias[:, offs[k]:offs[k + 1]]
    bias_near = jnp.concatenate([sect(0), sect(1)], axis=-1)
    bias_c = sect(2).reshape(HEADS, seq // QB, QB, QB).transpose(1, 0, 2, 3)
    bias_cs = sect(3).reshape(HEADS, -1)
    bias_ss = sect(4)
    bias_slc = bias_ss[:, :3].transpose(1, 0, 2)
    bias_0 = bias_ss[:, 3]
    bias_w = sect(5).reshape(HEADS, w_buf)

    def dense_tail(x, mixed_in, w_o, p, i):
        y = matmul_residual(mixed_in, w_o, 0, x, alpha=alpha)
        hmid, mu, rstd = swiglu_ln(y, ln_g[i, 0], ln_b[i, 0], ffn_w_in, i)
        y = matmul_ln_residual(hmid, ffn_w_out, i, y, mu, rstd, ln_g[i, 0], ln_b[i, 0], alpha=alpha)
        return ple_ln(y, ln_g[i, 1], ln_b[i, 1], ple_w_gate, p, ple_w_proj, i)

    pp = p_prompt.reshape(depth, mp, -1)
    ps = p_sample.reshape(depth, db, -1)

    xp = x_prompt.reshape(mp, d)
    xs = x_sample.reshape(db, d)
    w_in0 = nsa_w_in[0]
    hp = matmul(xp, w_in0, tn=512)
    hs = matmul(xs, w_in0, tn=512)

    att_p = nsa_prompt_attention(hp, nsa_cmp_a[0], nsa_cmp_phi[0], bias_c, bias_near,
                                 batch=batch, seq=seq)

    hp3 = hp.reshape(batch, seq, -1)
    nsa_kv_prompt = hp3[:, :, q_cols:q_cols + 4 * HD].reshape(1, batch, seq, 4, HD)
    win_rows_p = hp3[:, :, q_cols + 4 * HD:q_cols + kv_cols].reshape(batch, seq, 2, HD)
    if seq >= w_buf:
        nsa_win_prompt = win_rows_p[:, seq - w_buf:][None]
    else:
        nsa_win_prompt = jnp.pad(win_rows_p, ((0, 0), (w_buf - seq, 0), (0, 0), (0, 0)))[None]

    qs = hs[:, :q_cols].reshape(db, HEADS, HD)
    kvn = hs[:, q_cols:q_cols + kv_cols].reshape(db, 1, kv_cols)
    cache_rows = cache_nsa_kv.reshape(cache_nsa_kv.shape[0] * n_pool, page_size * NSA_CACHE_SLOTS, HD)
    o_cmp_s, imp_s = nsa_sample_compressed(qs, page_table, cache_rows, nsa_cmp_a[0], nsa_cmp_phi[0],
                                           bias_cs, past_len=past_len)
    idx = nsa_sample_select(imp_s.reshape(db, -1), npb=npb)
    win_state = state_nsa_win[0].reshape(db, w_buf * 2, HD)
    o_slc_s, o_win_s = nsa_sample_slc_win(qs, kvn, win_state, idx, page_table, cache_rows,
                                          bias_slc, bias_0, bias_w, npb=npb)
    gate_logits = hs[:, q_cols + kv_cols:].reshape(db, 3, HEADS).transpose(1, 0, 2).reshape(3, db * HEADS, 1)
    att_s = nsa_gate_combine(gate_logits, o_cmp_s.reshape(db * HEADS, HD),
                             o_slc_s.reshape(db * HEADS, HD), o_win_s.reshape(db * HEADS, HD))
    att_s = att_s.reshape(db, q_cols)

    nsa_kv_sample = hs[:, q_cols:q_cols + 4 * HD].reshape(1, db, 1, 4, HD)
    new_win = hs[:, q_cols + 4 * HD:q_cols + kv_cols].reshape(db, 1, 2, HD)
    nsa_win_sample = jnp.concatenate([state_nsa_win[0], new_win], axis=1)[:, -w_buf:][None]

    xp = dense_tail(xp, att_p, nsa_w_o, pp, 0)
    xs = dense_tail(xs, att_s, nsa_w_o, ps, 0)

    wqb = mla_w_q_b[0].reshape(Q_LORA, HEADS, HD + QK_ROPE)
    wqb = jnp.concatenate([wqb[:, :, :HD].reshape(Q_LORA, HEADS * HD),
                           wqb[:, :, HD:].reshape(Q_LORA, HEADS * QK_ROPE)], axis=1)
    w_kv_b = mla_w_kv_b[0]
    cos_p, sin_p = _rope_tables(jnp.arange(seq))
    cos_s, sin_s = _rope_tables(jnp.full((db,), past_len))

    tmp = 512
    hp = matmul(xp, mla_w_in[0], tn=512)
    cq_p, ckv_p, rows_p, kr2_p = mla_prep(hp, mla_q_norm[0], mla_kv_norm[0], cos_p, sin_p,
                                          pos_blocks=seq // tmp, tm=tmp)
    q_p = matmul(cq_p, wqb, tn=512)
    qr_p = rope_q(q_p, cos_p, sin_p, pos_blocks=seq // tmp, tm=tmp)
    kvx_p = matmul(ckv_p, w_kv_b, tn=512, out_dtype=MXU_DT)
    att_p = mla_prompt_attention(q_p, qr_p, kvx_p, kr2_p, batch=batch, seq=seq)

    hs = matmul(xs, mla_w_in[0], tn=512)
    cq_s, _, rows_s, _ = mla_prep(hs, mla_q_norm[0], mla_kv_norm[0], cos_s, sin_s, pos_blocks=1, tm=db)
    q_s = matmul(cq_s, wqb, tn=512)
    qr_s = rope_q(q_s, cos_s, sin_s, pos_blocks=1, tm=db)
    q_lat = mla_absorb_q(q_s, w_kv_b).transpose(1, 0, 2)
    cache_mla = jnp.swapaxes(cache_mla_kv, 2, 3).reshape(cache_mla_kv.shape[0] * n_pool, MLA_ROW, page_size)
    o_lat = mla_decode(q_lat, qr_s.reshape(db, HEADS, QK_ROPE), rows_s.reshape(db, 1, MLA_ROW),
                       page_table, cache_mla)
    att_s = mla_absorb_o(o_lat.transpose(1, 0, 2), w_kv_b)

    xp = dense_tail(xp, att_p, mla_w_o, pp, 1)
    xs = dense_tail(xs, att_s, mla_w_o, ps, 1)

    return (xp.reshape(batch, seq, d), xs.reshape(db, 1, d),
            nsa_kv_prompt, nsa_win_prompt, rows_p.reshape(1, batch, seq, MLA_ROW),
            nsa_kv_sample, nsa_win_sample, rows_s.reshape(1, db, 1, MLA_ROW))
```

```python
import functools
import math

import numpy as np
import jax
import jax.numpy as jnp
from jax import lax
from jax.experimental import pallas as pl
from jax.experimental.pallas import tpu as pltpu

F32 = jnp.float32
MXU_DT = jnp.bfloat16
HI = lax.Precision.HIGHEST

HEADS = 16
HD = 128
CMP_STRIDE = 16
CMP_BLOCK = 32
SLC_BLOCK = 64
N_SEL = 16
WINDOW = 512
NSA_KV_SLOTS = 6
NSA_CACHE_SLOTS = 4
NSA_SCALE = HD ** -0.5
Q_LORA = 768
KV_LORA = 512
QK_ROPE = 64
MLA_ROW = KV_LORA + QK_ROPE
MLA_SCALE = (HD + QK_ROPE) ** -0.5
ROPE_THETA = 10000.0
T5_BUCKETS = 32
T5_MAX_DIST = 128
LN_EPS = 1e-5
RMS_EPS = 1e-6
NEG_INF = -1e30
FORCE_SCORE = 1e4
QB = 128
LANES = 128
HEAD_GROUP = 8
FAR_CHUNK = 512
MLA_QBLK = 512
RING = 3

VMEM_LIMIT = 56 * 1024 * 1024


def _cparams(n_axes):
    return pltpu.CompilerParams(dimension_semantics=("arbitrary",) * n_axes,
                                vmem_limit_bytes=VMEM_LIMIT)


def _dot(a, b):
    return jnp.dot(a, b, preferred_element_type=F32)


def _dot_nt(a, b):
    return lax.dot_general(a, b, (((1,), (1,)), ((), ())), preferred_element_type=F32)


def _dot_hi(a, b):
    return jnp.dot(a, b, precision=HI, preferred_element_type=F32)


def _sigmoid(x):
    return 1.0 / (1.0 + jnp.exp(-x))


def _mm_kernel(x_ref, w_ref, o_ref, xb_ref):
    @pl.when(pl.program_id(1) == 0)
    def _():
        xb_ref[...] = x_ref[...].astype(MXU_DT)
    o_ref[...] = _dot(xb_ref[...], w_ref[...].astype(MXU_DT)).astype(o_ref.dtype)


def matmul(x, w, *, tn, out_dtype=F32, tm=1024):
    m, k = x.shape
    n = w.shape[1]
    tm = min(tm, m)
    return pl.pallas_call(
        _mm_kernel,
        grid=(pl.cdiv(m, tm), pl.cdiv(n, tn)),
        in_specs=[pl.BlockSpec((tm, k), lambda i, j: (i, 0)),
                  pl.BlockSpec((k, tn), lambda i, j: (0, j))],
        out_specs=pl.BlockSpec((tm, tn), lambda i, j: (i, j)),
        out_shape=jax.ShapeDtypeStruct((m, n), out_dtype),
        scratch_shapes=[pltpu.VMEM((tm, k), MXU_DT)],
        compiler_params=_cparams(2),
        name="matmul",
    )(x, w)


def _row_stats(y):
    mu = jnp.mean(y, axis=-1, keepdims=True)
    yc = y - mu
    var = jnp.mean(yc * yc, axis=-1, keepdims=True)
    return mu, lax.rsqrt(var + LN_EPS)


def _layer_norm(y, g, b):
    mu, rstd = _row_stats(y)
    return (y - mu) * rstd * g + b


def _mm_res_kernel(x_ref, w_ref, r_ref, o_ref, *, alpha):
    o_ref[...] = alpha * r_ref[...] + _dot(x_ref[...].astype(MXU_DT), w_ref[...].astype(MXU_DT))


def _mm_lnres_kernel(x_ref, w_ref, y_ref, mu_ref, rs_ref, g_ref, b_ref, o_ref, *, alpha):
    resid = (y_ref[...] - mu_ref[...]) * rs_ref[...] * g_ref[...] + b_ref[...]
    o_ref[...] = alpha * resid + _dot(x_ref[...].astype(MXU_DT), w_ref[...].astype(MXU_DT))


def matmul_ln_residual(x, w, layer, y, mu, rstd, g, b, *, alpha, tn=256, tm=1024):
    m, k = x.shape
    n = w.shape[2]
    tm = min(tm, m)
    return pl.pallas_call(
        functools.partial(_mm_lnres_kernel, alpha=alpha),
        grid=(pl.cdiv(m, tm), pl.cdiv(n, tn)),
        in_specs=[pl.BlockSpec((tm, k), lambda i, j: (i, 0)),
                  pl.BlockSpec((None, k, tn), lambda i, j: (layer, 0, j)),
                  pl.BlockSpec((tm, tn), lambda i, j: (i, j)),
                  pl.BlockSpec((tm, 1), lambda i, j: (i, 0)),
                  pl.BlockSpec((tm, 1), lambda i, j: (i, 0)),
                  pl.BlockSpec((1, tn), lambda i, j: (0, j)),
                  pl.BlockSpec((1, tn), lambda i, j: (0, j))],
        out_specs=pl.BlockSpec((tm, tn), lambda i, j: (i, j)),
        out_shape=jax.ShapeDtypeStruct((m, n), F32),
        compiler_params=_cparams(2),
        name="matmul_ln_residual",
    )(x, w, y, mu, rstd, g.reshape(1, n), b.reshape(1, n))


def matmul_residual(x, w, layer, resid, *, alpha, tn=512, tm=1024):
    m, k = x.shape
    n = w.shape[2]
    tm = min(tm, m)
    return pl.pallas_call(
        functools.partial(_mm_res_kernel, alpha=alpha),
        grid=(pl.cdiv(m, tm), pl.cdiv(n, tn)),
        in_specs=[pl.BlockSpec((tm, k), lambda i, j: (i, 0)),
                  pl.BlockSpec((None, k, tn), lambda i, j: (layer, 0, j)),
                  pl.BlockSpec((tm, tn), lambda i, j: (i, j))],
        out_specs=pl.BlockSpec((tm, tn), lambda i, j: (i, j)),
        out_shape=jax.ShapeDtypeStruct((m, n), F32),
        compiler_params=_cparams(2),
        name="matmul_residual",
    )(x, w, resid)


def _swiglu_ln_kernel(y_ref, g_ref, b_ref, w1_ref, w2_ref, o_ref, mu_ref, rs_ref, xb_ref):
    @pl.when(pl.program_id(1) == 0)
    def _():
        y = y_ref[...]
        mu, rstd = _row_stats(y)
        mu_ref[...] = mu
        rs_ref[...] = rstd
        xb_ref[...] = ((y - mu) * rstd * g_ref[...] + b_ref[...]).astype(MXU_DT)
    xb = xb_ref[...]
    h1 = _dot(xb, w1_ref[...].astype(MXU_DT))
    h2 = _dot(xb, w2_ref[...].astype(MXU_DT))
    o_ref[...] = (h1 * _sigmoid(h1) * h2).astype(o_ref.dtype)


def swiglu_ln(y, g, b, w_in, layer, *, tn=512, tm=1024):
    m, k = y.shape
    d_ff = w_in.shape[2] // 2
    assert d_ff % tn == 0
    nff = d_ff // tn
    tm = min(tm, m)
    return pl.pallas_call(
        _swiglu_ln_kernel,
        grid=(pl.cdiv(m, tm), nff),
        in_specs=[pl.BlockSpec((tm, k), lambda i, j: (i, 0)),
                  pl.BlockSpec((1, k), lambda i, j: (0, 0)),
                  pl.BlockSpec((1, k), lambda i, j: (0, 0)),
                  pl.BlockSpec((None, k, tn), lambda i, j: (layer, 0, j)),
                  pl.BlockSpec((None, k, tn), lambda i, j: (layer, 0, j + nff))],
        out_specs=[pl.BlockSpec((tm, tn), lambda i, j: (i, j)),
                   pl.BlockSpec((tm, 1), lambda i, j: (i, 0)),
                   pl.BlockSpec((tm, 1), lambda i, j: (i, 0))],
        out_shape=[jax.ShapeDtypeStruct((m, d_ff), MXU_DT), jax.ShapeDtypeStruct((m, 1), F32),
                   jax.ShapeDtypeStruct((m, 1), F32)],
        scratch_shapes=[pltpu.VMEM((tm, k), MXU_DT)],
        compiler_params=_cparams(2),
        name="swiglu_ln",
    )(y, g.reshape(1, k), b.reshape(1, k), w_in, w_in)


def _ple_ln_kernel(y_ref, g_ref, b_ref, wg_ref, p_ref, wp_ref, o_ref, xn_ref, xb_ref, *, tn):
    j = pl.program_id(1)

    @pl.when(j == 0)
    def _():
        xn = _layer_norm(y_ref[...], g_ref[...], b_ref[...])
        xb_ref[...] = xn.astype(MXU_DT)
        for t in range(xn_ref.shape[0]):
            xn_ref[t] = xn[:, t * tn:(t + 1) * tn]

    gate = _sigmoid(_dot(xb_ref[...], wg_ref[...].astype(MXU_DT)))
    proj = _dot(p_ref[...].astype(MXU_DT), wp_ref[...].astype(MXU_DT))
    o_ref[...] = xn_ref[j] + gate * proj


def ple_ln(y, g, b, w_gate, p, w_proj, layer, *, tn=512, tm=1024):
    m, k = y.shape
    n = w_gate.shape[2]
    pe = p.shape[2]
    tm = min(tm, m)
    assert n == k and n % tn == 0
    return pl.pallas_call(
        functools.partial(_ple_ln_kernel, tn=tn),
        grid=(pl.cdiv(m, tm), n // tn),
        in_specs=[pl.BlockSpec((tm, k), lambda i, j: (i, 0)),
                  pl.BlockSpec((1, k), lambda i, j: (0, 0)),
                  pl.BlockSpec((1, k), lambda i, j: (0, 0)),
                  pl.BlockSpec((None, k, tn), lambda i, j: (layer, 0, j)),
                  pl.BlockSpec((None, tm, pe), lambda i, j: (layer, i, 0)),
                  pl.BlockSpec((None, pe, tn), lambda i, j: (layer, 0, j))],
        out_specs=pl.BlockSpec((tm, tn), lambda i, j: (i, j)),
        out_shape=jax.ShapeDtypeStruct((m, n), F32),
        scratch_shapes=[pltpu.VMEM((n // tn, tm, tn), F32), pltpu.VMEM((tm, k), MXU_DT)],
        compiler_params=_cparams(2),
        name="ple_ln",
    )(y, g.reshape(1, k), b.reshape(1, k), w_gate, p, w_proj)


def _t5_bucket_np(dist):
    n = np.maximum(np.asarray(dist, np.int64), 0)
    max_exact = T5_BUCKETS // 2
    nf = np.maximum(n, 1).astype(np.float32)
    scaled = (np.log(nf / np.float32(max_exact)) / np.float32(math.log(T5_MAX_DIST / max_exact))
              * np.float32(T5_BUCKETS - max_exact))
    large = np.minimum(max_exact + scaled.astype(np.int32), T5_BUCKETS - 1)
    return np.where(n < max_exact, n, large).astype(np.int32)


def _bias_kernel(tab_ref, ids_ref, o_ref, *, n_rel):
    ids = ids_ref[...]
    relative = pl.program_id(0) < n_rel
    for h in range(HEADS):
        acc = jnp.zeros(ids.shape, F32)
        for bkt in range(T5_BUCKETS):
            acc = jnp.where(ids == bkt, tab_ref[bkt, h], acc)
        o_ref[h] = acc - jnp.where(relative, tab_ref[T5_BUCKETS - 1, h], 0.0)


def t5_bias_lookup(table, ids, *, n_rel, tr=256):
    r = ids.shape[0]
    assert r % tr == 0
    return pl.pallas_call(
        functools.partial(_bias_kernel, n_rel=n_rel),
        grid=(r // tr,),
        in_specs=[pl.BlockSpec(memory_space=pltpu.SMEM),
                  pl.BlockSpec((tr, LANES), lambda i: (i, 0))],
        out_specs=pl.BlockSpec((HEADS, tr, LANES), lambda i: (0, i, 0)),
        out_shape=jax.ShapeDtypeStruct((HEADS, r, LANES), F32),
        compiler_params=_cparams(1),
        name="t5_bias_lookup",
    )(table, ids)


def _softmax_rows(s):
    m = jnp.max(s, axis=-1, keepdims=True)
    e = jnp.exp(s - m)
    l = jnp.sum(e, axis=-1, keepdims=True)
    return jnp.where(m > 0.5 * NEG_INF, e / l, 0.0)


def _lane_tiles(x):
    return [x[..., t * LANES:(t + 1) * LANES] for t in range(x.shape[-1] // LANES)]


def _fold(op, tiles):
    out = tiles[0]
    for t in tiles[1:]:
        out = op(out, t)
    return out


def _rope_lanes(x, cos_t, sin_t):
    lane = lax.broadcasted_iota(jnp.int32, x.shape, 1)
    first_half = (lane % QK_ROPE) < (QK_ROPE // 2)
    rot = jnp.where(first_half, pltpu.roll(x, LANES - QK_ROPE // 2, 1), pltpu.roll(x, QK_ROPE // 2, 1))
    return x * cos_t + rot * sin_t


def _nsa_prompt_kernel(q_ref, g_ref, kcr_ref, vcr_ref, ksr_ref, vsr_ref, kwr_ref, vwr_ref,
                       a_ref, phi_ref, bc_ref, bn_ref, amat_ref, emat_ref, o_ref,
                       kc_scr, vc_scr, ks_scr, vs_scr, kw_scr, vw_scr, q_scr, selk_scr, a_scr,
                       mx_scr, mb_scr, l_scr, acc_scr, sn_scr, sf_scr, oc_scr, os_scr, ow_scr,
                       *, nt, n_blk):
    i = pl.program_id(1)
    seq = nt * QB
    n_cmp = nt * (QB // CMP_STRIDE) - 1
    wt = WINDOW // QB
    hg = HEAD_GROUP
    gm = hg * QB
    fc = FAR_CHUNK
    tpc = fc // QB

    @pl.when(i == 0)
    def _():
        ks_scr[0:QB, :] = jnp.zeros((QB, HD), MXU_DT)
        vs_scr[0:QB, :] = jnp.zeros((QB, HD), MXU_DT)
        kw_scr[0:wt * QB, :] = jnp.zeros((wt * QB, HD), MXU_DT)
        vw_scr[0:wt * QB, :] = jnp.zeros((wt * QB, HD), MXU_DT)
        ks_scr[QB:QB + seq, :] = ksr_ref[...].astype(MXU_DT)
        vs_scr[QB:QB + seq, :] = vsr_ref[...].astype(MXU_DT)
        kw_scr[wt * QB:wt * QB + seq, :] = kwr_ref[...].astype(MXU_DT)
        vw_scr[wt * QB:wt * QB + seq, :] = vwr_ref[...].astype(MXU_DT)
        for which, (rows_ref, dst) in enumerate(((kcr_ref, kc_scr), (vcr_ref, vc_scr))):
            ng = nt * QB // CMP_STRIDE
            first = jnp.zeros((ng, HD), F32)
            second = jnp.zeros((ng, HD), F32)
            for j in range(CMP_STRIDE):
                xj = rows_ref[pl.ds(j, ng, stride=CMP_STRIDE), :]
                first = first + xj * a_ref[which, j:j + 1, :]
                second = second + xj * a_ref[which, CMP_STRIDE + j:CMP_STRIDE + j + 1, :]
            pre = first + pltpu.roll(second, ng - 1, 0)
            dst[...] = _dot_hi(pre, phi_ref[which]).astype(MXU_DT)

    for h in range(HEADS):
        q_scr[h * QB:(h + 1) * QB, :] = (q_ref[:, h * HD:(h + 1) * HD] * NSA_SCALE).astype(MXU_DT)

    row = lax.broadcasted_iota(jnp.int32, (QB, QB), 0)
    lane = lax.broadcasted_iota(jnp.int32, (QB, QB), 1)
    qpos = i * QB + row
    tri = jnp.where(lane <= row, 0.0, NEG_INF)
    cmp_valid = (qpos >= lane * CMP_STRIDE + (CMP_BLOCK - 1)) & (lane < n_cmp)
    cmp_mask = jnp.where(cmp_valid, 0.0, NEG_INF)

    s = _dot_nt(q_scr[...], kc_scr[...]).reshape(HEADS, QB, QB) + bc_ref[0] + cmp_mask[None]
    p = _softmax_rows(s)
    imp = jnp.sum(p, axis=0)
    oc_scr[...] = _dot(p.reshape(HEADS * QB, QB).astype(MXU_DT), vc_scr[...])

    slc_imp = _dot_hi(imp, amat_ref[...])
    cur = qpos // SLC_BLOCK
    forced = (lane == 0) | (lane == cur) | (lane == cur - 1)
    score = jnp.where(lane > cur, -1.0, jnp.where(forced, FORCE_SCORE, slc_imp))
    score = jnp.where(lane < n_blk, score, -2.0)
    rank = jnp.zeros((QB, QB), F32)
    for b2 in range(n_blk):
        col = score[:, b2:b2 + 1]
        beats = (col > score) | ((col == score) & (lane > b2))
        rank = rank + jnp.where(beats, 1.0, 0.0)
    sel = jnp.where((rank < min(N_SEL, n_blk)) & (lane < n_blk), 1.0, 0.0)
    sel_keys = _dot(sel.astype(MXU_DT), emat_ref[...])
    for kt in range(nt):
        tile = jnp.where(sel_keys[:, kt * QB:(kt + 1) * QB] > 0.5, 0.0, NEG_INF)
        selk_scr[kt] = tile
        a_scr[kt // tpc, :, (kt % tpc) * QB:(kt % tpc + 1) * QB] = jnp.where(kt < i - 1, tile, NEG_INF)
    prev_ok = jnp.where(i > 0, 0.0, NEG_INF)
    near_mask = jnp.concatenate([selk_scr[jnp.maximum(i - 1, 0)] + prev_ok, selk_scr[i] + tri], axis=1)

    zeros = jnp.zeros((QB, QB), F32)
    band = jnp.where(lane >= row, 0.0, NEG_INF)

    def tile_ok(t):
        return jnp.where(i - wt + t >= 0, 0.0, NEG_INF)

    win_far_mask = jnp.concatenate(
        [band + tile_ok(0)] + [zeros + tile_ok(t) for t in range(1, wt - 1)], axis=1)
    win_near_mask = jnp.concatenate([zeros + tile_ok(wt - 1), tri], axis=1)
    n_far = (i + tpc - 2) // tpc
    n0 = pl.multiple_of(i * QB, QB)

    def group(g, carry):
        r0 = pl.multiple_of(g * gm, gm)
        qg = q_scr[pl.ds(r0, gm), :]
        bn = bn_ref[pl.ds(g * hg, hg)]

        mx_scr[...] = jnp.full((hg, QB, QB), NEG_INF, F32)

        def far_max(c, carry2):
            k0 = pl.multiple_of(QB + c * fc, QB)
            s = _dot_nt(qg, ks_scr[pl.ds(k0, fc), :]).reshape(hg, QB, fc) + a_scr[c][None]
            sf_scr[c] = s
            mx_scr[...] = jnp.maximum(mx_scr[...], _fold(jnp.maximum, _lane_tiles(s)))
            return carry2

        lax.fori_loop(0, n_far, far_max, 0)
        sn = _dot_nt(qg, ks_scr[pl.ds(n0, 2 * QB), :]).reshape(hg, QB, 2 * QB) + bn + near_mask[None]
        sn_scr[...] = sn
        mfold = jnp.maximum(mx_scr[...], _fold(jnp.maximum, _lane_tiles(sn)))
        mb_scr[...] = jnp.broadcast_to(jnp.max(mfold, axis=-1, keepdims=True), (hg, QB, QB))
        l_scr[...] = jnp.zeros((hg, QB, QB), F32)
        acc_scr[...] = jnp.zeros((gm, HD), F32)

        def far_pv(c, carry2):
            k0 = pl.multiple_of(QB + c * fc, QB)
            mbv = mb_scr[...]
            ps = [jnp.exp(t - mbv) for t in _lane_tiles(sf_scr[c])]
            l_scr[...] += _fold(jnp.add, ps)
            pm = jnp.concatenate(ps, axis=-1).reshape(gm, fc).astype(MXU_DT)
            acc_scr[...] += _dot(pm, vs_scr[pl.ds(k0, fc), :])
            return carry2

        lax.fori_loop(0, n_far, far_pv, 0)
        mbv = mb_scr[...]
        pn = [jnp.exp(t - mbv) for t in _lane_tiles(sn_scr[...])]
        l = l_scr[...] + _fold(jnp.add, pn)
        pm = jnp.concatenate(pn, axis=-1).reshape(gm, 2 * QB).astype(MXU_DT)
        acc = acc_scr[...] + _dot(pm, vs_scr[pl.ds(n0, 2 * QB), :])
        o_slc = acc.reshape(hg, QB, HD) / jnp.sum(l, axis=-1, keepdims=True)
        os_scr[pl.ds(r0, gm), :] = o_slc.reshape(gm, HD)

        nw = (wt + 1) * QB
        nf = (wt - 1) * QB
        sw = _dot_nt(qg, kw_scr[pl.ds(n0, nw), :]).reshape(hg, QB, nw)
        s_far = sw[:, :, :nf] + win_far_mask[None]
        s_near = sw[:, :, nf:] + bn + win_near_mask[None]
        m = jnp.maximum(jnp.max(s_far, axis=-1, keepdims=True), jnp.max(s_near, axis=-1, keepdims=True))
        e_far = jnp.exp(s_far - m)
        e_near = jnp.exp(s_near - m)
        l = jnp.sum(e_far, axis=-1, keepdims=True) + jnp.sum(e_near, axis=-1, keepdims=True)
        pm = jnp.concatenate([e_far, e_near], axis=-1).reshape(gm, nw).astype(MXU_DT)
        o_win = _dot(pm, vw_scr[pl.ds(n0, nw), :]).reshape(hg, QB, HD) / l
        ow_scr[pl.ds(r0, gm), :] = o_win.reshape(gm, HD)
        return carry

    lax.fori_loop(0, HEADS // hg, group, 0)

    gates = _sigmoid(g_ref[...])
    for h in range(HEADS):
        rows = slice(h * QB, (h + 1) * QB)
        o = (gates[:, h:h + 1] * oc_scr[rows, :] + gates[:, HEADS + h:HEADS + h + 1] * os_scr[rows, :]
             + gates[:, 2 * HEADS + h:2 * HEADS + h + 1] * ow_scr[rows, :])
        o_ref[:, h * HD:(h + 1) * HD] = o.astype(o_ref.dtype)


def nsa_prompt_attention(h, cmp_a, cmp_phi, bias_c, bias_near, *, batch, seq):
    nt = seq // QB
    n_blk = seq // SLC_BLOCK
    wt = WINDOW // QB
    assert seq // CMP_STRIDE == QB and n_blk <= QB and WINDOW % QB == 0 and wt >= 2
    assert FAR_CHUNK % QB == 0 and seq % FAR_CHUNK == 0 and HEADS % HEAD_GROUP == 0
    q_cols = HEADS * HD
    kv_blk0 = q_cols // HD
    gate_blk = kv_blk0 + NSA_KV_SLOTS
    c = np.arange(QB)[:, None]
    b = np.arange(QB)[None, :]
    ratio = SLC_BLOCK // CMP_STRIDE
    amat = (((c >= ratio * b) & (c <= ratio * b + ratio - 1)).astype(np.float32)
            + ((c + 1 >= ratio * b) & (c + 1 <= ratio * b + ratio - 1)).astype(np.float32))
    amat = amat * (b < n_blk)
    emat = (np.arange(seq)[None, :] // SLC_BLOCK == np.arange(QB)[:, None]).astype(np.float32)

    kv_spec = lambda s: pl.BlockSpec((seq, HD), lambda bb, i, s=s: (bb, kv_blk0 + s))
    full = lambda shape: pl.BlockSpec(shape, lambda bb, i: (0,) * len(shape))
    keys_scr = lambda pad_tiles: pltpu.VMEM(((nt + pad_tiles) * QB, HD), MXU_DT)
    heads_scr = lambda dt: pltpu.VMEM((HEADS * QB, HD), dt)
    group_scr = lambda: pltpu.VMEM((HEAD_GROUP, QB, QB), F32)
    return pl.pallas_call(
        functools.partial(_nsa_prompt_kernel, nt=nt, n_blk=n_blk),
        grid=(batch, nt),
        in_specs=[pl.BlockSpec((QB, q_cols), lambda bb, i: (bb * nt + i, 0)),
                  pl.BlockSpec((QB, LANES), lambda bb, i: (bb * nt + i, gate_blk))]
                 + [kv_spec(s) for s in range(NSA_KV_SLOTS)]
                 + [full((2, CMP_BLOCK, HD)), full((2, HD, HD)),
                    pl.BlockSpec((1, HEADS, QB, QB), lambda bb, i: (i, 0, 0, 0)),
                    full((HEADS, QB, 2 * QB)), full((QB, QB)), full((QB, seq))],
        out_specs=pl.BlockSpec((QB, q_cols), lambda bb, i: (bb * nt + i, 0)),
        out_shape=jax.ShapeDtypeStruct((batch * seq, q_cols), MXU_DT),
        scratch_shapes=[pltpu.VMEM((QB, HD), MXU_DT), pltpu.VMEM((QB, HD), MXU_DT),
                        keys_scr(1), keys_scr(1), keys_scr(wt), keys_scr(wt),
                        heads_scr(MXU_DT), pltpu.VMEM((nt, QB, QB), F32),
                        pltpu.VMEM((seq // FAR_CHUNK, QB, FAR_CHUNK), F32),
                        group_scr(), group_scr(), group_scr(),
                        pltpu.VMEM((HEAD_GROUP * QB, HD), F32),
                        pltpu.VMEM((HEAD_GROUP, QB, 2 * QB), F32),
                        pltpu.VMEM((seq // FAR_CHUNK, HEAD_GROUP, QB, FAR_CHUNK), F32),
                        heads_scr(F32), heads_scr(F32), heads_scr(F32)],
        compiler_params=_cparams(2),
        name="nsa_prompt_attention",
    )(h, h, h, h, h, h, h, h, cmp_a, cmp_phi, bias_c, bias_near,
      jnp.asarray(amat), jnp.asarray(emat, MXU_DT))


def _nsa_sample_cmp_kernel(pt_ref, q_ref, w_ref, phi_ref, bias_ref, cache_ref, oc_ref, imp_ref,
                           buf, sem, part_scr, kc_scr, vc_scr, *, n_chunks, pages, past_len):
    b = pl.program_id(0)
    nb = pl.num_programs(0)
    prow = LANES * NSA_CACHE_SLOTS
    rows = pages * prow
    grow = CMP_STRIDE * NSA_CACHE_SLOTS
    groups = pages * LANES // CMP_STRIDE
    sub = 8
    nv = grow // sub

    def chunk_copies(bb, ch, slot):
        cps = []
        for p in range(pages):
            page = pt_ref[bb, ch * pages + p]
            cps.append(pltpu.make_async_copy(cache_ref.at[page],
                                             buf.at[slot, pl.ds(p * prow, prow), :], sem.at[slot]))
        if ch + 1 < n_chunks:
            page = pt_ref[bb, (ch + 1) * pages]
            cps.append(pltpu.make_async_copy(cache_ref.at[page, pl.ds(0, grow), :],
                                             buf.at[slot, pl.ds(rows, grow), :], sem.at[slot]))
        return cps

    ahead = RING - 1

    @pl.when(b == 0)
    def _():
        for g0 in range(ahead):
            for cp in chunk_copies(0, g0, g0):
                cp.start()

    for ch in range(n_chunks):
        g = b * n_chunks + ch
        slot = g % RING
        nxt = (g + ahead) % RING
        if ch + ahead < n_chunks:
            for cp in chunk_copies(b, ch + ahead, nxt):
                cp.start()
        else:
            @pl.when(b + 1 < nb)
            def _():
                for cp in chunk_copies(b + 1, ch + ahead - n_chunks, nxt):
                    cp.start()
        for cp in chunk_copies(b, ch, slot):
            cp.wait()
        if ch == n_chunks - 1:
            buf[slot, rows:rows + grow, :] = jnp.zeros((grow, HD), F32)
        x0 = buf[slot, 0:rows, :].reshape(groups, nv, sub, HD)
        x1 = buf[slot, grow:rows + grow, :].reshape(groups, nv, sub, HD)
        part = x0[:, 0] * w_ref[0, 0] + x1[:, 0] * w_ref[1, 0]
        for t in range(1, nv):
            part = part + x0[:, t] * w_ref[0, t] + x1[:, t] * w_ref[1, t]
        part_scr[...] = part.reshape(groups * sub, HD)
        for kv, dst in enumerate((kc_scr, vc_scr)):
            pre = (part_scr[pl.ds(kv, groups, stride=sub), :]
                   + part_scr[pl.ds(NSA_CACHE_SLOTS + kv, groups, stride=sub), :])
            dst[ch * groups:(ch + 1) * groups, :] = _dot_hi(pre, phi_ref[kv]).astype(MXU_DT)

    n_tok = n_chunks * groups
    q = (q_ref[0] * NSA_SCALE).astype(MXU_DT)
    tok = lax.broadcasted_iota(jnp.int32, (HEADS, n_tok), 1)
    valid = tok * CMP_STRIDE + (CMP_BLOCK - 1) <= past_len
    s = _dot_nt(q, kc_scr[...]) + bias_ref[...] + jnp.where(valid, 0.0, NEG_INF)
    p = _softmax_rows(s)
    oc_ref[0] = _dot(p.astype(MXU_DT), vc_scr[...])
    imp_ref[0] = jnp.sum(p, axis=0, keepdims=True)


def nsa_sample_compressed(q, page_table, cache_rows, cmp_a, cmp_phi, bias_c, *, past_len, pages=16):
    db, n_pages = page_table.shape
    assert n_pages % pages == 0
    n_chunks = n_pages // pages
    assert n_chunks >= RING - 1
    rows = (pages * LANES + CMP_STRIDE) * NSA_CACHE_SLOTS
    n_tok = n_pages * LANES // CMP_STRIDE
    nv = CMP_STRIDE // 2
    w = cmp_a.reshape(2, 2, nv, 2, HD).transpose(1, 2, 3, 0, 4)
    w = jnp.pad(w, ((0, 0), (0, 0), (0, 0), (0, NSA_CACHE_SLOTS - 2), (0, 0)))
    w = w.reshape(2, nv, 2 * NSA_CACHE_SLOTS, HD)
    grid_spec = pltpu.PrefetchScalarGridSpec(
        num_scalar_prefetch=1,
        grid=(db,),
        in_specs=[pl.BlockSpec((1, HEADS, HD), lambda b, pt: (b, 0, 0)),
                  pl.BlockSpec((2, nv, 2 * NSA_CACHE_SLOTS, HD), lambda b, pt: (0, 0, 0, 0)),
                  pl.BlockSpec((2, HD, HD), lambda b, pt: (0, 0, 0)),
                  pl.BlockSpec((HEADS, n_tok), lambda b, pt: (0, 0)),
                  pl.BlockSpec(memory_space=pl.ANY)],
        out_specs=[pl.BlockSpec((1, HEADS, HD), lambda b, pt: (b, 0, 0)),
                   pl.BlockSpec((1, 1, n_tok), lambda b, pt: (b, 0, 0))],
        scratch_shapes=[pltpu.VMEM((RING, rows, HD), F32),
                        pltpu.SemaphoreType.DMA((RING,)),
                        pltpu.VMEM((pages * LANES // CMP_STRIDE * 8, HD), F32),
                        pltpu.VMEM((n_tok, HD), MXU_DT), pltpu.VMEM((n_tok, HD), MXU_DT)])
    return pl.pallas_call(
        functools.partial(_nsa_sample_cmp_kernel, n_chunks=n_chunks, pages=pages, past_len=past_len),
        grid_spec=grid_spec,
        out_shape=[jax.ShapeDtypeStruct((db, HEADS, HD), F32),
                   jax.ShapeDtypeStruct((db, 1, n_tok), F32)],
        compiler_params=_cparams(1),
        name="nsa_sample_compressed",
    )(page_table, q, w, cmp_phi, bias_c, cache_rows)


def _select_kernel(imp_ref, amat_ref, idx_ref, *, npb, n_pick):
    slc_imp = _dot_hi(imp_ref[...], amat_ref[...])
    lane = lax.broadcasted_iota(jnp.int32, slc_imp.shape, 1).astype(F32)
    forced = (lane == 0.0) | (lane == float(npb - 1))
    score = jnp.where(forced, FORCE_SCORE, slc_imp)
    score = jnp.where(lane < float(npb), score, -2.0)
    out_lane = lax.broadcasted_iota(jnp.int32, idx_ref.shape, 1)
    out = jnp.zeros(idx_ref.shape, F32)
    for t in range(n_pick):
        best = jnp.max(score, axis=-1, keepdims=True)
        first = jnp.min(jnp.where(score == best, lane, 1e9), axis=-1, keepdims=True)
        out = jnp.where(out_lane == t, first, out)
        score = jnp.where(lane == first, -3.0, score)
    idx_ref[...] = out.astype(jnp.int32)


def nsa_sample_select(imp, *, npb):
    db, n_tok = imp.shape
    assert npb >= N_SEL
    nbp = -(-npb // LANES) * LANES
    c = np.arange(n_tok)[:, None]
    b = np.arange(nbp)[None, :]
    ratio = SLC_BLOCK // CMP_STRIDE
    amat = (((c >= ratio * b) & (c <= ratio * b + ratio - 1)).astype(np.float32)
            + ((c + 1 >= ratio * b) & (c + 1 <= ratio * b + ratio - 1)).astype(np.float32))
    amat = amat * (b < npb)
    return pl.pallas_call(
        functools.partial(_select_kernel, npb=npb, n_pick=N_SEL - 1),
        out_shape=jax.ShapeDtypeStruct((db, LANES), jnp.int32),
        compiler_params=pltpu.CompilerParams(vmem_limit_bytes=VMEM_LIMIT),
        name="nsa_sample_select",
    )(imp, jnp.asarray(amat))


def _nsa_sample_slc_win_kernel(idx_ref, pt_ref, q_ref, kvn_ref, win_ref, bsl_ref, b0_ref, bw_ref,
                               cache_ref, osl_ref, owin_ref, kbuf, sem, *, n_pick, npb):
    b = pl.program_id(0)
    nb = pl.num_programs(0)
    bpp = LANES // SLC_BLOCK
    brow = SLC_BLOCK * NSA_CACHE_SLOTS

    def block_copies(bb, slot):
        cps = []
        for t in range(n_pick):
            blk = idx_ref[bb, t]
            page = pt_ref[bb, blk // bpp]
            off = pl.multiple_of((blk % bpp) * brow, brow)
            cps.append(pltpu.make_async_copy(cache_ref.at[page, pl.ds(off, brow), :],
                                             kbuf.at[slot, pl.ds(t * brow, brow), :], sem.at[slot]))
        return cps

    slot = b % 2

    @pl.when(b == 0)
    def _():
        for cp in block_copies(0, 0):
            cp.start()

    @pl.when(b + 1 < nb)
    def _():
        for cp in block_copies(b + 1, 1 - slot):
            cp.start()

    for cp in block_copies(b, slot):
        cp.wait()

    n_keys = (n_pick + 1) * SLC_BLOCK
    kvn = kvn_ref[0]
    rowi = lax.broadcasted_iota(jnp.int32, (SLC_BLOCK, HD), 0)

    def gathered(cache_slot):
        parts = [kbuf[slot, pl.ds(t * brow + cache_slot, SLC_BLOCK, stride=NSA_CACHE_SLOTS), :]
                 for t in range(n_pick)]
        new_row = kvn[:, cache_slot * HD:(cache_slot + 1) * HD]
        parts.append(jnp.where(rowi == 0, new_row, 0.0))
        return jnp.concatenate(parts, axis=0).astype(MXU_DT)

    q = (q_ref[0] * NSA_SCALE).astype(MXU_DT)
    keys = gathered(2)
    vals = gathered(3)
    lane = lax.broadcasted_iota(jnp.int32, (HEADS, LANES), 1)

    def slot_bias(t):
        if t == n_pick:
            return b0_ref[...]
        blk = idx_ref[b, t]
        return jnp.where(blk == npb - 1, bsl_ref[0], jnp.where(blk == npb - 2, bsl_ref[1], bsl_ref[2]))

    tiles = []
    for u in range(n_keys // LANES):
        tiles.append(jnp.where(lane < SLC_BLOCK, slot_bias(2 * u), slot_bias(2 * u + 1)))
    bias = jnp.concatenate(tiles, axis=1)
    col = lax.broadcasted_iota(jnp.int32, (HEADS, n_keys), 1)
    s = _dot_nt(q, keys) + bias + jnp.where(col <= n_pick * SLC_BLOCK, 0.0, NEG_INF)
    p = _softmax_rows(s)
    osl_ref[0] = _dot(p.astype(MXU_DT), vals)

    w_buf = win_ref.shape[1] // 2
    wk = win_ref[0, pl.ds(0, w_buf, stride=2), :].astype(MXU_DT)
    wv = win_ref[0, pl.ds(1, w_buf, stride=2), :].astype(MXU_DT)
    sw = _dot_nt(q, wk) + bw_ref[...]
    new_k = kvn[:, 4 * HD:5 * HD].astype(MXU_DT).astype(F32)
    new_v = kvn[:, 5 * HD:6 * HD].astype(MXU_DT).astype(F32)
    s_new = jnp.sum(q.astype(F32) * new_k, axis=-1, keepdims=True) + b0_ref[:, 0:1]
    m = jnp.maximum(jnp.max(sw, axis=-1, keepdims=True), s_new)
    pw = jnp.exp(sw - m)
    pn = jnp.exp(s_new - m)
    l = jnp.sum(pw, axis=-1, keepdims=True) + pn
    pn_r = pn.astype(MXU_DT).astype(F32)
    owin_ref[0] = (_dot(pw.astype(MXU_DT), wv) + pn_r * new_v) / l


def nsa_sample_slc_win(q, kv_new, win_state, idx, page_table, cache_rows, bias_slc, bias_0, bias_w,
                       *, npb):
    db = q.shape[0]
    n_pick = N_SEL - 1
    w_buf = win_state.shape[1] // 2
    assert w_buf <= WINDOW and ((n_pick + 1) * SLC_BLOCK) % LANES == 0
    n_keys = (n_pick + 1) * SLC_BLOCK
    grid_spec = pltpu.PrefetchScalarGridSpec(
        num_scalar_prefetch=2,
        grid=(db,),
        in_specs=[pl.BlockSpec((1, HEADS, HD), lambda b, ix, pt: (b, 0, 0)),
                  pl.BlockSpec((1, 1, NSA_KV_SLOTS * HD), lambda b, ix, pt: (b, 0, 0)),
                  pl.BlockSpec((1, 2 * w_buf, HD), lambda b, ix, pt: (b, 0, 0)),
                  pl.BlockSpec((3, HEADS, LANES), lambda b, ix, pt: (0, 0, 0)),
                  pl.BlockSpec((HEADS, LANES), lambda b, ix, pt: (0, 0)),
                  pl.BlockSpec((HEADS, w_buf), lambda b, ix, pt: (0, 0)),
                  pl.BlockSpec(memory_space=pl.ANY)],
        out_specs=[pl.BlockSpec((1, HEADS, HD), lambda b, ix, pt: (b, 0, 0)),
                   pl.BlockSpec((1, HEADS, HD), lambda b, ix, pt: (b, 0, 0))],
        scratch_shapes=[pltpu.VMEM((2, n_pick * SLC_BLOCK * NSA_CACHE_SLOTS, HD), F32),
                        pltpu.SemaphoreType.DMA((2,))])
    return pl.pallas_call(
        functools.partial(_nsa_sample_slc_win_kernel, n_pick=n_pick, npb=npb),
        grid_spec=grid_spec,
        out_shape=[jax.ShapeDtypeStruct((db, HEADS, HD), F32),
                   jax.ShapeDtypeStruct((db, HEADS, HD), F32)],
        compiler_params=_cparams(1),
        name="nsa_sample_slc_win",
    )(idx, page_table, q, kv_new, win_state, bias_slc, bias_0, bias_w, cache_rows)


def _gate_combine_kernel(g_ref, oc_ref, os_ref, ow_ref, o_ref):
    g = _sigmoid(g_ref[...])
    o_ref[...] = (g[0] * oc_ref[...] + g[1] * os_ref[...] + g[2] * ow_ref[...]).astype(o_ref.dtype)


def nsa_gate_combine(gate_logits, o_cmp, o_slc, o_win):
    r = o_cmp.shape[0]
    return pl.pallas_call(
        _gate_combine_kernel,
        out_shape=jax.ShapeDtypeStruct((r, HD), MXU_DT),
        compiler_params=pltpu.CompilerParams(vmem_limit_bytes=VMEM_LIMIT),
        name="nsa_gate_combine",
    )(gate_logits, o_cmp, o_slc, o_win)


def _mla_prep_kernel(h_ref, gq_ref, gkv_ref, cos_ref, sin_ref, dup_ref,
                     cq_ref, ckv_ref, rows_ref, kr_ref):
    h = h_ref[...]

    def rms(x, g):
        return x * lax.rsqrt(jnp.mean(x * x, axis=-1, keepdims=True) + RMS_EPS) * g

    cq_ref[...] = rms(h[:, :Q_LORA], gq_ref[...]).astype(cq_ref.dtype)
    ckv = rms(h[:, Q_LORA:Q_LORA + KV_LORA], gkv_ref[...])
    ckv_ref[...] = ckv.astype(ckv_ref.dtype)
    kr2 = _dot_hi(h[:, Q_LORA + KV_LORA:Q_LORA + KV_LORA + QK_ROPE], dup_ref[...])
    kr2 = _rope_lanes(kr2, cos_ref[...], sin_ref[...])
    kr_ref[...] = kr2
    rows_ref[:, :KV_LORA] = ckv
    rows_ref[:, KV_LORA:] = kr2[:, :QK_ROPE]


def mla_prep(h, g_q, g_kv, cos_t, sin_t, *, pos_blocks, tm=512):
    m = h.shape[0]
    tm = min(tm, m)
    dup = np.concatenate([np.eye(QK_ROPE, dtype=np.float32)] * 2, axis=1)
    return pl.pallas_call(
        _mla_prep_kernel,
        grid=(m // tm,),
        in_specs=[pl.BlockSpec((tm, h.shape[1]), lambda i: (i, 0)),
                  pl.BlockSpec((1, Q_LORA), lambda i: (0, 0)),
                  pl.BlockSpec((1, KV_LORA), lambda i: (0, 0)),
                  pl.BlockSpec((tm, LANES), lambda i: (i % pos_blocks, 0)),
                  pl.BlockSpec((tm, LANES), lambda i: (i % pos_blocks, 0)),
                  pl.BlockSpec((QK_ROPE, LANES), lambda i: (0, 0))],
        out_specs=[pl.BlockSpec((tm, Q_LORA), lambda i: (i, 0)),
                   pl.BlockSpec((tm, KV_LORA), lambda i: (i, 0)),
                   pl.BlockSpec((tm, MLA_ROW), lambda i: (i, 0)),
                   pl.BlockSpec((tm, LANES), lambda i: (i, 0))],
        out_shape=[jax.ShapeDtypeStruct((m, Q_LORA), MXU_DT),
                   jax.ShapeDtypeStruct((m, KV_LORA), MXU_DT),
                   jax.ShapeDtypeStruct((m, MLA_ROW), F32),
                   jax.ShapeDtypeStruct((m, LANES), F32)],
        compiler_params=_cparams(1),
        name="mla_prep",
    )(h, g_q.reshape(1, -1), g_kv.reshape(1, -1), cos_t, sin_t, jnp.asarray(dup))


def _rope_q_kernel(q_ref, cos_ref, sin_ref, o_ref):
    cos_t = cos_ref[...]
    sin_t = sin_ref[...]
    for g in range(q_ref.shape[1] // LANES):
        cols = slice(g * LANES, (g + 1) * LANES)
        o_ref[:, cols] = _rope_lanes(q_ref[:, cols], cos_t, sin_t)


def rope_q(q, cos_t, sin_t, *, pos_blocks, tm=512):
    m = q.shape[0]
    tm = min(tm, m)
    wr = HEADS * QK_ROPE
    assert (HEADS * HD) % wr == 0
    col_blk = HEADS * HD // wr
    return pl.pallas_call(
        _rope_q_kernel,
        grid=(m // tm,),
        in_specs=[pl.BlockSpec((tm, wr), lambda i: (i, col_blk)),
                  pl.BlockSpec((tm, LANES), lambda i: (i % pos_blocks, 0)),
                  pl.BlockSpec((tm, LANES), lambda i: (i % pos_blocks, 0))],
        out_specs=pl.BlockSpec((tm, wr), lambda i: (i, 0)),
        out_shape=jax.ShapeDtypeStruct((m, wr), F32),
        compiler_params=_cparams(1),
        name="rope_q",
    )(q, cos_t, sin_t)


def _mla_prompt_kernel(qn_ref, qr_ref, kv_ref, kr_ref, o_ref, qc_scr, kc_scr, v_scr, *, seq):
    h = pl.program_id(1)
    blk = MLA_QBLK
    nq = seq // blk
    lane = lax.broadcasted_iota(jnp.int32, (blk, LANES), 1)
    for qb in range(nq):
        rows = slice(qb * blk, (qb + 1) * blk)
        qr = jnp.where((lane // QK_ROPE) == (h % 2), qr_ref[rows, :], 0.0)
        qc_scr[rows, 0:HD] = (qn_ref[rows, :] * MLA_SCALE).astype(MXU_DT)
        qc_scr[rows, HD:2 * HD] = (qr * MLA_SCALE).astype(MXU_DT)
        kc_scr[rows, 0:HD] = kv_ref[rows, 0:HD].astype(MXU_DT)
        kc_scr[rows, HD:2 * HD] = kr_ref[rows, :].astype(MXU_DT)
        v_scr[rows, :] = kv_ref[rows, HD:2 * HD].astype(MXU_DT)

    r = lax.broadcasted_iota(jnp.int32, (blk, blk), 0)
    c = lax.broadcasted_iota(jnp.int32, (blk, blk), 1)
    tri = jnp.where(c <= r, 0.0, NEG_INF)
    for qb in range(nq):
        d0 = qb * blk
        q = qc_scr[d0:d0 + blk, :]
        s_d = _dot_nt(q, kc_scr[d0:d0 + blk, :]) + tri
        m = jnp.max(s_d, axis=-1, keepdims=True)
        if qb > 0:
            s_o = _dot_nt(q, kc_scr[0:d0, :])
            m = jnp.maximum(m, jnp.max(s_o, axis=-1, keepdims=True))
        p_d = jnp.exp(s_d - m)
        l = jnp.sum(p_d, axis=-1, keepdims=True)
        o = _dot(p_d.astype(MXU_DT), v_scr[d0:d0 + blk, :])
        if qb > 0:
            p_o = jnp.exp(s_o - m)
            l = l + jnp.sum(p_o, axis=-1, keepdims=True)
            o = o + _dot(p_o.astype(MXU_DT), v_scr[0:d0, :])
        o_ref[d0:d0 + blk, :] = (o / l).astype(o_ref.dtype)


def mla_prompt_attention(q, qr, kvx, kr2, *, batch, seq):
    assert seq % MLA_QBLK == 0
    return pl.pallas_call(
        functools.partial(_mla_prompt_kernel, seq=seq),
        grid=(batch, HEADS),
        in_specs=[pl.BlockSpec((seq, HD), lambda b, h: (b, h)),
                  pl.BlockSpec((seq, LANES), lambda b, h: (b, h // 2)),
                  pl.BlockSpec((seq, 2 * HD), lambda b, h: (b, h)),
                  pl.BlockSpec((seq, LANES), lambda b, h: (b, 0))],
        out_specs=pl.BlockSpec((seq, HD), lambda b, h: (b, h)),
        out_shape=jax.ShapeDtypeStruct((batch * seq, HEADS * HD), MXU_DT),
        scratch_shapes=[pltpu.VMEM((seq, 2 * HD), MXU_DT), pltpu.VMEM((seq, 2 * HD), MXU_DT),
                        pltpu.VMEM((seq, HD), MXU_DT)],
        compiler_params=_cparams(2),
        name="mla_prompt_attention",
    )(q, qr, kvx, kr2)


def _absorb_q_kernel(q_ref, w_ref, o_ref):
    o_ref[0] = _dot_nt(q_ref[...].astype(MXU_DT), w_ref[...].astype(MXU_DT))


def mla_absorb_q(q, w_kv_b):
    m = q.shape[0]
    return pl.pallas_call(
        _absorb_q_kernel,
        grid=(HEADS,),
        in_specs=[pl.BlockSpec((m, HD), lambda h: (0, h)),
                  pl.BlockSpec((KV_LORA, HD), lambda h: (0, 2 * h))],
        out_specs=pl.BlockSpec((1, m, KV_LORA), lambda h: (h, 0, 0)),
        out_shape=jax.ShapeDtypeStruct((HEADS, m, KV_LORA), F32),
        compiler_params=_cparams(1),
        name="mla_absorb_q",
    )(q, w_kv_b)


def _absorb_o_kernel(o_ref, w_ref, out_ref):
    out_ref[...] = _dot(o_ref[0].astype(MXU_DT), w_ref[...].astype(MXU_DT)).astype(out_ref.dtype)


def mla_absorb_o(o_lat, w_kv_b):
    m = o_lat.shape[1]
    return pl.pallas_call(
        _absorb_o_kernel,
        grid=(HEADS,),
        in_specs=[pl.BlockSpec((1, m, KV_LORA), lambda h: (h, 0, 0)),
                  pl.BlockSpec((KV_LORA, HD), lambda h: (0, 2 * h + 1))],
        out_specs=pl.BlockSpec((m, HD), lambda h: (0, h)),
        out_shape=jax.ShapeDtypeStruct((m, HEADS * HD), MXU_DT),
        compiler_params=_cparams(1),
        name="mla_absorb_o",
    )(o_lat, w_kv_b)


def _mla_decode_kernel(pt_ref, ql_ref, qr_ref, new_ref, cache_ref, o_ref, buf, sem,
                       *, n_chunks, pages):
    b = pl.program_id(0)
    nb = pl.num_programs(0)

    def chunk_copies(bb, ch, slot):
        cps = []
        for p in range(pages):
            page = pt_ref[bb, ch * pages + p]
            cps.append(pltpu.make_async_copy(cache_ref.at[page], buf.at[slot, p], sem.at[slot]))
        return cps

    ahead = RING - 1

    @pl.when(b == 0)
    def _():
        for g0 in range(ahead):
            for cp in chunk_copies(0, g0, g0):
                cp.start()

    ql = (ql_ref[0] * MLA_SCALE).astype(MXU_DT)
    qr = (qr_ref[0] * MLA_SCALE).astype(MXU_DT)

    def chunk_step(ch, carry):
        m, l, acc = carry
        g = b * n_chunks + ch
        slot = g % RING
        nxt = (g + ahead) % RING

        @pl.when(ch + ahead < n_chunks)
        def _():
            for cp in chunk_copies(b, ch + ahead, nxt):
                cp.start()

        @pl.when((ch + ahead >= n_chunks) & (b + 1 < nb))
        def _():
            for cp in chunk_copies(b + 1, ch + ahead - n_chunks, nxt):
                cp.start()

        for cp in chunk_copies(b, ch, slot):
            cp.wait()
        s = jnp.concatenate(
            [_dot(ql, buf[slot, p, 0:KV_LORA, :].astype(MXU_DT))
             + _dot(qr, buf[slot, p, KV_LORA:MLA_ROW, :].astype(MXU_DT))
             for p in range(pages)], axis=1)
        m_new = jnp.maximum(m, jnp.max(s, axis=-1, keepdims=True))
        alpha = jnp.exp(m - m_new)
        pr = jnp.exp(s - m_new)
        l = l * alpha + jnp.sum(pr, axis=-1, keepdims=True)
        pr = pr.astype(MXU_DT)
        acc = acc * alpha
        for p in range(pages):
            acc = acc + _dot_nt(pr[:, p * LANES:(p + 1) * LANES],
                                buf[slot, p, 0:KV_LORA, :].astype(MXU_DT))
        return m_new, l, acc

    m, l, acc = lax.fori_loop(
        0, n_chunks, chunk_step,
        (jnp.full((HEADS, 1), NEG_INF, F32), jnp.zeros((HEADS, 1), F32),
         jnp.zeros((HEADS, KV_LORA), F32)))

    new = new_ref[0].astype(MXU_DT).astype(F32)
    s_new = (jnp.sum(ql.astype(F32) * new[:, :KV_LORA], axis=-1, keepdims=True)
             + jnp.sum(qr.astype(F32) * new[:, KV_LORA:], axis=-1, keepdims=True))
    m_f = jnp.maximum(m, s_new)
    alpha = jnp.exp(m - m_f)
    pn = jnp.exp(s_new - m_f)
    pn_r = pn.astype(MXU_DT).astype(F32)
    o_ref[0] = (acc * alpha + pn_r * new[:, :KV_LORA]) / (l * alpha + pn)


def mla_decode(q_lat, q_rope, rows_new, page_table, cache, *, pages=32):
    db, n_pages = page_table.shape
    assert n_pages % pages == 0
    n_chunks = n_pages // pages
    assert n_chunks >= RING - 1
    grid_spec = pltpu.PrefetchScalarGridSpec(
        num_scalar_prefetch=1,
        grid=(db,),
        in_specs=[pl.BlockSpec((1, HEADS, KV_LORA), lambda b, pt: (b, 0, 0)),
                  pl.BlockSpec((1, HEADS, QK_ROPE), lambda b, pt: (b, 0, 0)),
                  pl.BlockSpec((1, 1, MLA_ROW), lambda b, pt: (b, 0, 0)),
                  pl.BlockSpec(memory_space=pl.ANY)],
        out_specs=pl.BlockSpec((1, HEADS, KV_LORA), lambda b, pt: (b, 0, 0)),
        scratch_shapes=[pltpu.VMEM((RING, pages, MLA_ROW, LANES), F32),
                        pltpu.SemaphoreType.DMA((RING,))])
    return pl.pallas_call(
        functools.partial(_mla_decode_kernel, n_chunks=n_chunks, pages=pages),
        grid_spec=grid_spec,
        out_shape=jax.ShapeDtypeStruct((db, HEADS, KV_LORA), F32),
        compiler_params=_cparams(1),
        name="mla_decode",
    )(page_table, q_lat, q_rope, rows_new, cache)


def _rope_tables(pos):
    inv = ROPE_THETA ** (-jnp.arange(0, QK_ROPE, 2, dtype=F32) / QK_ROPE)
    ang = pos.astype(F32)[:, None] * inv[None, :]
    cos, sin = jnp.cos(ang), jnp.sin(ang)
    return jnp.tile(jnp.concatenate([cos, cos], axis=-1), (1, 2)), \
        jnp.tile(jnp.concatenate([-sin, sin], axis=-1), (1, 2))


def _bias_ids(seq, past_len, w_buf):
    nt = seq // QB
    r = np.arange(QB)[:, None]
    c = np.arange(QB)[None, :]
    prev = _t5_bucket_np(r - c + QB)
    diag = _t5_bucket_np(r - c)
    qi = np.arange(nt)[:, None, None]
    cmp_p = _t5_bucket_np(qi * QB + r[None] - (c[None] * CMP_STRIDE + CMP_BLOCK - 1)).reshape(nt * QB, QB)
    n_tok = past_len // CMP_STRIDE
    cmp_s = _t5_bucket_np(past_len - (np.arange(n_tok) * CMP_STRIDE + CMP_BLOCK - 1)).reshape(-1, LANES)
    off = np.arange(LANES) % SLC_BLOCK
    slc_s = np.stack([_t5_bucket_np(SLC_BLOCK - off), _t5_bucket_np(2 * SLC_BLOCK - off),
                      _t5_bucket_np(np.full(LANES, 3 * SLC_BLOCK)), _t5_bucket_np(np.zeros(LANES))])
    win_s = _t5_bucket_np(w_buf - np.arange(w_buf)).reshape(-1, LANES)
    parts = [prev, diag, cmp_p, cmp_s, slc_s, win_s]
    rows = sum(p.shape[0] for p in parts)
    pad = -rows % 256
    ids = np.concatenate(parts + [np.zeros((pad, LANES), np.int32)], axis=0).astype(np.int32)
    offs = np.cumsum([0] + [p.shape[0] for p in parts])
    return ids, offs


def kernel(x_prompt, x_sample, p_prompt, p_sample, cache_nsa_kv, state_nsa_win, cache_mla_kv, page_table, t5_table, ln_g, ln_b, nsa_w_in, nsa_cmp_a, nsa_cmp_phi, nsa_w_o, mla_w_in, mla_q_norm, mla_kv_norm, mla_w_q_b, mla_w_kv_b, mla_w_o, ffn_w_in, ffn_w_out, ple_w_gate, ple_w_proj):
    batch, seq, d = x_prompt.shape
    db = x_sample.shape[0]
    depth = ln_g.shape[0]
    assert depth == 2 and x_sample.shape[1] == 1
    n_pages = page_table.shape[1]
    page_size = cache_nsa_kv.shape[2]
    assert page_size == LANES
    past_len = n_pages * page_size
    npb = past_len // SLC_BLOCK
    n_pool = cache_nsa_kv.shape[1]
    w_buf = state_nsa_win.shape[2]
    alpha = (2 * depth) ** 0.25
    q_cols = HEADS * HD
    kv_cols = NSA_KV_SLOTS * HD
    mp = batch * seq

    ids, offs = _bias_ids(seq, past_len, w_buf)
    assert offs[2] == 256
    bias = t5_bias_lookup(t5_table, jnp.asarray(ids), n_rel=1)
    sect = lambda k: bias[:, offs[k]:offs[k + 1]]
    bias_near = jnp.concatenate([sect(0), sect(1)], axis=-1)
    bias_c = sect(2).reshape(HEADS, seq // QB, QB, QB).transpose(1, 0, 2, 3)
    bias_cs = sect(3).reshape(HEADS, -1)
    bias_ss = sect(4)
    bias_slc = bias_ss[:, :3].transpose(1, 0, 2)
    bias_0 = bias_ss[:, 3]
    bias_w = sect(5).reshape(HEADS, w_buf)

    def dense_tail(x, mixed_in, w_o, p, i):
        y = matmul_residual(mixed_in, w_o, 0, x, alpha=alpha)
        hmid, mu, rstd = swiglu_ln(y, ln_g[i, 0], ln_b[i, 0], ffn_w_in, i)
        y = matmul_ln_residual(hmid, ffn_w_out, i, y, mu, rstd, ln_g[i, 0], ln_b[i, 0], alpha=alpha)
        return ple_ln(y, ln_g[i, 1], ln_b[i, 1], ple_w_gate, p, ple_w_proj, i)

    pp = p_prompt.reshape(depth, mp, -1)
    ps = p_sample.reshape(depth, db, -1)

    xp = x_prompt.reshape(mp, d)
    xs = x_sample.reshape(db, d)
    w_in0 = nsa_w_in[0]
    hp = matmul(xp, w_in0, tn=512)
    hs = matmul(xs, w_in0, tn=512)

    att_p = nsa_prompt_attention(hp, nsa_cmp_a[0], nsa_cmp_phi[0], bias_c, bias_near,
                                 batch=batch, seq=seq)

    hp3 = hp.reshape(batch, seq, -1)
    nsa_kv_prompt = hp3[:, :, q_cols:q_cols + 4 * HD].reshape(1, batch, seq, 4, HD)
    win_rows_p = hp3[:, :, q_cols + 4 * HD:q_cols + kv_cols].reshape(batch, seq, 2, HD)
    if seq >= w_buf:
        nsa_win_prompt = win_rows_p[:, seq - w_buf:][None]
    else:
        nsa_win_prompt = jnp.pad(win_rows_p, ((0, 0), (w_buf - seq, 0), (0, 0), (0, 0)))[None]

    qs = hs[:, :q_cols].reshape(db, HEADS, HD)
    kvn = hs[:, q_cols:q_cols + kv_cols].reshape(db, 1, kv_cols)
    cache_rows = cache_nsa_kv.reshape(cache_nsa_kv.shape[0] * n_pool, page_size * NSA_CACHE_SLOTS, HD)
    o_cmp_s, imp_s = nsa_sample_compressed(qs, page_table, cache_rows, nsa_cmp_a[0], nsa_cmp_phi[0],
                                           bias_cs, past_len=past_len)
    idx = nsa_sample_select(imp_s.reshape(db, -1), npb=npb)
    win_state = state_nsa_win[0].reshape(db, w_buf * 2, HD)
    o_slc_s, o_win_s = nsa_sample_slc_win(qs, kvn, win_state, idx, page_table, cache_rows,
                                          bias_slc, bias_0, bias_w, npb=npb)
    gate_logits = hs[:, q_cols + kv_cols:].reshape(db, 3, HEADS).transpose(1, 0, 2).reshape(3, db * HEADS, 1)
    att_s = nsa_gate_combine(gate_logits, o_cmp_s.reshape(db * HEADS, HD),
                             o_slc_s.reshape(db * HEADS, HD), o_win_s.reshape(db * HEADS, HD))
    att_s = att_s.reshape(db, q_cols)

    nsa_kv_sample = hs[:, q_cols:q_cols + 4 * HD].reshape(1, db, 1, 4, HD)
    new_win = hs[:, q_cols + 4 * HD:q_cols + kv_cols].reshape(db, 1, 2, HD)
    nsa_win_sample = jnp.concatenate([state_nsa_win[0], new_win], axis=1)[:, -w_buf:][None]

    xp = dense_tail(xp, att_p, nsa_w_o, pp, 0)
    xs = dense_tail(xs, att_s, nsa_w_o, ps, 0)

    wqb = mla_w_q_b[0].reshape(Q_LORA, HEADS, HD + QK_ROPE)
    wqb = jnp.concatenate([wqb[:, :, :HD].reshape(Q_LORA, HEADS * HD),
                           wqb[:, :, HD:].reshape(Q_LORA, HEADS * QK_ROPE)], axis=1)
    w_kv_b = mla_w_kv_b[0]
    cos_p, sin_p = _rope_tables(jnp.arange(seq))
    cos_s, sin_s = _rope_tables(jnp.full((db,), past_len))

    tmp = 512
    hp = matmul(xp, mla_w_in[0], tn=512)
    cq_p, ckv_p, rows_p, kr2_p = mla_prep(hp, mla_q_norm[0], mla_kv_norm[0], cos_p, sin_p,
                                          pos_blocks=seq // tmp, tm=tmp)
    q_p = matmul(cq_p, wqb, tn=512)
    qr_p = rope_q(q_p, cos_p, sin_p, pos_blocks=seq // tmp, tm=tmp)
    kvx_p = matmul(ckv_p, w_kv_b, tn=512, out_dtype=MXU_DT)
    att_p = mla_prompt_attention(q_p, qr_p, kvx_p, kr2_p, batch=batch, seq=seq)

    hs = matmul(xs, mla_w_in[0], tn=512)
    cq_s, _, rows_s, _ = mla_prep(hs, mla_q_norm[0], mla_kv_norm[0], cos_s, sin_s, pos_blocks=1, tm=db)
    q_s = matmul(cq_s, wqb, tn=512)
    qr_s = rope_q(q_s, cos_s, sin_s, pos_blocks=1, tm=db)
    q_lat = mla_absorb_q(q_s, w_kv_b).transpose(1, 0, 2)
    cache_mla = jnp.swapaxes(cache_mla_kv, 2, 3).reshape(cache_mla_kv.shape[0] * n_pool, MLA_ROW, page_size)
    o_lat = mla_decode(q_lat, qr_s.reshape(db, HEADS, QK_ROPE), rows_s.reshape(db, 1, MLA_ROW),
                       page_table, cache_mla)
    att_s = mla_absorb_o(o_lat.transpose(1, 0, 2), w_kv_b)

    xp = dense_tail(xp, att_p, mla_w_o, pp, 1)
    xs = dense_tail(xs, att_s, mla_w_o, ps, 1)

    return (xp.reshape(batch, seq, d), xs.reshape(db, 1, d),
            nsa_kv_prompt, nsa_win_prompt, rows_p.reshape(1, batch, seq, MLA_ROW),
            nsa_kv_sample, nsa_win_sample, rows_s.reshape(1, db, 1, MLA_ROW))
```
